```python
import jax, jax.numpy as jnp
from jax import lax
import numpy as np

D_MODEL = 4096
BATCH = 2
SEQ = 4096
DEPTH = 2

GRID_W = 64
CTX_LEN = 256
NA_HEADS = 16
NA_HEAD_DIM = 128
NA_WIDTH = NA_HEADS * NA_HEAD_DIM
WIN_H = 8
WIN_W = 16
GMLP_GROUPS = 16
GMLP_WIDTH = 2048
GMLP_GROUP_DIM = GMLP_WIDTH // GMLP_GROUPS
CHUNK = 128
N_EXPERTS = 16
EXPERT_FF = D_MODEL // 4
CAPACITY_FACTOR = 2
N_MOD = 6
EPS = 1e-6
OFF_Q = 0
OFF_K = OFF_Q + NA_WIDTH
OFF_V = OFF_K + NA_WIDTH
OFF_U = OFF_V + NA_WIDTH
OFF_GV = OFF_U + GMLP_WIDTH
OFF_GA = OFF_GV + GMLP_WIDTH
OFF_GB = OFF_GA + D_MODEL
IN_COLS = OFF_GB + D_MODEL

kernel_name = "hybrid_natten_gmlp_ec_dit_block"


def rms_norm(x, g):
    xf = x.astype(jnp.float32)
    y = xf * lax.rsqrt(jnp.mean(xf * xf, axis=-1, keepdims=True) + EPS)
    return (y * g.astype(jnp.float32)).astype(x.dtype)


def layer_norm(x, g):
    xf = x.astype(jnp.float32)
    mu = jnp.mean(xf, axis=-1, keepdims=True)
    xc = xf - mu
    y = xc * lax.rsqrt(jnp.mean(xc * xc, axis=-1, keepdims=True) + EPS)
    return (y * g.astype(jnp.float32)).astype(x.dtype)


def modulate(h, shift, scale):
    return h * (1 + scale) + shift


def split_heads(t):
    return t.reshape(t.shape[:-1] + (NA_HEADS, NA_HEAD_DIM))


def neighbourhood_attention(q, k, v, k_ctx, v_ctx, rpb):
    B, N, H, dh = q.shape
    rows = N // GRID_W
    kh = min(WIN_H, rows)
    nloc = kh * WIN_W
    q = q * (dh ** -0.5)
    qg = q.reshape(B, rows, GRID_W, H, dh)
    kg = k.reshape(B, rows, GRID_W, H, dh)
    vg = v.reshape(B, rows, GRID_W, H, dh)
    cols = jnp.arange(GRID_W)
    col_start = jnp.clip(cols - WIN_W // 2, 0, GRID_W - WIN_W)
    col_idx = col_start[:, None] + jnp.arange(WIN_W)[None, :]
    col_off = col_idx - cols[:, None] + (WIN_W - 1)
    rpb_cols = rpb[:, :, col_off]

    def row_block(r):
        rs = jnp.clip(r - kh // 2, 0, rows - kh)
        q_r = lax.dynamic_index_in_dim(qg, r, axis=1, keepdims=False)
        k_band = lax.dynamic_slice_in_dim(kg, rs, kh, axis=1)
        v_band = lax.dynamic_slice_in_dim(vg, rs, kh, axis=1)
        k_win = k_band[:, :, col_idx]
        v_win = v_band[:, :, col_idx]
        row_off = rs + jnp.arange(kh) - r + (WIN_H - 1)
        bias = jnp.transpose(rpb_cols[:, row_off], (0, 2, 1, 3))
        s_loc = jnp.einsum('bqhd,biqjhd->bhqij', q_r, k_win).astype(jnp.float32) + bias.astype(jnp.float32)
        s_ctx = jnp.einsum('bqhd,bchd->bhqc', q_r, k_ctx).astype(jnp.float32)
        s = jnp.concatenate([s_loc.reshape(B, H, GRID_W, nloc), s_ctx], axis=-1)
        p = jax.nn.softmax(s, axis=-1).astype(v.dtype)
        p_loc = p[..., :nloc].reshape(B, H, GRID_W, kh, WIN_W)
        p_ctx = p[..., nloc:]
        return (jnp.einsum('bhqij,biqjhd->bqhd', p_loc, v_win)
                + jnp.einsum('bhqc,bchd->bqhd', p_ctx, v_ctx))

    o = lax.map(row_block, jnp.arange(rows))
    return jnp.moveaxis(o, 0, 1).reshape(B, N, H * dh)


def context_attention(q, k, v):
    B, Nc, H, dh = q.shape
    s = jnp.einsum('bqhd,bkhd->bhqk', q * (dh ** -0.5), k).astype(jnp.float32)
    p = jax.nn.softmax(s, axis=-1).astype(v.dtype)
    return jnp.einsum('bhqk,bkhd->bqhd', p, v).reshape(B, Nc, H * dh)


def spatial_gating(u, v, norm_g, ws, bs):
    B, N, _ = u.shape
    u = jax.nn.gelu(u)
    v = layer_norm(jax.nn.gelu(v), norm_g)
    vc = v.reshape(B, N // CHUNK, CHUNK, GMLP_GROUPS, GMLP_GROUP_DIM)
    mixed = jnp.einsum('gpq,bnqgc->bnpgc', ws, vc) + jnp.transpose(bs)[:, :, None]
    return u * mixed.reshape(B, N, GMLP_WIDTH)


def gated_merge(o_a, o_b, gate_a, gate_b, w_branch_a, w_branch_b, w_out):
    y = jax.nn.sigmoid(gate_a) * (o_a @ w_branch_a) + jax.nn.sigmoid(gate_b) * (o_b @ w_branch_b)
    return y @ w_out


def hybrid_mixer(h, h_ctx, w_in, rpb, gmlp_norm_g, gmlp_ws, gmlp_bs, w_branch_a, w_branch_b, w_out,
                 with_ctx_out):
    p = h @ w_in
    if with_ctx_out:
        pc = h_ctx @ w_in
        k_ctx = split_heads(pc[..., OFF_K:OFF_V])
        v_ctx = split_heads(pc[..., OFF_V:OFF_U])
    else:
        kv = h_ctx @ w_in[:, OFF_K:OFF_U]
        k_ctx = split_heads(kv[..., :NA_WIDTH])
        v_ctx = split_heads(kv[..., NA_WIDTH:])
    o_a = neighbourhood_attention(split_heads(p[..., OFF_Q:OFF_K]), split_heads(p[..., OFF_K:OFF_V]),
                                  split_heads(p[..., OFF_V:OFF_U]), k_ctx, v_ctx, rpb)
    o_b = spatial_gating(p[..., OFF_U:OFF_GV], p[..., OFF_GV:OFF_GA], gmlp_norm_g, gmlp_ws, gmlp_bs)
    y_lat = gated_merge(o_a, o_b, p[..., OFF_GA:OFF_GB], p[..., OFF_GB:IN_COLS],
                        w_branch_a, w_branch_b, w_out)
    if not with_ctx_out:
        return y_lat, None
    oc_a = context_attention(split_heads(pc[..., OFF_Q:OFF_K]), k_ctx, v_ctx)
    oc_b = spatial_gating(pc[..., OFF_U:OFF_GV], pc[..., OFF_GV:OFF_GA], gmlp_norm_g, gmlp_ws, gmlp_bs)
    y_ctx = gated_merge(oc_a, oc_b, pc[..., OFF_GA:OFF_GB], pc[..., OFF_GB:IN_COLS],
                        w_branch_a, w_branch_b, w_out)
    return y_lat, y_ctx


def expert_choice_ffn(h, router_w, w1, w3, w2):
    B, N, _ = h.shape
    cap = CAPACITY_FACTOR * N // N_EXPERTS
    aff = jax.nn.softmax((h @ router_w).astype(jnp.float32), axis=-1)
    g, idx = lax.top_k(jnp.transpose(aff, (0, 2, 1)), cap)
    bidx = jnp.arange(B)[:, None, None]
    xs = h[bidx, idx]
    hid = jax.nn.silu(jnp.einsum('becd,edf->becf', xs, w1)) * jnp.einsum('becd,edf->becf', xs, w3)
    out = jnp.einsum('becf,efd->becd', hid, w2) * g.astype(h.dtype)[..., None]
    return jnp.zeros_like(h).at[bidx, idx].add(out)


def setup_inputs(seed: int = 0) -> dict:
    key = jax.random.key(seed)
    ks = jax.random.split(key, 24)
    f32 = jnp.float32

    def nrm(k, shape, scale):
        return jax.random.normal(k, shape, f32) * scale

    D = D_MODEL
    return {
        "x": nrm(ks[0], (BATCH, SEQ, D), 1.0),
        "c": nrm(ks[1], (BATCH, D), 1.0),
        "ctx": nrm(ks[2], (BATCH, CTX_LEN, D), 1.0),
        "c_ctx": nrm(ks[3], (D,), 1.0),
        "ada_w": nrm(ks[4], (DEPTH, D, N_MOD * D), 0.3 * D ** -0.5),
        "ada_b": nrm(ks[5], (DEPTH, N_MOD * D), 0.02),
        "norm1_g": 1.0 + nrm(ks[6], (DEPTH, D), 0.02),
        "norm2_g": 1.0 + nrm(ks[7], (DEPTH, D), 0.02),
        "w_in": nrm(ks[8], (DEPTH, D, IN_COLS), D ** -0.5),
        "na_rpb": nrm(ks[9], (DEPTH, NA_HEADS, 2 * WIN_H - 1, 2 * WIN_W - 1), 0.1),
        "gmlp_norm_g": 1.0 + nrm(ks[10], (DEPTH, GMLP_WIDTH), 0.02),
        "gmlp_ws": nrm(ks[11], (DEPTH, GMLP_GROUPS, CHUNK, CHUNK), CHUNK ** -0.5),
        "gmlp_bs": 1.0 + nrm(ks[12], (DEPTH, GMLP_GROUPS, CHUNK), 0.02),
        "w_branch_a": nrm(ks[13], (DEPTH, NA_WIDTH, D), NA_WIDTH ** -0.5),
        "w_branch_b": nrm(ks[14], (DEPTH, GMLP_WIDTH, D), GMLP_WIDTH ** -0.5),
        "w_out": nrm(ks[15], (DEPTH, D, D), D ** -0.5),
        "router_w": nrm(ks[16], (DEPTH, D, N_EXPERTS), D ** -0.5),
        "exp_w1": nrm(ks[17], (DEPTH, N_EXPERTS, D, EXPERT_FF), D ** -0.5),
        "exp_w3": nrm(ks[18], (DEPTH, N_EXPERTS, D, EXPERT_FF), D ** -0.5),
        "exp_w2": nrm(ks[19], (DEPTH, N_EXPERTS, EXPERT_FF, D), EXPERT_FF ** -0.5),
        "final_norm_g": 1.0 + nrm(ks[20], (D,), 0.02),
    }


def reference(x, c, ctx, c_ctx, ada_w, ada_b, norm1_g, norm2_g, w_in, na_rpb, gmlp_norm_g, gmlp_ws,
              gmlp_bs, w_branch_a, w_branch_b, w_out, router_w, exp_w1, exp_w3, exp_w2, final_norm_g):
    D = D_MODEL
    x_lat, x_ctx = x, ctx
    silu_c = jax.nn.silu(c)
    silu_cc = jax.nn.silu(c_ctx)
    for layer in range(DEPTH):
        last = layer == DEPTH - 1
        mod = (silu_c @ ada_w[layer] + ada_b[layer])[:, None, :]
        sh1, sc1, gt1, sh2, sc2, gt2 = jnp.split(mod, N_MOD, axis=-1)
        n_ctx_mod = 2 if last else N_MOD
        mod_c = silu_cc @ ada_w[layer][:, :n_ctx_mod * D] + ada_b[layer][:n_ctx_mod * D]
        mod_c = jnp.split(mod_c, n_ctx_mod, axis=-1)

        h = modulate(rms_norm(x_lat, norm1_g[layer]), sh1, sc1)
        h_ctx = modulate(rms_norm(x_ctx, norm1_g[layer]), mod_c[0], mod_c[1])
        y_lat, y_ctx = hybrid_mixer(h, h_ctx, w_in[layer], na_rpb[layer], gmlp_norm_g[layer],
                                    gmlp_ws[layer], gmlp_bs[layer], w_branch_a[layer],
                                    w_branch_b[layer], w_out[layer], not last)
        x_lat = x_lat + gt1 * y_lat

        h = modulate(rms_norm(x_lat, norm2_g[layer]), sh2, sc2)
        x_lat = x_lat + gt2 * expert_choice_ffn(h, router_w[layer], exp_w1[layer], exp_w3[layer],
                                                exp_w2[layer])
        if not last:
            x_ctx = x_ctx + mod_c[2] * y_ctx
            h_ctx = modulate(rms_norm(x_ctx, norm2_g[layer]), mod_c[3], mod_c[4])
            x_ctx = x_ctx + mod_c[5] * expert_choice_ffn(h_ctx, router_w[layer], exp_w1[layer],
                                                         exp_w3[layer], exp_w2[layer])
    return rms_norm(x_lat, final_norm_g)
```

```python
import functools

import jax
import jax.numpy as jnp
from jax import lax
from jax.experimental import pallas as pl
from jax.experimental.pallas import tpu as pltpu

F32, BF16, I32 = jnp.float32, jnp.bfloat16, jnp.int32

GRID_W = 64
NA_HEADS = 16
NA_HEAD_DIM = 128
NA_WIDTH = NA_HEADS * NA_HEAD_DIM
WIN_H = 8
WIN_W = 16
GMLP_GROUPS = 16
GMLP_WIDTH = 2048
CHUNK = 128
N_EXPERTS = 16
CAPACITY_FACTOR = 2
N_MOD = 6
EPS = 1e-6
MASK_BIAS = -1e30

V7X_VMEM_LIMIT_BYTES = 56 * 1024 * 1024
LANES = 128
MOD_ROWS = 8


def _params(*sem):
    return pltpu.CompilerParams(dimension_semantics=sem, vmem_limit_bytes=V7X_VMEM_LIMIT_BYTES)


def _cast_weight(w_ref, wb_ref):
    rows = w_ref.shape[0]
    step = min(rows, 512)

    def body(i, carry):
        r = pl.multiple_of(i * step, step)
        wb_ref[pl.ds(r, step), :] = w_ref[pl.ds(r, step), :].astype(BF16)
        return carry

    lax.fori_loop(0, rows // step, body, 0)


def _ada_kernel(c_ref, w_ref, b_ref, o_ref):
    c = c_ref[...]
    s = c * jax.nn.sigmoid(c)
    o_ref[...] = jnp.dot(s.astype(BF16), w_ref[...].astype(BF16), preferred_element_type=F32) + b_ref[...]


def ada_mod(cv, ada_w, ada_b, layer):
    _, d, m = ada_w.shape
    tn = min(m, 512)
    return pl.pallas_call(
        _ada_kernel,
        grid=(m // tn,),
        in_specs=[
            pl.BlockSpec((MOD_ROWS, d), lambda j: (0, 0)),
            pl.BlockSpec((None, d, tn), lambda j: (layer, 0, j)),
            pl.BlockSpec((None, 1, tn), lambda j: (layer, 0, j)),
        ],
        out_specs=pl.BlockSpec((MOD_ROWS, tn), lambda j: (0, j)),
        out_shape=jax.ShapeDtypeStruct((MOD_ROWS, m), F32),
        compiler_params=_params("arbitrary"),
        name="ada_mod",
    )(cv, ada_w, ada_b.reshape(ada_b.shape[0], 1, m))


class Rows:
    def __init__(self, batch, n_lat, n_ctx):
        self.batch, self.n_lat, self.n_ctx = batch, n_lat, n_ctx
        self.lat_rows = batch * n_lat
        self.all_rows = self.lat_rows + batch * n_ctx

    def mod_row(self, tile_rows):
        lat_tiles = self.lat_rows // tile_rows
        per_batch = self.n_lat // tile_rows
        batch = self.batch

        def f(i):
            return jnp.where(i < lat_tiles, i // per_batch, batch)

        return f


def _mod_spec(mod_row, k, d):
    return pl.BlockSpec((1, 1, d), lambda i: (mod_row(i), 0, k))


def _rms(x, g):
    return x * lax.rsqrt(jnp.mean(x * x, axis=-1, keepdims=True) + EPS) * g


def _norm_mod_kernel(x_ref, g_ref, sh_ref, sc_ref, o_ref):
    y = _rms(x_ref[...], g_ref[...])
    o_ref[...] = (y * (1.0 + sc_ref[0]) + sh_ref[0]).astype(o_ref.dtype)


def norm_mod(x, g, layer, mod3, rows, k_shift, k_scale, row0, nrows, out_dtype):
    d = x.shape[1]
    tr = 256
    mod_row = rows.mod_row(tr)
    t0 = row0 // tr
    return pl.pallas_call(
        _norm_mod_kernel,
        grid=(nrows // tr,),
        in_specs=[
            pl.BlockSpec((tr, d), lambda i: (i + t0, 0)),
            pl.BlockSpec((None, 1, d), lambda i: (layer, 0, 0)),
            pl.BlockSpec((1, 1, d), lambda i: (mod_row(i + t0), 0, k_shift)),
            pl.BlockSpec((1, 1, d), lambda i: (mod_row(i + t0), 0, k_scale)),
        ],
        out_specs=pl.BlockSpec((tr, d), lambda i: (i, 0)),
        out_shape=jax.ShapeDtypeStruct((nrows, d), out_dtype),
        compiler_params=_params("arbitrary"),
        name="norm_mod",
    )(x, g.reshape(g.shape[0], 1, d), mod3, mod3)


def _final_norm_kernel(x_ref, g_ref, o_ref):
    o_ref[...] = _rms(x_ref[...], g_ref[...])


def final_norm(x, g, nrows):
    d = x.shape[1]
    tr = 256
    return pl.pallas_call(
        _final_norm_kernel,
        grid=(nrows // tr,),
        in_specs=[pl.BlockSpec((tr, d), lambda i: (i, 0)), pl.BlockSpec((1, d), lambda i: (0, 0))],
        out_specs=pl.BlockSpec((tr, d), lambda i: (i, 0)),
        out_shape=jax.ShapeDtypeStruct((nrows, d), F32),
        compiler_params=_params("arbitrary"),
        name="final_norm",
    )(x, g.reshape(1, d))


def _mm_kernel(a_ref, w_ref, o_ref, wb_ref):
    @pl.when(pl.program_id(1) == 0)
    def _():
        _cast_weight(w_ref, wb_ref)

    o_ref[...] = jnp.dot(a_ref[...], wb_ref[...], preferred_element_type=F32).astype(o_ref.dtype)


def matmul(a, w, layer, *, row0, nrows, col0, ncols, out_dtype, tm=512, tn=512):
    k = a.shape[1]
    tm, tn = min(tm, nrows), min(tn, ncols)
    r0, c0 = row0 // tm, col0 // tn
    return pl.pallas_call(
        _mm_kernel,
        grid=(ncols // tn, nrows // tm),
        in_specs=[
            pl.BlockSpec((tm, k), lambda j, i: (i + r0, 0)),
            pl.BlockSpec((None, k, tn), lambda j, i: (layer, 0, j + c0)),
        ],
        out_specs=pl.BlockSpec((tm, tn), lambda j, i: (i, j)),
        out_shape=jax.ShapeDtypeStruct((nrows, ncols), out_dtype),
        scratch_shapes=[pltpu.VMEM((k, tn), BF16)],
        compiler_params=_params("arbitrary", "arbitrary"),
        name="matmul",
    )(a, w)


def _merge_kernel(oa_ref, ob_ref, wa_ref, wb_ref, ga_ref, gb_ref, y_ref, wab_ref, wbb_ref):
    @pl.when(pl.program_id(1) == 0)
    def _():
        _cast_weight(wa_ref, wab_ref)
        _cast_weight(wb_ref, wbb_ref)

    ya = jnp.dot(oa_ref[...], wab_ref[...], preferred_element_type=F32)
    yb = jnp.dot(ob_ref[...], wbb_ref[...], preferred_element_type=F32)
    y = jax.nn.sigmoid(ga_ref[...]) * ya + jax.nn.sigmoid(gb_ref[...]) * yb
    y_ref[...] = y.astype(y_ref.dtype)


def gated_merge(o_a, o_b, w_a, w_b, gates, layer, *, nrows, ga_col0, gb_col0, tm=512, tn=512):
    ka, kb = o_a.shape[1], o_b.shape[1]
    d = w_a.shape[2]
    tm, tn = min(tm, nrows), min(tn, d)
    ca, cb = ga_col0 // tn, gb_col0 // tn
    return pl.pallas_call(
        _merge_kernel,
        grid=(d // tn, nrows // tm),
        in_specs=[
            pl.BlockSpec((tm, ka), lambda j, i: (i, 0)),
            pl.BlockSpec((tm, kb), lambda j, i: (i, 0)),
            pl.BlockSpec((None, ka, tn), lambda j, i: (layer, 0, j)),
            pl.BlockSpec((None, kb, tn), lambda j, i: (layer, 0, j)),
            pl.BlockSpec((tm, tn), lambda j, i: (i, j + ca)),
            pl.BlockSpec((tm, tn), lambda j, i: (i, j + cb)),
        ],
        out_specs=pl.BlockSpec((tm, tn), lambda j, i: (i, j)),
        out_shape=jax.ShapeDtypeStruct((nrows, d), BF16),
        scratch_shapes=[pltpu.VMEM((ka, tn), BF16), pltpu.VMEM((kb, tn), BF16)],
        compiler_params=_params("arbitrary", "arbitrary"),
        name="gated_merge",
    )(o_a, o_b, w_a, w_b, gates, gates)


def _outproj_kernel(y_ref, w_ref, x_ref, gt_ref, o_ref, wb_ref):
    @pl.when(pl.program_id(1) == 0)
    def _():
        _cast_weight(w_ref, wb_ref)

    o_ref[...] = x_ref[...] + gt_ref[0] * jnp.dot(y_ref[...], wb_ref[...], preferred_element_type=F32)


def outproj_residual(y, w_out, x, mod3, layer, rows, k_gate, *, nrows, tm=512, tn=512):
    k = y.shape[1]
    d = w_out.shape[2]
    tm, tn = min(tm, nrows), min(tn, d)
    mod_row = rows.mod_row(tm)
    gate_col0 = k_gate * (d // tn)
    return pl.pallas_call(
        _outproj_kernel,
        grid=(d // tn, nrows // tm),
        in_specs=[
            pl.BlockSpec((tm, k), lambda j, i: (i, 0)),
            pl.BlockSpec((None, k, tn), lambda j, i: (layer, 0, j)),
            pl.BlockSpec((tm, tn), lambda j, i: (i, j)),
            pl.BlockSpec((1, 1, tn), lambda j, i: (mod_row(i), 0, j + gate_col0)),
        ],
        out_specs=pl.BlockSpec((tm, tn), lambda j, i: (i, j)),
        out_shape=jax.ShapeDtypeStruct((nrows, d), F32),
        scratch_shapes=[pltpu.VMEM((k, tn), BF16)],
        compiler_params=_params("arbitrary", "arbitrary"),
        name="outproj_residual",
    )(y, w_out, x, mod3)


N_BIAS_ROWS = WIN_H + 1


def build_bias_table(rpb):
    half = WIN_H // 2
    cols = jnp.arange(GRID_W)
    col_start = jnp.clip(cols - WIN_W // 2, 0, GRID_W - WIN_W)
    key_col = jnp.arange(GRID_W)
    in_win = (key_col[None, :] >= col_start[:, None]) & (key_col[None, :] < col_start[:, None] + WIN_W)
    col_off = jnp.clip(key_col[None, :] - cols[:, None] + (WIN_W - 1), 0, 2 * WIN_W - 2)
    bi = jnp.arange(N_BIAS_ROWS)[:, None]
    band_row = jnp.arange(WIN_H)[None, :]
    row_off = jnp.where(bi <= half, band_row - bi + (WIN_H - 1), band_row + WIN_H - bi)
    t = rpb[:, row_off]
    t = t[..., col_off]
    t = jnp.where(in_win[None, None, None], t, MASK_BIAS)
    t = jnp.transpose(t, (0, 1, 3, 2, 4))
    return t.reshape(rpb.shape[0], N_BIAS_ROWS, GRID_W, WIN_H * GRID_W).astype(F32)


def _softmax_pv(parts):
    m = functools.reduce(jnp.maximum, [jnp.max(s, axis=-1, keepdims=True) for s, _ in parts])
    es = [jnp.exp(s - m) for s, _ in parts]
    denom = functools.reduce(jnp.add, [jnp.sum(e, axis=-1, keepdims=True) for e in es])
    acc = functools.reduce(jnp.add, [jnp.dot(e.astype(BF16), v, preferred_element_type=F32)
                                     for e, (_, v) in zip(es, parts)])
    return acc / denom


def _qk(q, k):
    return lax.dot_general(q, k, (((1,), (1,)), ((), ())), preferred_element_type=F32) * (NA_HEAD_DIM ** -0.5)


def _natten_kernel(q_ref, k_ref, v_ref, kc_ref, vc_ref, bt_ref, o_ref, *, grid_rows):
    half = WIN_H // 2
    band = WIN_H * GRID_W
    kc = kc_ref[...]
    vc = vc_ref[...]

    def body(r, carry):
        rs = jnp.clip(r - half, 0, grid_rows - WIN_H)
        bi = jnp.clip(r, 0, half) + jnp.clip(r - (grid_rows - half - 1), 0, half)
        q0 = pl.multiple_of(r * GRID_W, GRID_W)
        k0 = pl.multiple_of(rs * GRID_W, GRID_W)
        q = q_ref[pl.ds(q0, GRID_W), :]
        kb = k_ref[pl.ds(k0, band), :]
        vb = v_ref[pl.ds(k0, band), :]
        s_loc = _qk(q, kb) + bt_ref[bi]
        s_ctx = _qk(q, kc)
        o_ref[pl.ds(q0, GRID_W), :] = _softmax_pv([(s_loc, vb), (s_ctx, vc)]).astype(o_ref.dtype)
        return carry

    lax.fori_loop(0, grid_rows, body, 0)


def neighbourhood_attention(qkv, kvc, bias_table, *, batch, n_lat, n_ctx, kc_row0, kc_col0, vc_col0):
    dh, h = NA_HEAD_DIM, NA_HEADS
    grid_rows = n_lat // GRID_W
    rb0, ck, cv = kc_row0 // n_ctx, kc_col0 // dh, vc_col0 // dh
    return pl.pallas_call(
        functools.partial(_natten_kernel, grid_rows=grid_rows),
        grid=(batch, h),
        in_specs=[
            pl.BlockSpec((n_lat, dh), lambda b, hh: (b, hh)),
            pl.BlockSpec((n_lat, dh), lambda b, hh: (b, h + hh)),
            pl.BlockSpec((n_lat, dh), lambda b, hh: (b, 2 * h + hh)),
            pl.BlockSpec((n_ctx, dh), lambda b, hh: (rb0 + b, ck + hh)),
            pl.BlockSpec((n_ctx, dh), lambda b, hh: (rb0 + b, cv + hh)),
            pl.BlockSpec((None, N_BIAS_ROWS, GRID_W, WIN_H * GRID_W), lambda b, hh: (hh, 0, 0, 0)),
        ],
        out_specs=pl.BlockSpec((n_lat, dh), lambda b, hh: (b, hh)),
        out_shape=jax.ShapeDtypeStruct((batch * n_lat, NA_WIDTH), BF16),
        compiler_params=_params("arbitrary", "arbitrary"),
        name="neighbourhood_attention",
    )(qkv, qkv, qkv, kvc, kvc, bias_table)


def _ctx_attn_kernel(q_ref, k_ref, v_ref, o_ref):
    o_ref[...] = _softmax_pv([(_qk(q_ref[...], k_ref[...]), v_ref[...])]).astype(o_ref.dtype)


def context_attention(qkv, *, batch, n_ctx, row0):
    dh, h = NA_HEAD_DIM, NA_HEADS
    rb0 = row0 // n_ctx
    return pl.pallas_call(
        _ctx_attn_kernel,
        grid=(batch, h),
        in_specs=[
            pl.BlockSpec((n_ctx, dh), lambda b, hh: (rb0 + b, hh)),
            pl.BlockSpec((n_ctx, dh), lambda b, hh: (rb0 + b, h + hh)),
            pl.BlockSpec((n_ctx, dh), lambda b, hh: (rb0 + b, 2 * h + hh)),
        ],
        out_specs=pl.BlockSpec((n_ctx, dh), lambda b, hh: (b, hh)),
        out_shape=jax.ShapeDtypeStruct((batch * n_ctx, NA_WIDTH), BF16),
        compiler_params=_params("arbitrary", "arbitrary"),
        name="context_attention",
    )(qkv, qkv, qkv)


def _gmlp_kernel(u_ref, gv_ref, ng_ref, ws_ref, bst_ref, o_ref):
    v = jax.nn.gelu(gv_ref[...])
    vc = v - jnp.mean(v, axis=-1, keepdims=True)
    y = vc * lax.rsqrt(jnp.mean(vc * vc, axis=-1, keepdims=True) + EPS) * ng_ref[...]
    yb = y.astype(BF16)
    gd = GMLP_WIDTH // GMLP_GROUPS
    for g in range(GMLP_GROUPS):
        sl = slice(g * gd, (g + 1) * gd)
        mixed = jnp.dot(ws_ref[g].astype(BF16), yb[:, sl], preferred_element_type=F32) + bst_ref[:, g:g + 1]
        o_ref[:, sl] = (jax.nn.gelu(u_ref[:, sl]) * mixed).astype(o_ref.dtype)


def spatial_gating(p, norm_g, ws, bs, layer, *, nrows, u_col0, gv_col0):
    w = GMLP_WIDTH
    bst = jnp.transpose(bs[layer])
    return pl.pallas_call(
        _gmlp_kernel,
        grid=(nrows // CHUNK,),
        in_specs=[
            pl.BlockSpec((CHUNK, w), lambda i: (i, u_col0 // w)),
            pl.BlockSpec((CHUNK, w), lambda i: (i, gv_col0 // w)),
            pl.BlockSpec((None, 1, w), lambda i: (layer, 0, 0)),
            pl.BlockSpec((None, GMLP_GROUPS, CHUNK, CHUNK), lambda i: (layer, 0, 0, 0)),
            pl.BlockSpec((CHUNK, GMLP_GROUPS), lambda i: (0, 0)),
        ],
        out_specs=pl.BlockSpec((CHUNK, w), lambda i: (i, 0)),
        out_shape=jax.ShapeDtypeStruct((nrows, w), BF16),
        compiler_params=_params("arbitrary"),
        name="spatial_gating",
    )(p, p, norm_g.reshape(norm_g.shape[0], 1, w), ws, bst)


def _split_bf16(x):
    hi = x.astype(BF16)
    return hi, (x - hi.astype(F32)).astype(BF16)


def _norm_router_kernel(x_ref, g_ref, sh_ref, sc_ref, rw_ref, h_ref, lg_ref):
    h = _rms(x_ref[...], g_ref[...]) * (1.0 + sc_ref[0]) + sh_ref[0]
    h_ref[...] = h
    hh, hl = _split_bf16(h)
    wh, wl = _split_bf16(rw_ref[...])
    lg = (jnp.dot(hh, wh, preferred_element_type=F32) + jnp.dot(hl, wh, preferred_element_type=F32)
          + jnp.dot(hh, wl, preferred_element_type=F32))
    lg_ref[...] = jnp.transpose(lg)[:N_EXPERTS, :]


def norm_mod_router(x, g, layer, mod3, rows, k_shift, k_scale, router_w, nrows):
    d = x.shape[1]
    tr = 256
    mod_row = rows.mod_row(tr)
    rw = jnp.pad(router_w[layer], ((0, 0), (0, LANES - N_EXPERTS)))
    return pl.pallas_call(
        _norm_router_kernel,
        grid=(nrows // tr,),
        in_specs=[
            pl.BlockSpec((tr, d), lambda i: (i, 0)),
            pl.BlockSpec((None, 1, d), lambda i: (layer, 0, 0)),
            pl.BlockSpec((1, 1, d), lambda i: (mod_row(i), 0, k_shift)),
            pl.BlockSpec((1, 1, d), lambda i: (mod_row(i), 0, k_scale)),
            pl.BlockSpec((d, LANES), lambda i: (0, 0)),
        ],
        out_specs=[pl.BlockSpec((tr, d), lambda i: (i, 0)), pl.BlockSpec((N_EXPERTS, tr), lambda i: (0, i))],
        out_shape=[jax.ShapeDtypeStruct((nrows, d), F32), jax.ShapeDtypeStruct((N_EXPERTS, nrows), F32)],
        compiler_params=_params("arbitrary"),
        name="norm_mod_router",
    )(x, g.reshape(g.shape[0], 1, d), mod3, mod3, rw)


def _cumsum_lanes(x):
    r, n = x.shape
    blk = min(n, LANES)
    tri = (lax.broadcasted_iota(I32, (blk, blk), 0) <= lax.broadcasted_iota(I32, (blk, blk), 1)).astype(BF16)
    off = jnp.zeros((r, 1), F32)
    out = []
    for j in range(n // blk):
        cs = jnp.dot(x[:, j * blk:(j + 1) * blk].astype(BF16), tri, preferred_element_type=F32) + off
        out.append(cs)
        off = cs[:, blk - 1:blk]
    return jnp.concatenate(out, axis=1) if len(out) > 1 else out[0]


def _route_kernel(lg_ref, idx_ref, dest_ref, gate_ref, start_ref, cnt_ref, slot_s, aff_s, *, cap):
    e_n, n = lg_ref.shape
    lg = lg_ref[...]
    ex = jnp.exp(lg - jnp.max(lg, axis=0, keepdims=True))
    aff = ex / jnp.sum(ex, axis=0, keepdims=True)
    bits = lax.bitcast_convert_type(aff, I32)

    thr = jnp.zeros((e_n, 1), I32)
    for bit in range(30, -1, -1):
        cand = thr | (1 << bit)
        n_ge = jnp.sum((bits >= cand).astype(F32), axis=1, keepdims=True)
        thr = jnp.where(n_ge >= cap, cand, thr)
    above = bits > thr
    tied = (bits == thr).astype(F32)
    need = cap - jnp.sum(above.astype(F32), axis=1, keepdims=True)
    tied_before = _cumsum_lanes(tied) - tied
    sel = jnp.where(above, 1.0, jnp.where(tied_before < need, tied, 0.0))

    slot = _cumsum_lanes(sel) - sel
    per_tok = jnp.sum(sel, axis=0, keepdims=True)
    per_tok8 = jnp.broadcast_to(per_tok, (8, n))
    start = (_cumsum_lanes(per_tok8) - per_tok8)[0:1]
    start_ref[...] = start.astype(I32)
    cnt_ref[...] = per_tok.astype(I32)
    slot_s[...] = jnp.where(sel > 0.0, slot, -1.0)
    aff_s[...] = aff

    tok = lax.broadcasted_iota(I32, (1, n), 1).astype(F32)
    cc = min(cap, LANES)

    def body(e, chosen_by_earlier):
        srow = slot_s[pl.ds(e, 1), :]
        arow = aff_s[pl.ds(e, 1), :]
        drow = start + chosen_by_earlier
        for c0 in range(0, cap, cc):
            want = (lax.broadcasted_iota(I32, (cc, 1), 0) + c0).astype(F32)
            hit = srow == want
            pick = lambda row: jnp.sum(jnp.where(hit, row, 0.0), axis=1, keepdims=True)
            idx_ref[e, pl.ds(c0, cc), :] = pick(tok).astype(I32)
            dest_ref[e, pl.ds(c0, cc), :] = pick(drow).astype(I32)
            gate_ref[e, pl.ds(c0, cc), :] = pick(arow)
        return chosen_by_earlier + (srow >= 0.0).astype(F32)

    lax.fori_loop(0, e_n, body, jnp.zeros((1, n), F32))


def expert_choice_route(logits_t, *, batch, n, col0):
    e_n = logits_t.shape[0]
    cap = CAPACITY_FACTOR * n // e_n
    cb0 = col0 // n
    per_slot = lambda dt: jax.ShapeDtypeStruct((batch, e_n, cap, 1), dt)
    per_tok = jax.ShapeDtypeStruct((batch, 1, n), I32)
    slot_spec = pl.BlockSpec((None, e_n, cap, 1), lambda b: (b, 0, 0, 0))
    tok_spec = pl.BlockSpec((None, 1, n), lambda b: (b, 0, 0))
    return pl.pallas_call(
        functools.partial(_route_kernel, cap=cap),
        grid=(batch,),
        in_specs=[pl.BlockSpec((e_n, n), lambda b: (0, cb0 + b))],
        out_specs=[slot_spec, slot_spec, slot_spec, tok_spec, tok_spec],
        out_shape=[per_slot(I32), per_slot(I32), per_slot(F32), per_tok, per_tok],
        scratch_shapes=[pltpu.VMEM((e_n, n), F32), pltpu.VMEM((e_n, n), F32)],
        compiler_params=_params("arbitrary"),
        name="expert_choice_route",
    )(logits_t)


DMA_WINDOW = 16


def _permute_kernel(sidx_ref, didx_ref, src_ref, dst_ref, sem, *, n):
    def row_copy(i):
        return pltpu.make_async_copy(src_ref.at[pl.ds(sidx_ref[i], 1), :], dst_ref.at[pl.ds(didx_ref[i], 1), :], sem)

    def issue(i, carry):
        row_copy(i).start()

        @pl.when(i >= DMA_WINDOW)
        def _():
            row_copy(i - DMA_WINDOW).wait()

        return carry

    lax.fori_loop(0, n, issue, 0)

    def drain(i, carry):
        row_copy(i).wait()
        return carry

    lax.fori_loop(n - DMA_WINDOW, n, drain, 0)


def permute_rows(src, src_idx, dst_idx, n_dst):
    n = src_idx.shape[0]
    assert n == n_dst and n >= DMA_WINDOW
    return pl.pallas_call(
        functools.partial(_permute_kernel, n=n),
        grid_spec=pltpu.PrefetchScalarGridSpec(
            num_scalar_prefetch=2,
            grid=(1,),
            in_specs=[pl.BlockSpec(memory_space=pl.ANY)],
            out_specs=pl.BlockSpec(memory_space=pl.ANY),
            scratch_shapes=[pltpu.SemaphoreType.DMA(())],
        ),
        out_shape=jax.ShapeDtypeStruct((n_dst, src.shape[1]), src.dtype),
        compiler_params=_params("arbitrary"),
        name="permute_rows",
    )(src_idx, dst_idx, src)


def _ffn_kernel(xg_ref, w1_ref, w3_ref, w2_ref, g_ref, y_ref, xs_s, hid_s, *, f0, f1, mchunk, tf):
    s = pl.program_id(1)

    @pl.when(s < f0)
    def _():
        r = pl.multiple_of(s * mchunk, mchunk)
        xs_s[pl.ds(r, mchunk), :] = xg_ref[...].astype(BF16)

    @pl.when(jnp.logical_and(s >= f0, s < f0 + f1))
    def _():
        xs = xs_s[...]
        h1 = jnp.dot(xs, w1_ref[...].astype(BF16), preferred_element_type=F32)
        h3 = jnp.dot(xs, w3_ref[...].astype(BF16), preferred_element_type=F32)
        hid_s[s - f0] = (jax.nn.silu(h1) * h3).astype(BF16)

    @pl.when(s >= f0 + f1)
    def _():
        acc = jnp.dot(hid_s[0], w2_ref[0:tf, :].astype(BF16), preferred_element_type=F32)
        for f in range(1, f1):
            acc += jnp.dot(hid_s[f], w2_ref[f * tf:(f + 1) * tf, :].astype(BF16), preferred_element_type=F32)
        y_ref[...] = acc * g_ref[...]


def expert_ffn(xg, w1, w3, w2, gate, layer):
    e_n, m, d = xg.shape
    ff = w1.shape[3]
    tf, tn = min(ff, 256), min(d, 512)
    mchunk = min(m, 256)
    f0, f1, f2 = m // mchunk, ff // tf, d // tn
    return pl.pallas_call(
        functools.partial(_ffn_kernel, f0=f0, f1=f1, mchunk=mchunk, tf=tf),
        grid=(e_n, f0 + f1 + f2),
        in_specs=[
            pl.BlockSpec((None, mchunk, d), lambda e, s: (e, jnp.minimum(s, f0 - 1), 0)),
            pl.BlockSpec((None, None, d, tf), lambda e, s: (layer, e, 0, jnp.clip(s - f0, 0, f1 - 1))),
            pl.BlockSpec((None, None, d, tf), lambda e, s: (layer, e, 0, jnp.clip(s - f0, 0, f1 - 1))),
            pl.BlockSpec((None, None, ff, tn), lambda e, s: (layer, e, 0, jnp.maximum(s - f0 - f1, 0))),
            pl.BlockSpec((None, m, 1), lambda e, s: (e, 0, 0)),
        ],
        out_specs=pl.BlockSpec((None, m, tn), lambda e, s: (e, 0, jnp.maximum(s - f0 - f1, 0))),
        out_shape=jax.ShapeDtypeStruct((e_n, m, d), F32),
        scratch_shapes=[pltpu.VMEM((m, d), BF16), pltpu.VMEM((f1, m, tf), BF16)],
        compiler_params=_params("arbitrary", "arbitrary"),
        name="expert_ffn",
    )(xg, w1, w3, w2, gate)


def _combine_kernel(cf_ref, cn_ref, yg_ref, st_ref, ct_ref, x_ref, gt_ref, o_ref, acc_s, *, tiles, ch, kmax):
    b, t, k = pl.program_id(0), pl.program_id(1), pl.program_id(2)
    tile = b * tiles + t

    @pl.when(k == 0)
    def _():
        acc_s[...] = jnp.zeros_like(acc_s)

    @pl.when(k < cn_ref[tile])
    def _():
        pair = (cf_ref[tile] + k) * ch + lax.broadcasted_iota(I32, (1, ch), 1)
        rel = pair - st_ref[...]
        own = jnp.where(jnp.logical_and(rel >= 0, rel < ct_ref[...]), 1.0, 0.0).astype(BF16)
        yh, yl = _split_bf16(yg_ref[...])
        acc_s[...] += (jnp.dot(own, yh, preferred_element_type=F32) + jnp.dot(own, yl, preferred_element_type=F32))

    @pl.when(k == kmax - 1)
    def _():
        o_ref[...] = x_ref[...] + gt_ref[0] * acc_s[...]


def combine_residual(yg, start, cnt, x, mod3, rows, k_gate, *, batch, n, row0):
    d = x.shape[1]
    pairs = yg.shape[0] // batch
    tt = min(n, 128)
    ch = min(pairs, 256)
    tiles = n // tt
    kmax = tt * N_EXPERTS // ch + 1
    n_chunks = pairs // ch
    st = start.reshape(batch, n)
    ct = cnt.reshape(batch, n)
    first = st[:, ::tt]
    last = jnp.concatenate([first[:, 1:], jnp.full((batch, 1), pairs, I32)], axis=1)
    cfirst = jnp.minimum(first // ch, n_chunks - 1)
    cnum = jnp.where(last > first, (last - 1) // ch - first // ch + 1, 0)
    cfirst, cnum = cfirst.reshape(-1).astype(I32), cnum.reshape(-1).astype(I32)
    mod_row = rows.mod_row(tt)
    rt0 = row0 // tt
    gate_col = k_gate

    def yg_map(b, t, k, cf, cn):
        tile = b * tiles + t
        return (b * n_chunks + jnp.minimum(cf[tile] + k, cf[tile] + jnp.maximum(cn[tile] - 1, 0)), 0)

    row_map = lambda b, t, k, cf, cn: (rt0 + b * tiles + t, 0)
    tok_map = lambda b, t, k, cf, cn: (b * tiles + t, 0)
    return pl.pallas_call(
        functools.partial(_combine_kernel, tiles=tiles, ch=ch, kmax=kmax),
        grid_spec=pltpu.PrefetchScalarGridSpec(
            num_scalar_prefetch=2,
            grid=(batch, tiles, kmax),
            in_specs=[
                pl.BlockSpec((ch, d), yg_map),
                pl.BlockSpec((tt, 1), tok_map),
                pl.BlockSpec((tt, 1), tok_map),
                pl.BlockSpec((tt, d), row_map),
                pl.BlockSpec((1, 1, d), lambda b, t, k, cf, cn: (mod_row(rt0 + b * tiles + t), 0, gate_col)),
            ],
            out_specs=pl.BlockSpec((tt, d), row_map),
            scratch_shapes=[pltpu.VMEM((tt, d), F32)],
        ),
        out_shape=jax.ShapeDtypeStruct(x.shape, F32),
        input_output_aliases={5: 0},
        compiler_params=_params("arbitrary", "arbitrary", "arbitrary"),
        name="combine_residual",
    )(cfirst, cnum, yg, st.reshape(batch * n, 1), ct.reshape(batch * n, 1), x, mod3)


def expert_choice_ffn_residual(x, h, logits_t, mod3, rows, k_gate, w1, w3, w2, layer, *, batch, n, row0):
    e_n = logits_t.shape[0]
    d = x.shape[1]
    idx, dest, gate, start, cnt = expert_choice_route(logits_t, batch=batch, n=n, col0=row0)
    cap = idx.shape[2]
    pairs = e_n * cap
    boff = jnp.arange(batch, dtype=I32)[:, None, None]
    src_rows = jnp.transpose(idx[..., 0] + row0 + boff * n, (1, 0, 2)).reshape(-1)
    pair_rows = jnp.transpose(dest[..., 0] + boff * pairs, (1, 0, 2)).reshape(-1)
    gate_col = jnp.transpose(gate, (1, 0, 2, 3)).reshape(e_n, batch * cap, 1)
    slots = jnp.arange(batch * pairs, dtype=I32)
    xg = permute_rows(h, src_rows, slots, batch * pairs).reshape(e_n, batch * cap, d)
    y = expert_ffn(xg, w1, w3, w2, gate_col, layer).reshape(batch * pairs, d)
    yg = permute_rows(y, slots, pair_rows, batch * pairs)
    return combine_residual(yg, start, cnt, x, mod3, rows, k_gate, batch=batch, n=n, row0=row0)


def kernel(x, c, ctx, c_ctx, ada_w, ada_b, norm1_g, norm2_g, w_in, na_rpb, gmlp_norm_g, gmlp_ws, gmlp_bs,
           w_branch_a, w_branch_b, w_out, router_w, exp_w1, exp_w3, exp_w2, final_norm_g):
    batch, n_lat, d = x.shape
    n_ctx = ctx.shape[1]
    depth = ada_w.shape[0]
    rows = Rows(batch, n_lat, n_ctx)
    lat_rows, all_rows = rows.lat_rows, rows.all_rows
    off_k, off_v, off_u = NA_WIDTH, 2 * NA_WIDTH, 3 * NA_WIDTH
    off_gv = off_u + GMLP_WIDTH
    off_ga = off_gv + GMLP_WIDTH
    off_gb = off_ga + d
    in_cols = off_gb + d

    stream = jnp.concatenate([x.reshape(lat_rows, d), ctx.reshape(batch * n_ctx, d)], axis=0)
    cv = jnp.zeros((MOD_ROWS, d), F32).at[:batch].set(c).at[batch].set(c_ctx)

    for layer in range(depth):
        last = layer == depth - 1
        nrows = lat_rows if last else all_rows
        mod3 = ada_mod(cv, ada_w, ada_b, layer).reshape(MOD_ROWS, 1, N_MOD * d)

        h = norm_mod(stream, norm1_g, layer, mod3, rows, 0, 1, 0, nrows, BF16)
        qkv = matmul(h, w_in, layer, row0=0, nrows=nrows, col0=0, ncols=off_u, out_dtype=BF16)
        rest = matmul(h, w_in, layer, row0=0, nrows=nrows, col0=off_u, ncols=in_cols - off_u, out_dtype=F32)
        bias_table = build_bias_table(na_rpb[layer])
        if last:
            h_ctx = norm_mod(stream, norm1_g, layer, mod3, rows, 0, 1, lat_rows, batch * n_ctx, BF16)
            kvc = matmul(h_ctx, w_in, layer, row0=0, nrows=batch * n_ctx, col0=off_k, ncols=2 * NA_WIDTH,
                         out_dtype=BF16)
            o_a = neighbourhood_attention(qkv, kvc, bias_table, batch=batch, n_lat=n_lat, n_ctx=n_ctx,
                                          kc_row0=0, kc_col0=0, vc_col0=NA_WIDTH)
        else:
            o_lat = neighbourhood_attention(qkv, qkv, bias_table, batch=batch, n_lat=n_lat, n_ctx=n_ctx,
                                            kc_row0=lat_rows, kc_col0=off_k, vc_col0=off_v)
            o_ctx = context_attention(qkv, batch=batch, n_ctx=n_ctx, row0=lat_rows)
            o_a = jnp.concatenate([o_lat, o_ctx], axis=0)
        o_b = spatial_gating(rest, gmlp_norm_g, gmlp_ws, gmlp_bs, layer, nrows=nrows, u_col0=0,
                             gv_col0=off_gv - off_u)
        y = gated_merge(o_a, o_b, w_branch_a, w_branch_b, rest, layer, nrows=nrows,
                        ga_col0=off_ga - off_u, gb_col0=off_gb - off_u)
        stream = outproj_residual(y, w_out, stream, mod3, layer, rows, 2, nrows=nrows)

        h2, logits_t = norm_mod_router(stream, norm2_g, layer, mod3, rows, 3, 4, router_w, nrows)
        stream = expert_choice_ffn_residual(stream, h2, logits_t, mod3, rows, 5, exp_w1, exp_w3, exp_w2, layer,
                                            batch=batch, n=n_lat, row0=0)
        if not last:
            stream = expert_choice_ffn_residual(stream, h2, logits_t, mod3, rows, 5, exp_w1, exp_w3, exp_w2, layer,
                                                batch=batch, n=n_ctx, row0=lat_rows)

    return final_norm(stream, final_norm_g, lat_rows).reshape(batch, n_lat, d)
```

```python
import functools

import jax
import jax.numpy as jnp
from jax import lax
from jax.experimental import pallas as pl
from jax.experimental.pallas import tpu as pltpu

F32, BF16, I32 = jnp.float32, jnp.bfloat16, jnp.int32

GRID_W = 64
NA_HEADS = 16
NA_HEAD_DIM = 128
NA_WIDTH = NA_HEADS * NA_HEAD_DIM
WIN_H = 8
WIN_W = 16
GMLP_GROUPS = 16
GMLP_WIDTH = 2048
CHUNK = 128
N_EXPERTS = 16
CAPACITY_FACTOR = 2
N_MOD = 6
EPS = 1e-6
MASK_BIAS = -1e30

V7X_VMEM_LIMIT_BYTES = 56 * 1024 * 1024
LANES = 128
MOD_ROWS = 8


def _params(*sem):
    return pltpu.CompilerParams(dimension_semantics=sem, vmem_limit_bytes=V7X_VMEM_LIMIT_BYTES)


def _cast_weight(w_ref, wb_ref):
    rows = w_ref.shape[0]
    step = min(rows, 512)

    def body(i, carry):
        r = pl.multiple_of(i * step, step)
        wb_ref[pl.ds(r, step), :] = w_ref[pl.ds(r, step), :].astype(BF16)
        return carry

    lax.fori_loop(0, rows // step, body, 0)


def _ada_kernel(c_ref, w_ref, b_ref, o_ref):
    c = c_ref[...]
    s = c * jax.nn.sigmoid(c)
    o_ref[...] = jnp.dot(s.astype(BF16), w_ref[...].astype(BF16), preferred_element_type=F32) + b_ref[...]


def ada_mod(cv, ada_w, ada_b, layer):
    _, d, m = ada_w.shape
    tn = min(m, 512)
    return pl.pallas_call(
        _ada_kernel,
        grid=(m // tn,),
        in_specs=[
            pl.BlockSpec((MOD_ROWS, d), lambda j: (0, 0)),
            pl.BlockSpec((None, d, tn), lambda j: (layer, 0, j)),
            pl.BlockSpec((None, 1, tn), lambda j: (layer, 0, j)),
        ],
        out_specs=pl.BlockSpec((MOD_ROWS, tn), lambda j: (0, j)),
        out_shape=jax.ShapeDtypeStruct((MOD_ROWS, m), F32),
        compiler_params=_params("arbitrary"),
        name="ada_mod",
    )(cv, ada_w, ada_b.reshape(ada_b.shape[0], 1, m))


class Rows:
    def __init__(self, batch, n_lat, n_ctx):
        self.batch, self.n_lat, self.n_ctx = batch, n_lat, n_ctx
        self.lat_rows = batch * n_lat
        self.all_rows = self.lat_rows + batch * n_ctx

    def mod_row(self, tile_rows):
        lat_tiles = self.lat_rows // tile_rows
        per_batch = self.n_lat // tile_rows
        batch = self.batch

        def f(i):
            return jnp.where(i < lat_tiles, i // per_batch, batch)

        return f


def _mod_spec(mod_row, k, d):
    return pl.BlockSpec((1, 1, d), lambda i: (mod_row(i), 0, k))


def _rms(x, g):
    return x * lax.rsqrt(jnp.mean(x * x, axis=-1, keepdims=True) + EPS) * g


def _norm_mod_kernel(x_ref, g_ref, sh_ref, sc_ref, o_ref):
    y = _rms(x_ref[...], g_ref[...])
    o_ref[...] = (y * (1.0 + sc_ref[0]) + sh_ref[0]).astype(o_ref.dtype)


def norm_mod(x, g, layer, mod3, rows, k_shift, k_scale, row0, nrows, out_dtype):
    d = x.shape[1]
    tr = 256
    mod_row = rows.mod_row(tr)
    t0 = row0 // tr
    return pl.pallas_call(
        _norm_mod_kernel,
        grid=(nrows // tr,),
        in_specs=[
            pl.BlockSpec((tr, d), lambda i: (i + t0, 0)),
            pl.BlockSpec((None, 1, d), lambda i: (layer, 0, 0)),
            pl.BlockSpec((1, 1, d), lambda i: (mod_row(i + t0), 0, k_shift)),
            pl.BlockSpec((1, 1, d), lambda i: (mod_row(i + t0), 0, k_scale)),
        ],
        out_specs=pl.BlockSpec((tr, d), lambda i: (i, 0)),
        out_shape=jax.ShapeDtypeStruct((nrows, d), out_dtype),
        compiler_params=_params("arbitrary"),
        name="norm_mod",
    )(x, g.reshape(g.shape[0], 1, d), mod3, mod3)


def _final_norm_kernel(x_ref, g_ref, o_ref):
    o_ref[...] = _rms(x_ref[...], g_ref[...])


def final_norm(x, g, nrows):
    d = x.shape[1]
    tr = 256
    return pl.pallas_call(
        _final_norm_kernel,
        grid=(nrows // tr,),
        in_specs=[pl.BlockSpec((tr, d), lambda i: (i, 0)), pl.BlockSpec((1, d), lambda i: (0, 0))],
        out_specs=pl.BlockSpec((tr, d), lambda i: (i, 0)),
        out_shape=jax.ShapeDtypeStruct((nrows, d), F32),
        compiler_params=_params("arbitrary"),
        name="final_norm",
    )(x, g.reshape(1, d))


def _mm_kernel(a_ref, w_ref, o_ref, wb_ref):
    @pl.when(pl.program_id(1) == 0)
    def _():
        _cast_weight(w_ref, wb_ref)

    o_ref[...] = jnp.dot(a_ref[...], wb_ref[...], preferred_element_type=F32).astype(o_ref.dtype)


def matmul(a, w, layer, *, row0, nrows, col0, ncols, out_dtype, tm=512, tn=512):
    k = a.shape[1]
    tm, tn = min(tm, nrows), min(tn, ncols)
    r0, c0 = row0 // tm, col0 // tn
    return pl.pallas_call(
        _mm_kernel,
        grid=(ncols // tn, nrows // tm),
        in_specs=[
            pl.BlockSpec((tm, k), lambda j, i: (i + r0, 0)),
            pl.BlockSpec((None, k, tn), lambda j, i: (layer, 0, j + c0)),
        ],
        out_specs=pl.BlockSpec((tm, tn), lambda j, i: (i, j)),
        out_shape=jax.ShapeDtypeStruct((nrows, ncols), out_dtype),
        scratch_shapes=[pltpu.VMEM((k, tn), BF16)],
        compiler_params=_params("arbitrary", "arbitrary"),
        name="matmul",
    )(a, w)


def _merge_kernel(oa_ref, ob_ref, wa_ref, wb_ref, ga_ref, gb_ref, y_ref, wab_ref, wbb_ref):
    @pl.when(pl.program_id(1) == 0)
    def _():
        _cast_weight(wa_ref, wab_ref)
        _cast_weight(wb_ref, wbb_ref)

    ya = jnp.dot(oa_ref[...], wab_ref[...], preferred_element_type=F32)
    yb = jnp.dot(ob_ref[...], wbb_ref[...], preferred_element_type=F32)
    y = jax.nn.sigmoid(ga_ref[...]) * ya + jax.nn.sigmoid(gb_ref[...]) * yb
    y_ref[...] = y.astype(y_ref.dtype)


def gated_merge(o_a, o_b, w_a, w_b, gates, layer, *, nrows, ga_col0, gb_col0, tm=512, tn=512):
    ka, kb = o_a.shape[1], o_b.shape[1]
    d = w_a.shape[2]
    tm, tn = min(tm, nrows), min(tn, d)
    ca, cb = ga_col0 // tn, gb_col0 // tn
    return pl.pallas_call(
        _merge_kernel,
        grid=(d // tn, nrows // tm),
        in_specs=[
            pl.BlockSpec((tm, ka), lambda j, i: (i, 0)),
            pl.BlockSpec((tm, kb), lambda j, i: (i, 0)),
            pl.BlockSpec((None, ka, tn), lambda j, i: (layer, 0, j)),
            pl.BlockSpec((None, kb, tn), lambda j, i: (layer, 0, j)),
            pl.BlockSpec((tm, tn), lambda j, i: (i, j + ca)),
            pl.BlockSpec((tm, tn), lambda j, i: (i, j + cb)),
        ],
        out_specs=pl.BlockSpec((tm, tn), lambda j, i: (i, j)),
        out_shape=jax.ShapeDtypeStruct((nrows, d), BF16),
        scratch_shapes=[pltpu.VMEM((ka, tn), BF16), pltpu.VMEM((kb, tn), BF16)],
        compiler_params=_params("arbitrary", "arbitrary"),
        name="gated_merge",
    )(o_a, o_b, w_a, w_b, gates, gates)


def _outproj_kernel(y_ref, w_ref, x_ref, gt_ref, o_ref, wb_ref):
    @pl.when(pl.program_id(1) == 0)
    def _():
        _cast_weight(w_ref, wb_ref)

    o_ref[...] = x_ref[...] + gt_ref[0] * jnp.dot(y_ref[...], wb_ref[...], preferred_element_type=F32)


def outproj_residual(y, w_out, x, mod3, layer, rows, k_gate, *, nrows, tm=512, tn=512):
    k = y.shape[1]
    d = w_out.shape[2]
    tm, tn = min(tm, nrows), min(tn, d)
    mod_row = rows.mod_row(tm)
    gate_col0 = k_gate * (d // tn)
    return pl.pallas_call(
        _outproj_kernel,
        grid=(d // tn, nrows // tm),
        in_specs=[
            pl.BlockSpec((tm, k), lambda j, i: (i, 0)),
            pl.BlockSpec((None, k, tn), lambda j, i: (layer, 0, j)),
            pl.BlockSpec((tm, tn), lambda j, i: (i, j)),
            pl.BlockSpec((1, 1, tn), lambda j, i: (mod_row(i), 0, j + gate_col0)),
        ],
        out_specs=pl.BlockSpec((tm, tn), lambda j, i: (i, j)),
        out_shape=jax.ShapeDtypeStruct((nrows, d), F32),
        scratch_shapes=[pltpu.VMEM((k, tn), BF16)],
        compiler_params=_params("arbitrary", "arbitrary"),
        name="outproj_residual",
    )(y, w_out, x, mod3)


Q_ROWS = 4
BAND_ROWS = WIN_H + Q_ROWS
N_PATTERNS = 3


def build_bias_table(rpb, grid_rows):
    half = WIN_H // 2
    groups = grid_rows // Q_ROWS
    assert grid_rows % Q_ROWS == 0 and grid_rows >= BAND_ROWS
    r0 = jnp.array([0, min(1, groups - 1), groups - 1]) * Q_ROWS
    band_start = jnp.clip(r0 - half, 0, grid_rows - BAND_ROWS)
    q_row = r0[:, None] + jnp.arange(Q_ROWS)[None, :]
    win_start = jnp.clip(q_row - half, 0, grid_rows - WIN_H)
    k_row = band_start[:, None] + jnp.arange(BAND_ROWS)[None, :]
    row_ok = (k_row[:, None, :] >= win_start[:, :, None]) & (k_row[:, None, :] < win_start[:, :, None] + WIN_H)
    row_off = jnp.clip(k_row[:, None, :] - q_row[:, :, None] + (WIN_H - 1), 0, 2 * WIN_H - 2)
    cols = jnp.arange(GRID_W)
    col_start = jnp.clip(cols - WIN_W // 2, 0, GRID_W - WIN_W)
    col_ok = (cols[None, :] >= col_start[:, None]) & (cols[None, :] < col_start[:, None] + WIN_W)
    col_off = jnp.clip(cols[None, :] - cols[:, None] + (WIN_W - 1), 0, 2 * WIN_W - 2)
    t = rpb[:, row_off]
    t = t[..., col_off]
    ok = row_ok[None, :, :, :, None, None] & col_ok[None, None, None, None]
    t = jnp.where(ok, t, MASK_BIAS)
    t = jnp.transpose(t, (0, 1, 2, 4, 3, 5))
    return t.reshape(rpb.shape[0], N_PATTERNS, Q_ROWS * GRID_W, BAND_ROWS * GRID_W).astype(F32)


def _softmax_pv(parts):
    m = functools.reduce(jnp.maximum, [jnp.max(s, axis=-1, keepdims=True) for s, _ in parts])
    es = [jnp.exp(s - m) for s, _ in parts]
    denom = functools.reduce(jnp.add, [jnp.sum(e, axis=-1, keepdims=True) for e in es])
    acc = functools.reduce(jnp.add, [jnp.dot(e.astype(BF16), v, preferred_element_type=F32)
                                     for e, (_, v) in zip(es, parts)])
    return acc / denom


def _qk(q, k):
    return lax.dot_general(q, k, (((1,), (1,)), ((), ())), preferred_element_type=F32) * (NA_HEAD_DIM ** -0.5)


def _natten_kernel(q_ref, k_ref, v_ref, kc_ref, vc_ref, bt_ref, o_ref, *, grid_rows):
    groups = grid_rows // Q_ROWS
    nq, band = Q_ROWS * GRID_W, BAND_ROWS * GRID_W
    kc = kc_ref[...]
    vc = vc_ref[...]

    def body(g, carry):
        band_start = jnp.clip(g * Q_ROWS - WIN_H // 2, 0, grid_rows - BAND_ROWS)
        pattern = jnp.where(g == 0, 0, jnp.where(g == groups - 1, 2, 1))
        q0 = pl.multiple_of(g * nq, nq)
        k0 = pl.multiple_of(band_start * GRID_W, GRID_W)
        q = q_ref[pl.ds(q0, nq), :]
        kb = k_ref[pl.ds(k0, band), :]
        vb = v_ref[pl.ds(k0, band), :]
        s_loc = _qk(q, kb) + bt_ref[pattern]
        s_ctx = _qk(q, kc)
        o_ref[pl.ds(q0, nq), :] = _softmax_pv([(s_loc, vb), (s_ctx, vc)]).astype(o_ref.dtype)
        return carry

    lax.fori_loop(0, groups, body, 0)


def neighbourhood_attention(qkv, kvc, bias_table, *, batch, n_lat, n_ctx, kc_row0, kc_col0, vc_col0):
    dh, h = NA_HEAD_DIM, NA_HEADS
    grid_rows = n_lat // GRID_W
    rb0, ck, cv = kc_row0 // n_ctx, kc_col0 // dh, vc_col0 // dh
    return pl.pallas_call(
        functools.partial(_natten_kernel, grid_rows=grid_rows),
        grid=(batch, h),
        in_specs=[
            pl.BlockSpec((n_lat, dh), lambda b, hh: (b, hh)),
            pl.BlockSpec((n_lat, dh), lambda b, hh: (b, h + hh)),
            pl.BlockSpec((n_lat, dh), lambda b, hh: (b, 2 * h + hh)),
            pl.BlockSpec((n_ctx, dh), lambda b, hh: (rb0 + b, ck + hh)),
            pl.BlockSpec((n_ctx, dh), lambda b, hh: (rb0 + b, cv + hh)),
            pl.BlockSpec((None, N_PATTERNS, Q_ROWS * GRID_W, BAND_ROWS * GRID_W), lambda b, hh: (hh, 0, 0, 0)),
        ],
        out_specs=pl.BlockSpec((n_lat, dh), lambda b, hh: (b, hh)),
        out_shape=jax.ShapeDtypeStruct((batch * n_lat, NA_WIDTH), BF16),
        compiler_params=_params("arbitrary", "arbitrary"),
        name="neighbourhood_attention",
    )(qkv, qkv, qkv, kvc, kvc, bias_table)


def _ctx_attn_kernel(q_ref, k_ref, v_ref, o_ref):
    o_ref[...] = _softmax_pv([(_qk(q_ref[...], k_ref[...]), v_ref[...])]).astype(o_ref.dtype)


def context_attention(qkv, *, batch, n_ctx, row0):
    dh, h = NA_HEAD_DIM, NA_HEADS
    rb0 = row0 // n_ctx
    return pl.pallas_call(
        _ctx_attn_kernel,
        grid=(batch, h),
        in_specs=[
            pl.BlockSpec((n_ctx, dh), lambda b, hh: (rb0 + b, hh)),
            pl.BlockSpec((n_ctx, dh), lambda b, hh: (rb0 + b, h + hh)),
            pl.BlockSpec((n_ctx, dh), lambda b, hh: (rb0 + b, 2 * h + hh)),
        ],
        out_specs=pl.BlockSpec((n_ctx, dh), lambda b, hh: (b, hh)),
        out_shape=jax.ShapeDtypeStruct((batch * n_ctx, NA_WIDTH), BF16),
        compiler_params=_params("arbitrary", "arbitrary"),
        name="context_attention",
    )(qkv, qkv, qkv)


def _gmlp_kernel(u_ref, gv_ref, ng_ref, ws_ref, bst_ref, o_ref):
    v = jax.nn.gelu(gv_ref[...])
    vc = v - jnp.mean(v, axis=-1, keepdims=True)
    y = vc * lax.rsqrt(jnp.mean(vc * vc, axis=-1, keepdims=True) + EPS) * ng_ref[...]
    yb = y.astype(BF16)
    gd = GMLP_WIDTH // GMLP_GROUPS
    for g in range(GMLP_GROUPS):
        sl = slice(g * gd, (g + 1) * gd)
        mixed = jnp.dot(ws_ref[g].astype(BF16), yb[:, sl], preferred_element_type=F32) + bst_ref[:, g:g + 1]
        o_ref[:, sl] = (jax.nn.gelu(u_ref[:, sl]) * mixed).astype(o_ref.dtype)


def spatial_gating(p, norm_g, ws, bs, layer, *, nrows, u_col0, gv_col0):
    w = GMLP_WIDTH
    bst = jnp.transpose(bs[layer])
    return pl.pallas_call(
        _gmlp_kernel,
        grid=(nrows // CHUNK,),
        in_specs=[
            pl.BlockSpec((CHUNK, w), lambda i: (i, u_col0 // w)),
            pl.BlockSpec((CHUNK, w), lambda i: (i, gv_col0 // w)),
            pl.BlockSpec((None, 1, w), lambda i: (layer, 0, 0)),
            pl.BlockSpec((None, GMLP_GROUPS, CHUNK, CHUNK), lambda i: (layer, 0, 0, 0)),
            pl.BlockSpec((CHUNK, GMLP_GROUPS), lambda i: (0, 0)),
        ],
        out_specs=pl.BlockSpec((CHUNK, w), lambda i: (i, 0)),
        out_shape=jax.ShapeDtypeStruct((nrows, w), BF16),
        compiler_params=_params("arbitrary"),
        name="spatial_gating",
    )(p, p, norm_g.reshape(norm_g.shape[0], 1, w), ws, bst)


def _split_bf16(x):
    hi = x.astype(BF16)
    return hi, (x - hi.astype(F32)).astype(BF16)


def _store_row_major(ref3, val):
    for j in range(ref3.shape[1]):
        ref3[:, j, :] = val[:, j * LANES:(j + 1) * LANES].astype(ref3.dtype)


def _norm_router_kernel(x_ref, g_ref, sh_ref, sc_ref, rw_ref, h_ref, lg_ref):
    h = _rms(x_ref[...], g_ref[...]) * (1.0 + sc_ref[0]) + sh_ref[0]
    _store_row_major(h_ref, h)
    hh, hl = _split_bf16(h)
    wh, wl = _split_bf16(rw_ref[...])
    lg = (jnp.dot(hh, wh, preferred_element_type=F32) + jnp.dot(hl, wh, preferred_element_type=F32)
          + jnp.dot(hh, wl, preferred_element_type=F32))
    lg_ref[...] = jnp.transpose(lg)[:N_EXPERTS, :]


def norm_mod_router(x, g, layer, mod3, rows, k_shift, k_scale, router_w, nrows):
    d = x.shape[1]
    tr = 256
    mod_row = rows.mod_row(tr)
    rw = jnp.pad(router_w[layer], ((0, 0), (0, LANES - N_EXPERTS)))
    return pl.pallas_call(
        _norm_router_kernel,
        grid=(nrows // tr,),
        in_specs=[
            pl.BlockSpec((tr, d), lambda i: (i, 0)),
            pl.BlockSpec((None, 1, d), lambda i: (layer, 0, 0)),
            pl.BlockSpec((1, 1, d), lambda i: (mod_row(i), 0, k_shift)),
            pl.BlockSpec((1, 1, d), lambda i: (mod_row(i), 0, k_scale)),
            pl.BlockSpec((d, LANES), lambda i: (0, 0)),
        ],
        out_specs=[pl.BlockSpec((tr, d // LANES, LANES), lambda i: (i, 0, 0)),
                   pl.BlockSpec((N_EXPERTS, tr), lambda i: (0, i))],
        out_shape=[jax.ShapeDtypeStruct((nrows, d // LANES, LANES), F32),
                   jax.ShapeDtypeStruct((N_EXPERTS, nrows), F32)],
        compiler_params=_params("arbitrary"),
        name="norm_mod_router",
    )(x, g.reshape(g.shape[0], 1, d), mod3, mod3, rw)


def _cumsum_lanes(x):
    r, n = x.shape
    blk = min(n, LANES)
    tri = (lax.broadcasted_iota(I32, (blk, blk), 0) <= lax.broadcasted_iota(I32, (blk, blk), 1)).astype(BF16)
    off = jnp.zeros((r, 1), F32)
    out = []
    for j in range(n // blk):
        cs = jnp.dot(x[:, j * blk:(j + 1) * blk].astype(BF16), tri, preferred_element_type=F32) + off
        out.append(cs)
        off = cs[:, blk - 1:blk]
    return jnp.concatenate(out, axis=1) if len(out) > 1 else out[0]


def _route_kernel(lg_ref, idx_ref, dest_ref, gate_ref, start_ref, cnt_ref, slot_s, aff_s, *, cap):
    e_n, n = lg_ref.shape
    lg = lg_ref[...]
    ex = jnp.exp(lg - jnp.max(lg, axis=0, keepdims=True))
    aff = ex / jnp.sum(ex, axis=0, keepdims=True)
    bits = lax.bitcast_convert_type(aff, I32)

    thr = jnp.zeros((e_n, 1), I32)
    for bit in range(30, -1, -1):
        cand = thr | (1 << bit)
        n_ge = jnp.sum((bits >= cand).astype(F32), axis=1, keepdims=True)
        thr = jnp.where(n_ge >= cap, cand, thr)
    above = bits > thr
    tied = (bits == thr).astype(F32)
    need = cap - jnp.sum(above.astype(F32), axis=1, keepdims=True)
    tied_before = _cumsum_lanes(tied) - tied
    sel = jnp.where(above, 1.0, jnp.where(tied_before < need, tied, 0.0))

    slot = _cumsum_lanes(sel) - sel
    per_tok = jnp.sum(sel, axis=0, keepdims=True)
    per_tok8 = jnp.broadcast_to(per_tok, (8, n))
    start = (_cumsum_lanes(per_tok8) - per_tok8)[0:1]
    start_ref[...] = start.astype(I32)
    cnt_ref[...] = per_tok.astype(I32)
    slot_s[...] = jnp.where(sel > 0.0, slot, -1.0)
    aff_s[...] = aff

    tok = lax.broadcasted_iota(I32, (1, n), 1).astype(F32)
    cc = min(cap, LANES)

    def body(e, chosen_by_earlier):
        srow = slot_s[pl.ds(e, 1), :]
        arow = aff_s[pl.ds(e, 1), :]
        drow = start + chosen_by_earlier
        for c0 in range(0, cap, cc):
            want = (lax.broadcasted_iota(I32, (cc, 1), 0) + c0).astype(F32)
            hit = srow == want
            pick = lambda row: jnp.sum(jnp.where(hit, row, 0.0), axis=1, keepdims=True)
            idx_ref[e, pl.ds(c0, cc), :] = pick(tok).astype(I32)
            dest_ref[e, pl.ds(c0, cc), :] = pick(drow).astype(I32)
            gate_ref[e, pl.ds(c0, cc), :] = pick(arow)
        return chosen_by_earlier + (srow >= 0.0).astype(F32)

    lax.fori_loop(0, e_n, body, jnp.zeros((1, n), F32))


def expert_choice_route(logits_t, *, batch, n, col0):
    e_n = logits_t.shape[0]
    cap = CAPACITY_FACTOR * n // e_n
    cb0 = col0 // n
    per_slot = lambda dt: jax.ShapeDtypeStruct((batch, e_n, cap, 1), dt)
    per_tok = jax.ShapeDtypeStruct((batch, 1, n), I32)
    slot_spec = pl.BlockSpec((None, e_n, cap, 1), lambda b: (b, 0, 0, 0))
    tok_spec = pl.BlockSpec((None, 1, n), lambda b: (b, 0, 0))
    return pl.pallas_call(
        functools.partial(_route_kernel, cap=cap),
        grid=(batch,),
        in_specs=[pl.BlockSpec((e_n, n), lambda b: (0, cb0 + b))],
        out_specs=[slot_spec, slot_spec, slot_spec, tok_spec, tok_spec],
        out_shape=[per_slot(I32), per_slot(I32), per_slot(F32), per_tok, per_tok],
        scratch_shapes=[pltpu.VMEM((e_n, n), F32), pltpu.VMEM((e_n, n), F32)],
        compiler_params=_params("arbitrary"),
        name="expert_choice_route",
    )(logits_t)


DMA_WINDOW = 16


def _permute_kernel(sidx_ref, didx_ref, src_ref, dst_ref, sem, *, n):
    def row_copy(i):
        return pltpu.make_async_copy(src_ref.at[sidx_ref[i]], dst_ref.at[didx_ref[i]], sem)

    def issue(i, carry):
        row_copy(i).start()

        @pl.when(i >= DMA_WINDOW)
        def _():
            row_copy(i - DMA_WINDOW).wait()

        return carry

    lax.fori_loop(0, n, issue, 0)

    def drain(i, carry):
        row_copy(i).wait()
        return carry

    lax.fori_loop(n - DMA_WINDOW, n, drain, 0)


def permute_rows(src, src_idx, dst_idx, n_dst):
    n = src_idx.shape[0]
    assert n == n_dst and n >= DMA_WINDOW
    return pl.pallas_call(
        functools.partial(_permute_kernel, n=n),
        grid_spec=pltpu.PrefetchScalarGridSpec(
            num_scalar_prefetch=2,
            grid=(1,),
            in_specs=[pl.BlockSpec(memory_space=pl.ANY)],
            out_specs=pl.BlockSpec(memory_space=pl.ANY),
            scratch_shapes=[pltpu.SemaphoreType.DMA(())],
        ),
        out_shape=jax.ShapeDtypeStruct((n_dst,) + src.shape[1:], src.dtype),
        compiler_params=_params("arbitrary"),
        name="permute_rows",
    )(src_idx, dst_idx, src)


def _ffn_kernel(xg_ref, w1_ref, w3_ref, w2_ref, g_ref, y_ref, xs_s, hid_s, *, f0, f1, mchunk, tf, m_split):
    s = pl.program_id(1)
    m_part = hid_s.shape[1] // m_split

    @pl.when(s < f0)
    def _():
        r = pl.multiple_of(s * mchunk, mchunk)
        for j in range(xg_ref.shape[1]):
            xs_s[pl.ds(r, mchunk), j * LANES:(j + 1) * LANES] = xg_ref[:, j, :].astype(BF16)

    @pl.when(jnp.logical_and(s >= f0, s < f0 + f1))
    def _():
        xs = xs_s[...]
        h1 = jnp.dot(xs, w1_ref[...].astype(BF16), preferred_element_type=F32)
        h3 = jnp.dot(xs, w3_ref[...].astype(BF16), preferred_element_type=F32)
        hid_s[s - f0] = (jax.nn.silu(h1) * h3).astype(BF16)

    @pl.when(s >= f0 + f1)
    def _():
        r = pl.multiple_of(((s - f0 - f1) % m_split) * m_part, m_part)
        acc = jnp.dot(hid_s[0, pl.ds(r, m_part), :], w2_ref[0:tf, :].astype(BF16), preferred_element_type=F32)
        for f in range(1, f1):
            acc += jnp.dot(hid_s[f, pl.ds(r, m_part), :], w2_ref[f * tf:(f + 1) * tf, :].astype(BF16),
                           preferred_element_type=F32)
        _store_row_major(y_ref, acc * g_ref[pl.ds(r, m_part), :])


def expert_ffn(xg, w1, w3, w2, gate, layer):
    e_n, m, sub, _ = xg.shape
    d = sub * LANES
    ff = w1.shape[3]
    tf, tn = min(ff, 256), min(d, 1024)
    mchunk = min(m, 128)
    m_split = 2 if m >= 512 else 1
    f0, f1, f2 = m // mchunk, ff // tf, (d // tn) * m_split
    out_step = lambda s: jnp.maximum(s - f0 - f1, 0)
    return pl.pallas_call(
        functools.partial(_ffn_kernel, f0=f0, f1=f1, mchunk=mchunk, tf=tf, m_split=m_split),
        grid=(e_n, f0 + f1 + f2),
        in_specs=[
            pl.BlockSpec((None, mchunk, sub, LANES), lambda e, s: (e, jnp.minimum(s, f0 - 1), 0, 0)),
            pl.BlockSpec((None, None, d, tf), lambda e, s: (layer, e, 0, jnp.clip(s - f0, 0, f1 - 1))),
            pl.BlockSpec((None, None, d, tf), lambda e, s: (layer, e, 0, jnp.clip(s - f0, 0, f1 - 1))),
            pl.BlockSpec((None, None, ff, tn), lambda e, s: (layer, e, 0, out_step(s) // m_split)),
            pl.BlockSpec((None, m, 1), lambda e, s: (e, 0, 0)),
        ],
        out_specs=pl.BlockSpec((None, m // m_split, tn // LANES, LANES),
                               lambda e, s: (e, out_step(s) % m_split, out_step(s) // m_split, 0)),
        out_shape=jax.ShapeDtypeStruct((e_n, m, sub, LANES), F32),
        scratch_shapes=[pltpu.VMEM((m, d), BF16), pltpu.VMEM((f1, m, tf), BF16)],
        compiler_params=_params("arbitrary", "arbitrary"),
        name="expert_ffn",
    )(xg, w1, w3, w2, gate)


def _combine_kernel(cf_ref, cn_ref, yg_ref, st_ref, ct_ref, x_ref, gt_ref, o_ref, acc_s, *, tiles, ch, kmax):
    b, t, k = pl.program_id(0), pl.program_id(1), pl.program_id(2)
    tile = b * tiles + t

    @pl.when(k == 0)
    def _():
        acc_s[...] = jnp.zeros_like(acc_s)

    @pl.when(k < cn_ref[tile])
    def _():
        pair = (cf_ref[tile] + k) * ch + lax.broadcasted_iota(I32, (1, ch), 1)
        rel = pair - st_ref[...]
        own = jnp.where(jnp.logical_and(rel >= 0, rel < ct_ref[...]), 1.0, 0.0).astype(BF16)
        for j in range(yg_ref.shape[1]):
            yh, yl = _split_bf16(yg_ref[:, j, :])
            acc_s[:, j * LANES:(j + 1) * LANES] += (jnp.dot(own, yh, preferred_element_type=F32)
                                                    + jnp.dot(own, yl, preferred_element_type=F32))

    @pl.when(k == kmax - 1)
    def _():
        o_ref[...] = x_ref[...] + gt_ref[0] * acc_s[...]


def combine_residual(yg, start, cnt, x, mod3, rows, k_gate, *, batch, n, row0):
    d = x.shape[1]
    pairs = yg.shape[0] // batch
    sub = yg.shape[1]
    tt = min(n, 256)
    ch = min(pairs, 512)
    tiles = n // tt
    kmax = tt * N_EXPERTS // ch + 1
    n_chunks = pairs // ch
    st = start.reshape(batch, n)
    ct = cnt.reshape(batch, n)
    first = st[:, ::tt]
    last = jnp.concatenate([first[:, 1:], jnp.full((batch, 1), pairs, I32)], axis=1)
    cfirst = jnp.minimum(first // ch, n_chunks - 1)
    cnum = jnp.where(last > first, (last - 1) // ch - first // ch + 1, 0)
    cfirst, cnum = cfirst.reshape(-1).astype(I32), cnum.reshape(-1).astype(I32)
    mod_row = rows.mod_row(tt)
    rt0 = row0 // tt
    gate_col = k_gate

    def yg_map(b, t, k, cf, cn):
        tile = b * tiles + t
        return (b * n_chunks + jnp.minimum(cf[tile] + k, cf[tile] + jnp.maximum(cn[tile] - 1, 0)), 0, 0)

    row_map = lambda b, t, k, cf, cn: (rt0 + b * tiles + t, 0)
    tok_map = lambda b, t, k, cf, cn: (b * tiles + t, 0)
    return pl.pallas_call(
        functools.partial(_combine_kernel, tiles=tiles, ch=ch, kmax=kmax),
        grid_spec=pltpu.PrefetchScalarGridSpec(
            num_scalar_prefetch=2,
            grid=(batch, tiles, kmax),
            in_specs=[
                pl.BlockSpec((ch, sub, LANES), yg_map),
                pl.BlockSpec((tt, 1), tok_map),
                pl.BlockSpec((tt, 1), tok_map),
                pl.BlockSpec((tt, d), row_map),
                pl.BlockSpec((1, 1, d), lambda b, t, k, cf, cn: (mod_row(rt0 + b * tiles + t), 0, gate_col)),
            ],
            out_specs=pl.BlockSpec((tt, d), row_map),
            scratch_shapes=[pltpu.VMEM((tt, d), F32)],
        ),
        out_shape=jax.ShapeDtypeStruct(x.shape, F32),
        input_output_aliases={5: 0},
        compiler_params=_params("arbitrary", "arbitrary", "arbitrary"),
        name="combine_residual",
    )(cfirst, cnum, yg, st.reshape(batch * n, 1), ct.reshape(batch * n, 1), x, mod3)


def expert_choice_ffn_residual(x, h, logits_t, mod3, rows, k_gate, w1, w3, w2, layer, *, batch, n, row0):
    e_n = logits_t.shape[0]
    d = x.shape[1]
    idx, dest, gate, start, cnt = expert_choice_route(logits_t, batch=batch, n=n, col0=row0)
    cap = idx.shape[2]
    pairs = e_n * cap
    boff = jnp.arange(batch, dtype=I32)[:, None, None]
    src_rows = jnp.transpose(idx[..., 0] + row0 + boff * n, (1, 0, 2)).reshape(-1)
    pair_rows = jnp.transpose(dest[..., 0] + boff * pairs, (1, 0, 2)).reshape(-1)
    gate_col = jnp.transpose(gate, (1, 0, 2, 3)).reshape(e_n, batch * cap, 1)
    slots = jnp.arange(batch * pairs, dtype=I32)
    sub = d // LANES
    xg = permute_rows(h, src_rows, slots, batch * pairs).reshape(e_n, batch * cap, sub, LANES)
    y = expert_ffn(xg, w1, w3, w2, gate_col, layer).reshape(batch * pairs, sub, LANES)
    yg = permute_rows(y, slots, pair_rows, batch * pairs)
    return combine_residual(yg, start, cnt, x, mod3, rows, k_gate, batch=batch, n=n, row0=row0)


def kernel(x, c, ctx, c_ctx, ada_w, ada_b, norm1_g, norm2_g, w_in, na_rpb, gmlp_norm_g, gmlp_ws, gmlp_bs,
           w_branch_a, w_branch_b, w_out, router_w, exp_w1, exp_w3, exp_w2, final_norm_g):
    batch, n_lat, d = x.shape
    n_ctx = ctx.shape[1]
    depth = ada_w.shape[0]
    rows = Rows(batch, n_lat, n_ctx)
    lat_rows, all_rows = rows.lat_rows, rows.all_rows
    off_k, off_v, off_u = NA_WIDTH, 2 * NA_WIDTH, 3 * NA_WIDTH
    off_gv = off_u + GMLP_WIDTH
    off_ga = off_gv + GMLP_WIDTH
    off_gb = off_ga + d
    in_cols = off_gb + d

    stream = jnp.concatenate([x.reshape(lat_rows, d), ctx.reshape(batch * n_ctx, d)], axis=0)
    cv = jnp.zeros((MOD_ROWS, d), F32).at[:batch].set(c).at[batch].set(c_ctx)

    for layer in range(depth):
        last = layer == depth - 1
        nrows = lat_rows if last else all_rows
        mod3 = ada_mod(cv, ada_w, ada_b, layer).reshape(MOD_ROWS, 1, N_MOD * d)

        h = norm_mod(stream, norm1_g, layer, mod3, rows, 0, 1, 0, nrows, BF16)
        qkv = matmul(h, w_in, layer, row0=0, nrows=nrows, col0=0, ncols=off_u, out_dtype=BF16)
        rest = matmul(h, w_in, layer, row0=0, nrows=nrows, col0=off_u, ncols=in_cols - off_u, out_dtype=F32)
        bias_table = build_bias_table(na_rpb[layer], n_lat // GRID_W)
        if last:
            h_ctx = norm_mod(stream, norm1_g, layer, mod3, rows, 0, 1, lat_rows, batch * n_ctx, BF16)
            kvc = matmul(h_ctx, w_in, layer, row0=0, nrows=batch * n_ctx, col0=off_k, ncols=2 * NA_WIDTH,
                         out_dtype=BF16)
            o_a = neighbourhood_attention(qkv, kvc, bias_table, batch=batch, n_lat=n_lat, n_ctx=n_ctx,
                                          kc_row0=0, kc_col0=0, vc_col0=NA_WIDTH)
        else:
            o_lat = neighbourhood_attention(qkv, qkv, bias_table, batch=batch, n_lat=n_lat, n_ctx=n_ctx,
                                            kc_row0=lat_rows, kc_col0=off_k, vc_col0=off_v)
            o_ctx = context_attention(qkv, batch=batch, n_ctx=n_ctx, row0=lat_rows)
            o_a = jnp.concatenate([o_lat, o_ctx], axis=0)
        o_b = spatial_gating(rest, gmlp_norm_g, gmlp_ws, gmlp_bs, layer, nrows=nrows, u_col0=0,
                             gv_col0=off_gv - off_u)
        y = gated_merge(o_a, o_b, w_branch_a, w_branch_b, rest, layer, nrows=nrows,
                        ga_col0=off_ga - off_u, gb_col0=off_gb - off_u)
        stream = outproj_residual(y, w_out, stream, mod3, layer, rows, 2, nrows=nrows)

        h2, logits_t = norm_mod_router(stream, norm2_g, layer, mod3, rows, 3, 4, router_w, nrows)
        stream = expert_choice_ffn_residual(stream, h2, logits_t, mod3, rows, 5, exp_w1, exp_w3, exp_w2, layer,
                                            batch=batch, n=n_lat, row0=0)
        if not last:
            stream = expert_choice_ffn_residual(stream, h2, logits_t, mod3, rows, 5, exp_w1, exp_w3, exp_w2, layer,
                                                batch=batch, n=n_ctx, row0=lat_rows)

    return final_norm(stream, final_norm_g, lat_rows).reshape(batch, n_lat, d)
```

```python
import functools

import jax
import jax.numpy as jnp
from jax import lax
from jax.experimental import pallas as pl
from jax.experimental.pallas import tpu as pltpu

F32, BF16, I32 = jnp.float32, jnp.bfloat16, jnp.int32

GRID_W = 64
NA_HEADS = 16
NA_HEAD_DIM = 128
NA_WIDTH = NA_HEADS * NA_HEAD_DIM
WIN_H = 8
WIN_W = 16
GMLP_GROUPS = 16
GMLP_WIDTH = 2048
CHUNK = 128
N_EXPERTS = 16
CAPACITY_FACTOR = 2
N_MOD = 6
EPS = 1e-6
MASK_BIAS = -1e30

V7X_VMEM_LIMIT_BYTES = 56 * 1024 * 1024
LANES = 128
MOD_ROWS = 8


def _params(*sem):
    return pltpu.CompilerParams(dimension_semantics=sem, vmem_limit_bytes=V7X_VMEM_LIMIT_BYTES)


def _cast_weight(w_ref, wb_ref):
    rows = w_ref.shape[0]
    step = min(rows, 512)

    def body(i, carry):
        r = pl.multiple_of(i * step, step)
        wb_ref[pl.ds(r, step), :] = w_ref[pl.ds(r, step), :].astype(BF16)
        return carry

    lax.fori_loop(0, rows // step, body, 0)


def _ada_kernel(c_ref, w_ref, b_ref, o_ref):
    c = c_ref[...]
    s = c * jax.nn.sigmoid(c)
    o_ref[...] = jnp.dot(s.astype(BF16), w_ref[...].astype(BF16), preferred_element_type=F32) + b_ref[...]


def ada_mod(cv, ada_w, ada_b, layer):
    _, d, m = ada_w.shape
    tn = min(m, 512)
    return pl.pallas_call(
        _ada_kernel,
        grid=(m // tn,),
        in_specs=[
            pl.BlockSpec((MOD_ROWS, d), lambda j: (0, 0)),
            pl.BlockSpec((None, d, tn), lambda j: (layer, 0, j)),
            pl.BlockSpec((None, 1, tn), lambda j: (layer, 0, j)),
        ],
        out_specs=pl.BlockSpec((MOD_ROWS, tn), lambda j: (0, j)),
        out_shape=jax.ShapeDtypeStruct((MOD_ROWS, m), F32),
        compiler_params=_params("arbitrary"),
        name="ada_mod",
    )(cv, ada_w, ada_b.reshape(ada_b.shape[0], 1, m))


class Rows:
    def __init__(self, batch, n_lat, n_ctx):
        self.batch, self.n_lat, self.n_ctx = batch, n_lat, n_ctx
        self.lat_rows = batch * n_lat
        self.all_rows = self.lat_rows + batch * n_ctx

    def mod_row(self, tile_rows):
        lat_tiles = self.lat_rows // tile_rows
        per_batch = self.n_lat // tile_rows
        batch = self.batch

        def f(i):
            return jnp.where(i < lat_tiles, i // per_batch, batch)

        return f


def _mod_spec(mod_row, k, d):
    return pl.BlockSpec((1, 1, d), lambda i: (mod_row(i), 0, k))


def _rms(x, g):
    return x * lax.rsqrt(jnp.mean(x * x, axis=-1, keepdims=True) + EPS) * g


def _norm_mod_kernel(x_ref, g_ref, sh_ref, sc_ref, o_ref):
    y = _rms(x_ref[...], g_ref[...])
    o_ref[...] = (y * (1.0 + sc_ref[0]) + sh_ref[0]).astype(o_ref.dtype)


def norm_mod(x, g, layer, mod3, rows, k_shift, k_scale, row0, nrows, out_dtype):
    d = x.shape[1]
    tr = 256
    mod_row = rows.mod_row(tr)
    t0 = row0 // tr
    return pl.pallas_call(
        _norm_mod_kernel,
        grid=(nrows // tr,),
        in_specs=[
            pl.BlockSpec((tr, d), lambda i: (i + t0, 0)),
            pl.BlockSpec((None, 1, d), lambda i: (layer, 0, 0)),
            pl.BlockSpec((1, 1, d), lambda i: (mod_row(i + t0), 0, k_shift)),
            pl.BlockSpec((1, 1, d), lambda i: (mod_row(i + t0), 0, k_scale)),
        ],
        out_specs=pl.BlockSpec((tr, d), lambda i: (i, 0)),
        out_shape=jax.ShapeDtypeStruct((nrows, d), out_dtype),
        compiler_params=_params("arbitrary"),
        name="norm_mod",
    )(x, g.reshape(g.shape[0], 1, d), mod3, mod3)


def _final_norm_kernel(x_ref, g_ref, o_ref):
    o_ref[...] = _rms(x_ref[...], g_ref[...])


def final_norm(x, g, nrows):
    d = x.shape[1]
    tr = 256
    return pl.pallas_call(
        _final_norm_kernel,
        grid=(nrows // tr,),
        in_specs=[pl.BlockSpec((tr, d), lambda i: (i, 0)), pl.BlockSpec((1, d), lambda i: (0, 0))],
        out_specs=pl.BlockSpec((tr, d), lambda i: (i, 0)),
        out_shape=jax.ShapeDtypeStruct((nrows, d), F32),
        compiler_params=_params("arbitrary"),
        name="final_norm",
    )(x, g.reshape(1, d))


def _mm_kernel(a_ref, w_ref, o_ref, wb_ref):
    @pl.when(pl.program_id(1) == 0)
    def _():
        _cast_weight(w_ref, wb_ref)

    o_ref[...] = jnp.dot(a_ref[...], wb_ref[...], preferred_element_type=F32).astype(o_ref.dtype)


def matmul(a, w, layer, *, row0, nrows, col0, ncols, out_dtype, tm=512, tn=512):
    k = a.shape[1]
    tm, tn = min(tm, nrows), min(tn, ncols)
    r0, c0 = row0 // tm, col0 // tn
    return pl.pallas_call(
        _mm_kernel,
        grid=(ncols // tn, nrows // tm),
        in_specs=[
            pl.BlockSpec((tm, k), lambda j, i: (i + r0, 0)),
            pl.BlockSpec((None, k, tn), lambda j, i: (layer, 0, j + c0)),
        ],
        out_specs=pl.BlockSpec((tm, tn), lambda j, i: (i, j)),
        out_shape=jax.ShapeDtypeStruct((nrows, ncols), out_dtype),
        scratch_shapes=[pltpu.VMEM((k, tn), BF16)],
        compiler_params=_params("arbitrary", "arbitrary"),
        name="matmul",
    )(a, w)


def _merge_kernel(oa_ref, ob_ref, wa_ref, wb_ref, ga_ref, gb_ref, y_ref, wab_ref, wbb_ref):
    @pl.when(pl.program_id(1) == 0)
    def _():
        _cast_weight(wa_ref, wab_ref)
        _cast_weight(wb_ref, wbb_ref)

    ya = jnp.dot(oa_ref[...], wab_ref[...], preferred_element_type=F32)
    yb = jnp.dot(ob_ref[...], wbb_ref[...], preferred_element_type=F32)
    y = jax.nn.sigmoid(ga_ref[...]) * ya + jax.nn.sigmoid(gb_ref[...]) * yb
    y_ref[...] = y.astype(y_ref.dtype)


def gated_merge(o_a, o_b, w_a, w_b, gates, layer, *, nrows, ga_col0, gb_col0, tm=512, tn=512):
    ka, kb = o_a.shape[1], o_b.shape[1]
    d = w_a.shape[2]
    tm, tn = min(tm, nrows), min(tn, d)
    ca, cb = ga_col0 // tn, gb_col0 // tn
    return pl.pallas_call(
        _merge_kernel,
        grid=(d // tn, nrows // tm),
        in_specs=[
            pl.BlockSpec((tm, ka), lambda j, i: (i, 0)),
            pl.BlockSpec((tm, kb), lambda j, i: (i, 0)),
            pl.BlockSpec((None, ka, tn), lambda j, i: (layer, 0, j)),
            pl.BlockSpec((None, kb, tn), lambda j, i: (layer, 0, j)),
            pl.BlockSpec((tm, tn), lambda j, i: (i, j + ca)),
            pl.BlockSpec((tm, tn), lambda j, i: (i, j + cb)),
        ],
        out_specs=pl.BlockSpec((tm, tn), lambda j, i: (i, j)),
        out_shape=jax.ShapeDtypeStruct((nrows, d), BF16),
        scratch_shapes=[pltpu.VMEM((ka, tn), BF16), pltpu.VMEM((kb, tn), BF16)],
        compiler_params=_params("arbitrary", "arbitrary"),
        name="gated_merge",
    )(o_a, o_b, w_a, w_b, gates, gates)


def _outproj_kernel(y_ref, w_ref, x_ref, gt_ref, o_ref, wb_ref):
    @pl.when(pl.program_id(1) == 0)
    def _():
        _cast_weight(w_ref, wb_ref)

    o_ref[...] = x_ref[...] + gt_ref[0] * jnp.dot(y_ref[...], wb_ref[...], preferred_element_type=F32)


def outproj_residual(y, w_out, x, mod3, layer, rows, k_gate, *, nrows, tm=512, tn=512):
    k = y.shape[1]
    d = w_out.shape[2]
    tm, tn = min(tm, nrows), min(tn, d)
    mod_row = rows.mod_row(tm)
    gate_col0 = k_gate * (d // tn)
    return pl.pallas_call(
        _outproj_kernel,
        grid=(d // tn, nrows // tm),
        in_specs=[
            pl.BlockSpec((tm, k), lambda j, i: (i, 0)),
            pl.BlockSpec((None, k, tn), lambda j, i: (layer, 0, j)),
            pl.BlockSpec((tm, tn), lambda j, i: (i, j)),
            pl.BlockSpec((1, 1, tn), lambda j, i: (mod_row(i), 0, j + gate_col0)),
        ],
        out_specs=pl.BlockSpec((tm, tn), lambda j, i: (i, j)),
        out_shape=jax.ShapeDtypeStruct((nrows, d), F32),
        scratch_shapes=[pltpu.VMEM((k, tn), BF16)],
        compiler_params=_params("arbitrary", "arbitrary"),
        name="outproj_residual",
    )(y, w_out, x, mod3)


Q_ROWS = 4
BAND_ROWS = WIN_H + Q_ROWS
N_PATTERNS = 3


def build_bias_table(rpb, grid_rows):
    half = WIN_H // 2
    groups = grid_rows // Q_ROWS
    assert grid_rows % Q_ROWS == 0 and grid_rows >= BAND_ROWS
    r0 = jnp.array([0, min(1, groups - 1), groups - 1]) * Q_ROWS
    band_start = jnp.clip(r0 - half, 0, grid_rows - BAND_ROWS)
    q_row = r0[:, None] + jnp.arange(Q_ROWS)[None, :]
    win_start = jnp.clip(q_row - half, 0, grid_rows - WIN_H)
    k_row = band_start[:, None] + jnp.arange(BAND_ROWS)[None, :]
    row_ok = (k_row[:, None, :] >= win_start[:, :, None]) & (k_row[:, None, :] < win_start[:, :, None] + WIN_H)
    row_off = jnp.clip(k_row[:, None, :] - q_row[:, :, None] + (WIN_H - 1), 0, 2 * WIN_H - 2)
    cols = jnp.arange(GRID_W)
    col_start = jnp.clip(cols - WIN_W // 2, 0, GRID_W - WIN_W)
    col_ok = (cols[None, :] >= col_start[:, None]) & (cols[None, :] < col_start[:, None] + WIN_W)
    col_off = jnp.clip(cols[None, :] - cols[:, None] + (WIN_W - 1), 0, 2 * WIN_W - 2)
    t = rpb[:, row_off]
    t = t[..., col_off]
    ok = row_ok[None, :, :, :, None, None] & col_ok[None, None, None, None]
    t = jnp.where(ok, t, MASK_BIAS)
    t = jnp.transpose(t, (0, 1, 2, 4, 3, 5))
    return t.reshape(rpb.shape[0], N_PATTERNS, Q_ROWS * GRID_W, BAND_ROWS * GRID_W).astype(F32)


def _softmax_pv(parts):
    m = functools.reduce(jnp.maximum, [jnp.max(s, axis=-1, keepdims=True) for s, _ in parts])
    es = [jnp.exp(s - m) for s, _ in parts]
    denom = functools.reduce(jnp.add, [jnp.sum(e, axis=-1, keepdims=True) for e in es])
    acc = functools.reduce(jnp.add, [jnp.dot(e.astype(BF16), v, preferred_element_type=F32)
                                     for e, (_, v) in zip(es, parts)])
    return acc / denom


def _qk(q, k):
    return lax.dot_general(q, k, (((1,), (1,)), ((), ())), preferred_element_type=F32) * (NA_HEAD_DIM ** -0.5)


def _natten_kernel(q_ref, k_ref, v_ref, kc_ref, vc_ref, bt_ref, o_ref, *, grid_rows):
    groups = grid_rows // Q_ROWS
    nq, band = Q_ROWS * GRID_W, BAND_ROWS * GRID_W
    kc = kc_ref[...]
    vc = vc_ref[...]

    def body(g, carry):
        band_start = jnp.clip(g * Q_ROWS - WIN_H // 2, 0, grid_rows - BAND_ROWS)
        pattern = jnp.where(g == 0, 0, jnp.where(g == groups - 1, 2, 1))
        q0 = pl.multiple_of(g * nq, nq)
        k0 = pl.multiple_of(band_start * GRID_W, GRID_W)
        q = q_ref[pl.ds(q0, nq), :]
        kb = k_ref[pl.ds(k0, band), :]
        vb = v_ref[pl.ds(k0, band), :]
        s_loc = _qk(q, kb) + bt_ref[pattern]
        s_ctx = _qk(q, kc)
        o_ref[pl.ds(q0, nq), :] = _softmax_pv([(s_loc, vb), (s_ctx, vc)]).astype(o_ref.dtype)
        return carry

    lax.fori_loop(0, groups, body, 0)


def neighbourhood_attention(qkv, kvc, bias_table, *, batch, n_lat, n_ctx, kc_row0, kc_col0, vc_col0):
    dh, h = NA_HEAD_DIM, NA_HEADS
    grid_rows = n_lat // GRID_W
    rb0, ck, cv = kc_row0 // n_ctx, kc_col0 // dh, vc_col0 // dh
    return pl.pallas_call(
        functools.partial(_natten_kernel, grid_rows=grid_rows),
        grid=(batch, h),
        in_specs=[
            pl.BlockSpec((n_lat, dh), lambda b, hh: (b, hh)),
            pl.BlockSpec((n_lat, dh), lambda b, hh: (b, h + hh)),
            pl.BlockSpec((n_lat, dh), lambda b, hh: (b, 2 * h + hh)),
            pl.BlockSpec((n_ctx, dh), lambda b, hh: (rb0 + b, ck + hh)),
            pl.BlockSpec((n_ctx, dh), lambda b, hh: (rb0 + b, cv + hh)),
            pl.BlockSpec((None, N_PATTERNS, Q_ROWS * GRID_W, BAND_ROWS * GRID_W), lambda b, hh: (hh, 0, 0, 0)),
        ],
        out_specs=pl.BlockSpec((n_lat, dh), lambda b, hh: (b, hh)),
        out_shape=jax.ShapeDtypeStruct((batch * n_lat, NA_WIDTH), BF16),
        compiler_params=_params("arbitrary", "arbitrary"),
        name="neighbourhood_attention",
    )(qkv, qkv, qkv, kvc, kvc, bias_table)


def _ctx_attn_kernel(q_ref, k_ref, v_ref, o_ref):
    o_ref[...] = _softmax_pv([(_qk(q_ref[...], k_ref[...]), v_ref[...])]).astype(o_ref.dtype)


def context_attention(qkv, *, batch, n_ctx, row0):
    dh, h = NA_HEAD_DIM, NA_HEADS
    rb0 = row0 // n_ctx
    return pl.pallas_call(
        _ctx_attn_kernel,
        grid=(batch, h),
        in_specs=[
            pl.BlockSpec((n_ctx, dh), lambda b, hh: (rb0 + b, hh)),
            pl.BlockSpec((n_ctx, dh), lambda b, hh: (rb0 + b, h + hh)),
            pl.BlockSpec((n_ctx, dh), lambda b, hh: (rb0 + b, 2 * h + hh)),
        ],
        out_specs=pl.BlockSpec((n_ctx, dh), lambda b, hh: (b, hh)),
        out_shape=jax.ShapeDtypeStruct((batch * n_ctx, NA_WIDTH), BF16),
        compiler_params=_params("arbitrary", "arbitrary"),
        name="context_attention",
    )(qkv, qkv, qkv)


def _gmlp_kernel(u_ref, gv_ref, ng_ref, ws_ref, bst_ref, o_ref):
    v = jax.nn.gelu(gv_ref[...])
    vc = v - jnp.mean(v, axis=-1, keepdims=True)
    y = vc * lax.rsqrt(jnp.mean(vc * vc, axis=-1, keepdims=True) + EPS) * ng_ref[...]
    yb = y.astype(BF16)
    gd = GMLP_WIDTH // GMLP_GROUPS
    for g in range(GMLP_GROUPS):
        sl = slice(g * gd, (g + 1) * gd)
        mixed = jnp.dot(ws_ref[g].astype(BF16), yb[:, sl], preferred_element_type=F32) + bst_ref[:, g:g + 1]
        o_ref[:, sl] = (jax.nn.gelu(u_ref[:, sl]) * mixed).astype(o_ref.dtype)


def spatial_gating(p, norm_g, ws, bs, layer, *, nrows, u_col0, gv_col0):
    w = GMLP_WIDTH
    bst = jnp.transpose(bs[layer])
    return pl.pallas_call(
        _gmlp_kernel,
        grid=(nrows // CHUNK,),
        in_specs=[
            pl.BlockSpec((CHUNK, w), lambda i: (i, u_col0 // w)),
            pl.BlockSpec((CHUNK, w), lambda i: (i, gv_col0 // w)),
            pl.BlockSpec((None, 1, w), lambda i: (layer, 0, 0)),
            pl.BlockSpec((None, GMLP_GROUPS, CHUNK, CHUNK), lambda i: (layer, 0, 0, 0)),
            pl.BlockSpec((CHUNK, GMLP_GROUPS), lambda i: (0, 0)),
        ],
        out_specs=pl.BlockSpec((CHUNK, w), lambda i: (i, 0)),
        out_shape=jax.ShapeDtypeStruct((nrows, w), BF16),
        compiler_params=_params("arbitrary"),
        name="spatial_gating",
    )(p, p, norm_g.reshape(norm_g.shape[0], 1, w), ws, bst)


def _split_bf16(x):
    hi = x.astype(BF16)
    return hi, (x - hi.astype(F32)).astype(BF16)


def _norm_router_kernel(x_ref, g_ref, sh_ref, sc_ref, rw_ref, h_ref, lg_ref):
    h = _rms(x_ref[...], g_ref[...]) * (1.0 + sc_ref[0]) + sh_ref[0]
    h_ref[...] = h
    hh, hl = _split_bf16(h)
    wh, wl = _split_bf16(rw_ref[...])
    lg = (jnp.dot(hh, wh, preferred_element_type=F32) + jnp.dot(hl, wh, preferred_element_type=F32)
          + jnp.dot(hh, wl, preferred_element_type=F32))
    lg_ref[...] = jnp.transpose(lg)[:N_EXPERTS, :]


def norm_mod_router(x, g, layer, mod3, rows, k_shift, k_scale, router_w, nrows):
    d = x.shape[1]
    tr = 256
    mod_row = rows.mod_row(tr)
    rw = jnp.pad(router_w[layer], ((0, 0), (0, LANES - N_EXPERTS)))
    return pl.pallas_call(
        _norm_router_kernel,
        grid=(nrows // tr,),
        in_specs=[
            pl.BlockSpec((tr, d), lambda i: (i, 0)),
            pl.BlockSpec((None, 1, d), lambda i: (layer, 0, 0)),
            pl.BlockSpec((1, 1, d), lambda i: (mod_row(i), 0, k_shift)),
            pl.BlockSpec((1, 1, d), lambda i: (mod_row(i), 0, k_scale)),
            pl.BlockSpec((d, LANES), lambda i: (0, 0)),
        ],
        out_specs=[pl.BlockSpec((tr, d), lambda i: (i, 0)), pl.BlockSpec((N_EXPERTS, tr), lambda i: (0, i))],
        out_shape=[jax.ShapeDtypeStruct((nrows, d), F32), jax.ShapeDtypeStruct((N_EXPERTS, nrows), F32)],
        compiler_params=_params("arbitrary"),
        name="norm_mod_router",
    )(x, g.reshape(g.shape[0], 1, d), mod3, mod3, rw)


def _cumsum_lanes(x):
    r, n = x.shape
    blk = min(n, LANES)
    tri = (lax.broadcasted_iota(I32, (blk, blk), 0) <= lax.broadcasted_iota(I32, (blk, blk), 1)).astype(BF16)
    off = jnp.zeros((r, 1), F32)
    out = []
    for j in range(n // blk):
        cs = jnp.dot(x[:, j * blk:(j + 1) * blk].astype(BF16), tri, preferred_element_type=F32) + off
        out.append(cs)
        off = cs[:, blk - 1:blk]
    return jnp.concatenate(out, axis=1) if len(out) > 1 else out[0]


def _route_kernel(lg_ref, idx_ref, dest_ref, gate_ref, start_ref, cnt_ref, slot_s, aff_s, *, cap):
    e_n, n = lg_ref.shape
    lg = lg_ref[...]
    ex = jnp.exp(lg - jnp.max(lg, axis=0, keepdims=True))
    aff = ex / jnp.sum(ex, axis=0, keepdims=True)
    bits = lax.bitcast_convert_type(aff, I32)

    thr = jnp.zeros((e_n, 1), I32)
    for bit in range(30, -1, -1):
        cand = thr | (1 << bit)
        n_ge = jnp.sum((bits >= cand).astype(F32), axis=1, keepdims=True)
        thr = jnp.where(n_ge >= cap, cand, thr)
    above = bits > thr
    tied = (bits == thr).astype(F32)
    need = cap - jnp.sum(above.astype(F32), axis=1, keepdims=True)
    tied_before = _cumsum_lanes(tied) - tied
    sel = jnp.where(above, 1.0, jnp.where(tied_before < need, tied, 0.0))

    slot = _cumsum_lanes(sel) - sel
    per_tok = jnp.sum(sel, axis=0, keepdims=True)
    per_tok8 = jnp.broadcast_to(per_tok, (8, n))
    start = (_cumsum_lanes(per_tok8) - per_tok8)[0:1]
    start_ref[...] = start.astype(I32)
    cnt_ref[...] = per_tok.astype(I32)
    slot_s[...] = jnp.where(sel > 0.0, slot, -1.0)
    aff_s[...] = aff

    tok = lax.broadcasted_iota(I32, (1, n), 1).astype(F32)
    cc = min(cap, LANES)

    def body(e, chosen_by_earlier):
        srow = slot_s[pl.ds(e, 1), :]
        arow = aff_s[pl.ds(e, 1), :]
        drow = start + chosen_by_earlier
        for c0 in range(0, cap, cc):
            want = (lax.broadcasted_iota(I32, (cc, 1), 0) + c0).astype(F32)
            hit = srow == want
            pick = lambda row: jnp.sum(jnp.where(hit, row, 0.0), axis=1, keepdims=True)
            idx_ref[e, pl.ds(c0, cc), :] = pick(tok).astype(I32)
            dest_ref[e, pl.ds(c0, cc), :] = pick(drow).astype(I32)
            gate_ref[e, pl.ds(c0, cc), :] = pick(arow)
        return chosen_by_earlier + (srow >= 0.0).astype(F32)

    lax.fori_loop(0, e_n, body, jnp.zeros((1, n), F32))


def expert_choice_route(logits_t, *, batch, n, col0):
    e_n = logits_t.shape[0]
    cap = CAPACITY_FACTOR * n // e_n
    cb0 = col0 // n
    per_slot = lambda dt: jax.ShapeDtypeStruct((batch, e_n, cap, 1), dt)
    per_tok = jax.ShapeDtypeStruct((batch, 1, n), I32)
    slot_spec = pl.BlockSpec((None, e_n, cap, 1), lambda b: (b, 0, 0, 0))
    tok_spec = pl.BlockSpec((None, 1, n), lambda b: (b, 0, 0))
    return pl.pallas_call(
        functools.partial(_route_kernel, cap=cap),
        grid=(batch,),
        in_specs=[pl.BlockSpec((e_n, n), lambda b: (0, cb0 + b))],
        out_specs=[slot_spec, slot_spec, slot_spec, tok_spec, tok_spec],
        out_shape=[per_slot(I32), per_slot(I32), per_slot(F32), per_tok, per_tok],
        scratch_shapes=[pltpu.VMEM((e_n, n), F32), pltpu.VMEM((e_n, n), F32)],
        compiler_params=_params("arbitrary"),
        name="expert_choice_route",
    )(logits_t)


ROWS_PER_STEP = 256


def _row_copies(n, make_copy):
    def start(i, carry):
        make_copy(i).start()
        return carry

    def wait(i, carry):
        make_copy(i).wait()
        return carry

    lax.fori_loop(0, n, start, 0)
    lax.fori_loop(0, n, wait, 0)


def _gather_kernel(idx_ref, src_ref, o_ref, sem):
    rows = o_ref.shape[0]
    base = pl.program_id(0) * rows
    _row_copies(rows, lambda i: pltpu.make_async_copy(
        src_ref.at[pl.ds(idx_ref[base + i], 1), :], o_ref.at[pl.ds(i, 1), :], sem))


def gather_rows(src, idx):
    n, d = idx.shape[0], src.shape[1]
    rows = min(n, ROWS_PER_STEP)
    return pl.pallas_call(
        _gather_kernel,
        grid_spec=pltpu.PrefetchScalarGridSpec(
            num_scalar_prefetch=1,
            grid=(n // rows,),
            in_specs=[pl.BlockSpec(memory_space=pl.ANY)],
            out_specs=pl.BlockSpec((rows, d), lambda i, idx_ref: (i, 0)),
            scratch_shapes=[pltpu.SemaphoreType.DMA(())],
        ),
        out_shape=jax.ShapeDtypeStruct((n, d), src.dtype),
        compiler_params=_params("arbitrary"),
        name="gather_rows",
    )(idx, src)


def _scatter_kernel(idx_ref, x_ref, o_ref, sem):
    rows = x_ref.shape[0]
    base = pl.program_id(0) * rows
    _row_copies(rows, lambda i: pltpu.make_async_copy(
        x_ref.at[pl.ds(i, 1), :], o_ref.at[pl.ds(idx_ref[base + i], 1), :], sem))


def scatter_rows(x, idx):
    n, d = x.shape
    rows = min(n, ROWS_PER_STEP)
    return pl.pallas_call(
        _scatter_kernel,
        grid_spec=pltpu.PrefetchScalarGridSpec(
            num_scalar_prefetch=1,
            grid=(n // rows,),
            in_specs=[pl.BlockSpec((rows, d), lambda i, idx_ref: (i, 0))],
            out_specs=pl.BlockSpec(memory_space=pl.ANY),
            scratch_shapes=[pltpu.SemaphoreType.DMA(())],
        ),
        out_shape=jax.ShapeDtypeStruct((n, d), x.dtype),
        compiler_params=_params("arbitrary"),
        name="scatter_rows",
    )(idx, x)


def _ffn_kernel(xg_ref, w1_ref, w3_ref, w2_ref, g_ref, y_ref, xs_s, hid_s, *, f0, f1, mchunk, tf):
    s = pl.program_id(1)

    @pl.when(s < f0)
    def _():
        r = pl.multiple_of(s * mchunk, mchunk)
        xs_s[pl.ds(r, mchunk), :] = xg_ref[...].astype(BF16)

    @pl.when(jnp.logical_and(s >= f0, s < f0 + f1))
    def _():
        xs = xs_s[...]
        h1 = jnp.dot(xs, w1_ref[...].astype(BF16), preferred_element_type=F32)
        h3 = jnp.dot(xs, w3_ref[...].astype(BF16), preferred_element_type=F32)
        hid_s[s - f0] = (jax.nn.silu(h1) * h3).astype(BF16)

    @pl.when(s >= f0 + f1)
    def _():
        acc = jnp.dot(hid_s[0], w2_ref[0:tf, :].astype(BF16), preferred_element_type=F32)
        for f in range(1, f1):
            acc += jnp.dot(hid_s[f], w2_ref[f * tf:(f + 1) * tf, :].astype(BF16), preferred_element_type=F32)
        y_ref[...] = acc * g_ref[...]


def expert_ffn(xg, w1, w3, w2, gate, layer):
    e_n, m, d = xg.shape
    ff = w1.shape[3]
    tf, tn = min(ff, 256), min(d, 512)
    mchunk = min(m, 256)
    f0, f1, f2 = m // mchunk, ff // tf, d // tn
    out_step = lambda s: jnp.maximum(s - f0 - f1, 0)
    return pl.pallas_call(
        functools.partial(_ffn_kernel, f0=f0, f1=f1, mchunk=mchunk, tf=tf),
        grid=(e_n, f0 + f1 + f2),
        in_specs=[
            pl.BlockSpec((None, mchunk, d), lambda e, s: (e, jnp.minimum(s, f0 - 1), 0)),
            pl.BlockSpec((None, None, d, tf), lambda e, s: (layer, e, 0, jnp.clip(s - f0, 0, f1 - 1))),
            pl.BlockSpec((None, None, d, tf), lambda e, s: (layer, e, 0, jnp.clip(s - f0, 0, f1 - 1))),
            pl.BlockSpec((None, None, ff, tn), lambda e, s: (layer, e, 0, out_step(s))),
            pl.BlockSpec((None, m, 1), lambda e, s: (e, 0, 0)),
        ],
        out_specs=pl.BlockSpec((None, m, tn), lambda e, s: (e, 0, out_step(s))),
        out_shape=jax.ShapeDtypeStruct((e_n, m, d), F32),
        scratch_shapes=[pltpu.VMEM((m, d), BF16), pltpu.VMEM((f1, m, tf), BF16)],
        compiler_params=_params("arbitrary", "arbitrary"),
        name="expert_ffn",
    )(xg, w1, w3, w2, gate)


def _combine_kernel(cf_ref, cn_ref, yg_ref, st_ref, ct_ref, x_ref, gt_ref, o_ref, acc_s, *, tiles, ch, kmax):
    b, t, k = pl.program_id(0), pl.program_id(1), pl.program_id(2)
    tile = b * tiles + t

    @pl.when(k == 0)
    def _():
        acc_s[...] = jnp.zeros_like(acc_s)

    @pl.when(k < cn_ref[tile])
    def _():
        pair = (cf_ref[tile] + k) * ch + lax.broadcasted_iota(I32, (1, ch), 1)
        rel = pair - st_ref[...]
        own = jnp.where(jnp.logical_and(rel >= 0, rel < ct_ref[...]), 1.0, 0.0).astype(BF16)
        yh, yl = _split_bf16(yg_ref[...])
        acc_s[...] += (jnp.dot(own, yh, preferred_element_type=F32) + jnp.dot(own, yl, preferred_element_type=F32))

    @pl.when(k == kmax - 1)
    def _():
        o_ref[...] = x_ref[...] + gt_ref[0] * acc_s[...]


def combine_residual(yg, start, cnt, x, mod3, rows, k_gate, *, batch, n, row0):
    d = x.shape[1]
    pairs = yg.shape[0] // batch
    tt = min(n, 128)
    ch = min(pairs, 256)
    tiles = n // tt
    kmax = tt * N_EXPERTS // ch + 1
    n_chunks = pairs // ch
    st = start.reshape(batch, n)
    ct = cnt.reshape(batch, n)
    first = st[:, ::tt]
    last = jnp.concatenate([first[:, 1:], jnp.full((batch, 1), pairs, I32)], axis=1)
    cfirst = jnp.minimum(first // ch, n_chunks - 1)
    cnum = jnp.where(last > first, (last - 1) // ch - first // ch + 1, 0)
    cfirst, cnum = cfirst.reshape(-1).astype(I32), cnum.reshape(-1).astype(I32)
    mod_row = rows.mod_row(tt)
    rt0 = row0 // tt
    gate_col = k_gate

    def yg_map(b, t, k, cf, cn):
        tile = b * tiles + t
        return (b * n_chunks + jnp.minimum(cf[tile] + k, cf[tile] + jnp.maximum(cn[tile] - 1, 0)), 0)

    row_map = lambda b, t, k, cf, cn: (rt0 + b * tiles + t, 0)
    tok_map = lambda b, t, k, cf, cn: (b * tiles + t, 0)
    return pl.pallas_call(
        functools.partial(_combine_kernel, tiles=tiles, ch=ch, kmax=kmax),
        grid_spec=pltpu.PrefetchScalarGridSpec(
            num_scalar_prefetch=2,
            grid=(batch, tiles, kmax),
            in_specs=[
                pl.BlockSpec((ch, d), yg_map),
                pl.BlockSpec((tt, 1), tok_map),
                pl.BlockSpec((tt, 1), tok_map),
                pl.BlockSpec((tt, d), row_map),
                pl.BlockSpec((1, 1, d), lambda b, t, k, cf, cn: (mod_row(rt0 + b * tiles + t), 0, gate_col)),
            ],
            out_specs=pl.BlockSpec((tt, d), row_map),
            scratch_shapes=[pltpu.VMEM((tt, d), F32)],
        ),
        out_shape=jax.ShapeDtypeStruct(x.shape, F32),
        input_output_aliases={5: 0},
        compiler_params=_params("arbitrary", "arbitrary", "arbitrary"),
        name="combine_residual",
    )(cfirst, cnum, yg, st.reshape(batch * n, 1), ct.reshape(batch * n, 1), x, mod3)


def expert_choice_ffn_residual(x, h, logits_t, mod3, rows, k_gate, w1, w3, w2, layer, *, batch, n, row0):
    e_n = logits_t.shape[0]
    d = x.shape[1]
    idx, dest, gate, start, cnt = expert_choice_route(logits_t, batch=batch, n=n, col0=row0)
    cap = idx.shape[2]
    pairs = e_n * cap
    boff = jnp.arange(batch, dtype=I32)[:, None, None]
    src_rows = jnp.transpose(idx[..., 0] + row0 + boff * n, (1, 0, 2)).reshape(-1)
    pair_rows = jnp.transpose(dest[..., 0] + boff * pairs, (1, 0, 2)).reshape(-1)
    gate_col = jnp.transpose(gate, (1, 0, 2, 3)).reshape(e_n, batch * cap, 1)
    xg = gather_rows(h, src_rows).reshape(e_n, batch * cap, d)
    y = expert_ffn(xg, w1, w3, w2, gate_col, layer).reshape(batch * pairs, d)
    yg = scatter_rows(y, pair_rows)
    return combine_residual(yg, start, cnt, x, mod3, rows, k_gate, batch=batch, n=n, row0=row0)


def kernel(x, c, ctx, c_ctx, ada_w, ada_b, norm1_g, norm2_g, w_in, na_rpb, gmlp_norm_g, gmlp_ws, gmlp_bs,
           w_branch_a, w_branch_b, w_out, router_w, exp_w1, exp_w3, exp_w2, final_norm_g):
    batch, n_lat, d = x.shape
    n_ctx = ctx.shape[1]
    depth = ada_w.shape[0]
    rows = Rows(batch, n_lat, n_ctx)
    lat_rows, all_rows = rows.lat_rows, rows.all_rows
    off_k, off_v, off_u = NA_WIDTH, 2 * NA_WIDTH, 3 * NA_WIDTH
    off_gv = off_u + GMLP_WIDTH
    off_ga = off_gv + GMLP_WIDTH
    off_gb = off_ga + d
    in_cols = off_gb + d

    stream = jnp.concatenate([x.reshape(lat_rows, d), ctx.reshape(batch * n_ctx, d)], axis=0)
    cv = jnp.zeros((MOD_ROWS, d), F32).at[:batch].set(c).at[batch].set(c_ctx)

    for layer in range(depth):
        last = layer == depth - 1
        nrows = lat_rows if last else all_rows
        mod3 = ada_mod(cv, ada_w, ada_b, layer).reshape(MOD_ROWS, 1, N_MOD * d)

        h = norm_mod(stream, norm1_g, layer, mod3, rows, 0, 1, 0, nrows, BF16)
        qkv = matmul(h, w_in, layer, row0=0, nrows=nrows, col0=0, ncols=off_u, out_dtype=BF16)
        rest = matmul(h, w_in, layer, row0=0, nrows=nrows, col0=off_u, ncols=in_cols - off_u, out_dtype=F32)
        bias_table = build_bias_table(na_rpb[layer], n_lat // GRID_W)
        if last:
            h_ctx = norm_mod(stream, norm1_g, layer, mod3, rows, 0, 1, lat_rows, batch * n_ctx, BF16)
            kvc = matmul(h_ctx, w_in, layer, row0=0, nrows=batch * n_ctx, col0=off_k, ncols=2 * NA_WIDTH,
                         out_dtype=BF16)
            o_a = neighbourhood_attention(qkv, kvc, bias_table, batch=batch, n_lat=n_lat, n_ctx=n_ctx,
                                          kc_row0=0, kc_col0=0, vc_col0=NA_WIDTH)
        else:
            o_lat = neighbourhood_attention(qkv, qkv, bias_table, batch=batch, n_lat=n_lat, n_ctx=n_ctx,
                                            kc_row0=lat_rows, kc_col0=off_k, vc_col0=off_v)
            o_ctx = context_attention(qkv, batch=batch, n_ctx=n_ctx, row0=lat_rows)
            o_a = jnp.concatenate([o_lat, o_ctx], axis=0)
        o_b = spatial_gating(rest, gmlp_norm_g, gmlp_ws, gmlp_bs, layer, nrows=nrows, u_col0=0,
                             gv_col0=off_gv - off_u)
        y = gated_merge(o_a, o_b, w_branch_a, w_branch_b, rest, layer, nrows=nrows,
                        ga_col0=off_ga - off_u, gb_col0=off_gb - off_u)
        stream = outproj_residual(y, w_out, stream, mod3, layer, rows, 2, nrows=nrows)

        h2, logits_t = norm_mod_router(stream, norm2_g, layer, mod3, rows, 3, 4, router_w, nrows)
        stream = expert_choice_ffn_residual(stream, h2, logits_t, mod3, rows, 5, exp_w1, exp_w3, exp_w2, layer,
                                            batch=batch, n=n_lat, row0=0)
        if not last:
            stream = expert_choice_ffn_residual(stream, h2, logits_t, mod3, rows, 5, exp_w1, exp_w3, exp_w2, layer,
                                                batch=batch, n=n_ctx, row0=lat_rows)

    return final_norm(stream, final_norm_g, lat_rows).reshape(batch, n_lat, d)
```

```python
import functools

import jax
import jax.numpy as jnp
from jax import lax
from jax.experimental import pallas as pl
from jax.experimental.pallas import tpu as pltpu

F32, BF16, I32 = jnp.float32, jnp.bfloat16, jnp.int32

GRID_W = 64
NA_HEADS = 16
NA_HEAD_DIM = 128
NA_WIDTH = NA_HEADS * NA_HEAD_DIM
WIN_H = 8
WIN_W = 16
GMLP_GROUPS = 16
GMLP_WIDTH = 2048
CHUNK = 128
N_EXPERTS = 16
CAPACITY_FACTOR = 2
N_MOD = 6
EPS = 1e-6
MASK_BIAS = -1e30

V7X_VMEM_LIMIT_BYTES = 56 * 1024 * 1024
LANES = 128
MOD_ROWS = 8


def _params(*sem):
    return pltpu.CompilerParams(dimension_semantics=sem, vmem_limit_bytes=V7X_VMEM_LIMIT_BYTES)


def _cast_weight(w_ref, wb_ref):
    rows = w_ref.shape[0]
    step = min(rows, 512)

    def body(i, carry):
        r = pl.multiple_of(i * step, step)
        wb_ref[pl.ds(r, step), :] = w_ref[pl.ds(r, step), :].astype(BF16)
        return carry

    lax.fori_loop(0, rows // step, body, 0)


def _ada_kernel(c_ref, w_ref, b_ref, o_ref):
    c = c_ref[...]
    s = c * jax.nn.sigmoid(c)
    o_ref[...] = jnp.dot(s.astype(BF16), w_ref[...].astype(BF16), preferred_element_type=F32) + b_ref[...]


def ada_mod(cv, ada_w, ada_b, layer):
    _, d, m = ada_w.shape
    tn = min(m, 512)
    return pl.pallas_call(
        _ada_kernel,
        grid=(m // tn,),
        in_specs=[
            pl.BlockSpec((MOD_ROWS, d), lambda j: (0, 0)),
            pl.BlockSpec((None, d, tn), lambda j: (layer, 0, j)),
            pl.BlockSpec((None, 1, tn), lambda j: (layer, 0, j)),
        ],
        out_specs=pl.BlockSpec((MOD_ROWS, tn), lambda j: (0, j)),
        out_shape=jax.ShapeDtypeStruct((MOD_ROWS, m), F32),
        compiler_params=_params("arbitrary"),
        name="ada_mod",
    )(cv, ada_w, ada_b.reshape(ada_b.shape[0], 1, m))


class Rows:
    def __init__(self, batch, n_lat, n_ctx):
        self.batch, self.n_lat, self.n_ctx = batch, n_lat, n_ctx
        self.lat_rows = batch * n_lat
        self.all_rows = self.lat_rows + batch * n_ctx

    def mod_row(self, tile_rows):
        lat_tiles = self.lat_rows // tile_rows
        per_batch = self.n_lat // tile_rows
        batch = self.batch

        def f(i):
            return jnp.where(i < lat_tiles, i // per_batch, batch)

        return f


def _mod_spec(mod_row, k, d):
    return pl.BlockSpec((1, 1, d), lambda i: (mod_row(i), 0, k))


def _rms(x, g):
    return x * lax.rsqrt(jnp.mean(x * x, axis=-1, keepdims=True) + EPS) * g


def _norm_mod_kernel(x_ref, g_ref, sh_ref, sc_ref, o_ref):
    y = _rms(x_ref[...], g_ref[...])
    o_ref[...] = (y * (1.0 + sc_ref[0]) + sh_ref[0]).astype(o_ref.dtype)


def norm_mod(x, g, layer, mod3, rows, k_shift, k_scale, row0, nrows, out_dtype):
    d = x.shape[1]
    tr = 256
    mod_row = rows.mod_row(tr)
    t0 = row0 // tr
    return pl.pallas_call(
        _norm_mod_kernel,
        grid=(nrows // tr,),
        in_specs=[
            pl.BlockSpec((tr, d), lambda i: (i + t0, 0)),
            pl.BlockSpec((None, 1, d), lambda i: (layer, 0, 0)),
            pl.BlockSpec((1, 1, d), lambda i: (mod_row(i + t0), 0, k_shift)),
            pl.BlockSpec((1, 1, d), lambda i: (mod_row(i + t0), 0, k_scale)),
        ],
        out_specs=pl.BlockSpec((tr, d), lambda i: (i, 0)),
        out_shape=jax.ShapeDtypeStruct((nrows, d), out_dtype),
        compiler_params=_params("arbitrary"),
        name="norm_mod",
    )(x, g.reshape(g.shape[0], 1, d), mod3, mod3)


def _final_norm_kernel(x_ref, g_ref, o_ref):
    o_ref[...] = _rms(x_ref[...], g_ref[...])


def final_norm(x, g, nrows):
    d = x.shape[1]
    tr = 256
    return pl.pallas_call(
        _final_norm_kernel,
        grid=(nrows // tr,),
        in_specs=[pl.BlockSpec((tr, d), lambda i: (i, 0)), pl.BlockSpec((1, d), lambda i: (0, 0))],
        out_specs=pl.BlockSpec((tr, d), lambda i: (i, 0)),
        out_shape=jax.ShapeDtypeStruct((nrows, d), F32),
        compiler_params=_params("arbitrary"),
        name="final_norm",
    )(x, g.reshape(1, d))


WEIGHT_TILE_BUFFERS = pl.Buffered(1)

def _mm_kernel(a_ref, w_ref, o_ref, wb_ref):
    @pl.when(pl.program_id(1) == 0)
    def _():
        _cast_weight(w_ref, wb_ref)

    o_ref[...] = jnp.dot(a_ref[...], wb_ref[...], preferred_element_type=F32).astype(o_ref.dtype)


def matmul(a, w, layer, *, row0, nrows, col0, ncols, out_dtype, tm=512, tn=1024):
    k = a.shape[1]
    tm, tn = min(tm, nrows), min(tn, ncols)
    r0, c0 = row0 // tm, col0 // tn
    return pl.pallas_call(
        _mm_kernel,
        grid=(ncols // tn, nrows // tm),
        in_specs=[
            pl.BlockSpec((tm, k), lambda j, i: (i + r0, 0)),
            pl.BlockSpec((None, k, tn), lambda j, i: (layer, 0, j + c0), pipeline_mode=WEIGHT_TILE_BUFFERS),
        ],
        out_specs=pl.BlockSpec((tm, tn), lambda j, i: (i, j)),
        out_shape=jax.ShapeDtypeStruct((nrows, ncols), out_dtype),
        scratch_shapes=[pltpu.VMEM((k, tn), BF16)],
        compiler_params=_params("arbitrary", "arbitrary"),
        name="matmul",
    )(a, w)


def _merge_kernel(oa_ref, ob_ref, wa_ref, wb_ref, ga_ref, gb_ref, y_ref, wab_ref, wbb_ref):
    @pl.when(pl.program_id(1) == 0)
    def _():
        _cast_weight(wa_ref, wab_ref)
        _cast_weight(wb_ref, wbb_ref)

    ya = jnp.dot(oa_ref[...], wab_ref[...], preferred_element_type=F32)
    yb = jnp.dot(ob_ref[...], wbb_ref[...], preferred_element_type=F32)
    y = jax.nn.sigmoid(ga_ref[...]) * ya + jax.nn.sigmoid(gb_ref[...]) * yb
    y_ref[...] = y.astype(y_ref.dtype)


def gated_merge(o_a, o_b, w_a, w_b, gates, layer, *, nrows, ga_col0, gb_col0, tm=512, tn=1024):
    ka, kb = o_a.shape[1], o_b.shape[1]
    d = w_a.shape[2]
    tm, tn = min(tm, nrows), min(tn, d)
    ca, cb = ga_col0 // tn, gb_col0 // tn
    return pl.pallas_call(
        _merge_kernel,
        grid=(d // tn, nrows // tm),
        in_specs=[
            pl.BlockSpec((tm, ka), lambda j, i: (i, 0)),
            pl.BlockSpec((tm, kb), lambda j, i: (i, 0)),
            pl.BlockSpec((None, ka, tn), lambda j, i: (layer, 0, j), pipeline_mode=WEIGHT_TILE_BUFFERS),
            pl.BlockSpec((None, kb, tn), lambda j, i: (layer, 0, j), pipeline_mode=WEIGHT_TILE_BUFFERS),
            pl.BlockSpec((tm, tn), lambda j, i: (i, j + ca)),
            pl.BlockSpec((tm, tn), lambda j, i: (i, j + cb)),
        ],
        out_specs=pl.BlockSpec((tm, tn), lambda j, i: (i, j)),
        out_shape=jax.ShapeDtypeStruct((nrows, d), BF16),
        scratch_shapes=[pltpu.VMEM((ka, tn), BF16), pltpu.VMEM((kb, tn), BF16)],
        compiler_params=_params("arbitrary", "arbitrary"),
        name="gated_merge",
    )(o_a, o_b, w_a, w_b, gates, gates)


def _outproj_kernel(y_ref, w_ref, x_ref, gt_ref, o_ref, wb_ref):
    @pl.when(pl.program_id(1) == 0)
    def _():
        _cast_weight(w_ref, wb_ref)

    o_ref[...] = x_ref[...] + gt_ref[0] * jnp.dot(y_ref[...], wb_ref[...], preferred_element_type=F32)


def outproj_residual(y, w_out, x, mod3, layer, rows, k_gate, *, nrows, tm=512, tn=1024):
    k = y.shape[1]
    d = w_out.shape[2]
    tm, tn = min(tm, nrows), min(tn, d)
    mod_row = rows.mod_row(tm)
    gate_col0 = k_gate * (d // tn)
    return pl.pallas_call(
        _outproj_kernel,
        grid=(d // tn, nrows // tm),
        in_specs=[
            pl.BlockSpec((tm, k), lambda j, i: (i, 0)),
            pl.BlockSpec((None, k, tn), lambda j, i: (layer, 0, j), pipeline_mode=WEIGHT_TILE_BUFFERS),
            pl.BlockSpec((tm, tn), lambda j, i: (i, j)),
            pl.BlockSpec((1, 1, tn), lambda j, i: (mod_row(i), 0, j + gate_col0)),
        ],
        out_specs=pl.BlockSpec((tm, tn), lambda j, i: (i, j)),
        out_shape=jax.ShapeDtypeStruct((nrows, d), F32),
        scratch_shapes=[pltpu.VMEM((k, tn), BF16)],
        compiler_params=_params("arbitrary", "arbitrary"),
        name="outproj_residual",
    )(y, w_out, x, mod3)


Q_ROWS = 4
BAND_ROWS = WIN_H + Q_ROWS
N_PATTERNS = 3


def build_col_table(rpb):
    cols = jnp.arange(GRID_W)
    col_start = jnp.clip(cols - WIN_W // 2, 0, GRID_W - WIN_W)
    col_ok = (cols[None, :] >= col_start[:, None]) & (cols[None, :] < col_start[:, None] + WIN_W)
    col_off = jnp.clip(cols[None, :] - cols[:, None] + (WIN_W - 1), 0, 2 * WIN_W - 2)
    return jnp.where(col_ok[None, None], rpb[:, :, col_off], MASK_BIAS).astype(F32)


def _band_patterns(grid_rows):
    half = WIN_H // 2
    groups = grid_rows // Q_ROWS
    assert grid_rows % Q_ROWS == 0 and grid_rows >= BAND_ROWS
    patterns = []
    for g in (0, min(1, groups - 1), groups - 1):
        band_start = min(max(g * Q_ROWS - half, 0), grid_rows - BAND_ROWS)
        pattern = []
        for a in range(Q_ROWS):
            q_row = g * Q_ROWS + a
            win_start = min(max(q_row - half, 0), grid_rows - WIN_H)
            pattern.append([band_start + i - q_row + WIN_H - 1 if win_start <= band_start + i < win_start + WIN_H
                            else None for i in range(BAND_ROWS)])
        patterns.append(pattern)
    return patterns


def _softmax_pv(parts):
    m = functools.reduce(jnp.maximum, [jnp.max(s, axis=-1, keepdims=True) for s, _ in parts])
    es = [jnp.exp(s - m) for s, _ in parts]
    denom = functools.reduce(jnp.add, [jnp.sum(e, axis=-1, keepdims=True) for e in es])
    acc = functools.reduce(jnp.add, [jnp.dot(e.astype(BF16), v, preferred_element_type=F32)
                                     for e, (_, v) in zip(es, parts)])
    return acc / denom


def _qk(q, k):
    return lax.dot_general(q, k, (((1,), (1,)), ((), ())), preferred_element_type=F32) * (NA_HEAD_DIM ** -0.5)


def _natten_kernel(q_ref, k_ref, v_ref, kc_ref, vc_ref, ct_ref, o_ref, bt_ref, *, grid_rows):
    groups = grid_rows // Q_ROWS
    nq, band = Q_ROWS * GRID_W, BAND_ROWS * GRID_W
    kc = kc_ref[...]
    vc = vc_ref[...]

    @pl.when(jnp.logical_and(pl.program_id(0) == 0, pl.program_id(1) == 0))
    def _():
        bt_ref[...] = jnp.full(bt_ref.shape, MASK_BIAS, F32)

    for p, pattern in enumerate(_band_patterns(grid_rows)):
        for a, row in enumerate(pattern):
            for i, off in enumerate(row):
                if off is not None:
                    bt_ref[p, a * GRID_W:(a + 1) * GRID_W, i * GRID_W:(i + 1) * GRID_W] = ct_ref[off]

    def body(g, carry):
        band_start = jnp.clip(g * Q_ROWS - WIN_H // 2, 0, grid_rows - BAND_ROWS)
        pattern = jnp.where(g == 0, 0, jnp.where(g == groups - 1, 2, 1))
        q0 = pl.multiple_of(g * nq, nq)
        k0 = pl.multiple_of(band_start * GRID_W, GRID_W)
        q = q_ref[pl.ds(q0, nq), :]
        kb = k_ref[pl.ds(k0, band), :]
        vb = v_ref[pl.ds(k0, band), :]
        s_loc = _qk(q, kb) + bt_ref[pattern]
        s_ctx = _qk(q, kc)
        o_ref[pl.ds(q0, nq), :] = _softmax_pv([(s_loc, vb), (s_ctx, vc)]).astype(o_ref.dtype)
        return carry

    lax.fori_loop(0, groups, body, 0)


def neighbourhood_attention(qkv, kvc, col_table, *, batch, n_lat, n_ctx, kc_row0, kc_col0, vc_col0):
    dh, h = NA_HEAD_DIM, NA_HEADS
    grid_rows = n_lat // GRID_W
    rb0, ck, cv = kc_row0 // n_ctx, kc_col0 // dh, vc_col0 // dh
    return pl.pallas_call(
        functools.partial(_natten_kernel, grid_rows=grid_rows),
        grid=(batch, h),
        in_specs=[
            pl.BlockSpec((n_lat, dh), lambda b, hh: (b, hh)),
            pl.BlockSpec((n_lat, dh), lambda b, hh: (b, h + hh)),
            pl.BlockSpec((n_lat, dh), lambda b, hh: (b, 2 * h + hh)),
            pl.BlockSpec((n_ctx, dh), lambda b, hh: (rb0 + b, ck + hh)),
            pl.BlockSpec((n_ctx, dh), lambda b, hh: (rb0 + b, cv + hh)),
            pl.BlockSpec((None, 2 * WIN_H - 1, GRID_W, GRID_W), lambda b, hh: (hh, 0, 0, 0)),
        ],
        out_specs=pl.BlockSpec((n_lat, dh), lambda b, hh: (b, hh)),
        out_shape=jax.ShapeDtypeStruct((batch * n_lat, NA_WIDTH), BF16),
        scratch_shapes=[pltpu.VMEM((N_PATTERNS, Q_ROWS * GRID_W, BAND_ROWS * GRID_W), F32)],
        compiler_params=_params("arbitrary", "arbitrary"),
        name="neighbourhood_attention",
    )(qkv, qkv, qkv, kvc, kvc, col_table)


def _ctx_attn_kernel(q_ref, k_ref, v_ref, o_ref):
    o_ref[...] = _softmax_pv([(_qk(q_ref[...], k_ref[...]), v_ref[...])]).astype(o_ref.dtype)


def context_attention(qkv, *, batch, n_ctx, row0):
    dh, h = NA_HEAD_DIM, NA_HEADS
    rb0 = row0 // n_ctx
    return pl.pallas_call(
        _ctx_attn_kernel,
        grid=(batch, h),
        in_specs=[
            pl.BlockSpec((n_ctx, dh), lambda b, hh: (rb0 + b, hh)),
            pl.BlockSpec((n_ctx, dh), lambda b, hh: (rb0 + b, h + hh)),
            pl.BlockSpec((n_ctx, dh), lambda b, hh: (rb0 + b, 2 * h + hh)),
        ],
        out_specs=pl.BlockSpec((n_ctx, dh), lambda b, hh: (b, hh)),
        out_shape=jax.ShapeDtypeStruct((batch * n_ctx, NA_WIDTH), BF16),
        compiler_params=_params("arbitrary", "arbitrary"),
        name="context_attention",
    )(qkv, qkv, qkv)


def _gmlp_kernel(u_ref, gv_ref, ng_ref, ws_ref, bst_ref, o_ref):
    v = jax.nn.gelu(gv_ref[...])
    vc = v - jnp.mean(v, axis=-1, keepdims=True)
    y = vc * lax.rsqrt(jnp.mean(vc * vc, axis=-1, keepdims=True) + EPS) * ng_ref[...]
    yb = y.astype(BF16)
    gd = GMLP_WIDTH // GMLP_GROUPS
    for g in range(GMLP_GROUPS):
        sl = slice(g * gd, (g + 1) * gd)
        mixed = jnp.dot(ws_ref[g].astype(BF16), yb[:, sl], preferred_element_type=F32) + bst_ref[:, g:g + 1]
        o_ref[:, sl] = (jax.nn.gelu(u_ref[:, sl]) * mixed).astype(o_ref.dtype)


def spatial_gating(p, norm_g, ws, bs, layer, *, nrows, u_col0, gv_col0):
    w = GMLP_WIDTH
    bst = jnp.transpose(bs[layer])
    return pl.pallas_call(
        _gmlp_kernel,
        grid=(nrows // CHUNK,),
        in_specs=[
            pl.BlockSpec((CHUNK, w), lambda i: (i, u_col0 // w)),
            pl.BlockSpec((CHUNK, w), lambda i: (i, gv_col0 // w)),
            pl.BlockSpec((None, 1, w), lambda i: (layer, 0, 0)),
            pl.BlockSpec((None, GMLP_GROUPS, CHUNK, CHUNK), lambda i: (layer, 0, 0, 0)),
            pl.BlockSpec((CHUNK, GMLP_GROUPS), lambda i: (0, 0)),
        ],
        out_specs=pl.BlockSpec((CHUNK, w), lambda i: (i, 0)),
        out_shape=jax.ShapeDtypeStruct((nrows, w), BF16),
        compiler_params=_params("arbitrary"),
        name="spatial_gating",
    )(p, p, norm_g.reshape(norm_g.shape[0], 1, w), ws, bst)


def _split_bf16(x):
    hi = x.astype(BF16)
    return hi, (x - hi.astype(F32)).astype(BF16)


def _norm_router_kernel(x_ref, g_ref, sh_ref, sc_ref, rw_ref, h_ref, lg_ref):
    h = _rms(x_ref[...], g_ref[...]) * (1.0 + sc_ref[0]) + sh_ref[0]
    h_ref[...] = h
    hh, hl = _split_bf16(h)
    wh, wl = _split_bf16(rw_ref[...])
    lg = (jnp.dot(hh, wh, preferred_element_type=F32) + jnp.dot(hl, wh, preferred_element_type=F32)
          + jnp.dot(hh, wl, preferred_element_type=F32))
    lg_ref[...] = jnp.transpose(lg)[:N_EXPERTS, :]


def norm_mod_router(x, g, layer, mod3, rows, k_shift, k_scale, router_w, nrows):
    d = x.shape[1]
    tr = 256
    mod_row = rows.mod_row(tr)
    rw = jnp.pad(router_w[layer], ((0, 0), (0, LANES - N_EXPERTS)))
    return pl.pallas_call(
        _norm_router_kernel,
        grid=(nrows // tr,),
        in_specs=[
            pl.BlockSpec((tr, d), lambda i: (i, 0)),
            pl.BlockSpec((None, 1, d), lambda i: (layer, 0, 0)),
            pl.BlockSpec((1, 1, d), lambda i: (mod_row(i), 0, k_shift)),
            pl.BlockSpec((1, 1, d), lambda i: (mod_row(i), 0, k_scale)),
            pl.BlockSpec((d, LANES), lambda i: (0, 0)),
        ],
        out_specs=[pl.BlockSpec((tr, d), lambda i: (i, 0)), pl.BlockSpec((N_EXPERTS, tr), lambda i: (0, i))],
        out_shape=[jax.ShapeDtypeStruct((nrows, d), F32), jax.ShapeDtypeStruct((N_EXPERTS, nrows), F32)],
        compiler_params=_params("arbitrary"),
        name="norm_mod_router",
    )(x, g.reshape(g.shape[0], 1, d), mod3, mod3, rw)


def _cumsum_lanes(x):
    r, n = x.shape
    blk = min(n, LANES)
    tri = (lax.broadcasted_iota(I32, (blk, blk), 0) <= lax.broadcasted_iota(I32, (blk, blk), 1)).astype(BF16)
    off = jnp.zeros((r, 1), F32)
    out = []
    for j in range(n // blk):
        cs = jnp.dot(x[:, j * blk:(j + 1) * blk].astype(BF16), tri, preferred_element_type=F32) + off
        out.append(cs)
        off = cs[:, blk - 1:blk]
    return jnp.concatenate(out, axis=1) if len(out) > 1 else out[0]


def _route_kernel(lg_ref, idx_ref, dest_ref, gate_ref, start_ref, cnt_ref, slot_s, aff_s, *, cap):
    e_n, n = lg_ref.shape
    lg = lg_ref[...]
    ex = jnp.exp(lg - jnp.max(lg, axis=0, keepdims=True))
    aff = ex / jnp.sum(ex, axis=0, keepdims=True)
    bits = lax.bitcast_convert_type(aff, I32)

    thr = jnp.zeros((e_n, 1), I32)
    for bit in range(30, -1, -1):
        cand = thr | (1 << bit)
        n_ge = jnp.sum((bits >= cand).astype(F32), axis=1, keepdims=True)
        thr = jnp.where(n_ge >= cap, cand, thr)
    above = bits > thr
    tied = (bits == thr).astype(F32)
    need = cap - jnp.sum(above.astype(F32), axis=1, keepdims=True)
    tied_before = _cumsum_lanes(tied) - tied
    sel = jnp.where(above, 1.0, jnp.where(tied_before < need, tied, 0.0))

    slot = _cumsum_lanes(sel) - sel
    per_tok = jnp.sum(sel, axis=0, keepdims=True)
    per_tok8 = jnp.broadcast_to(per_tok, (8, n))
    start = (_cumsum_lanes(per_tok8) - per_tok8)[0:1]
    start_ref[...] = start.astype(I32)
    cnt_ref[...] = per_tok.astype(I32)
    slot_s[...] = jnp.where(sel > 0.0, slot, -1.0)
    aff_s[...] = aff

    tok = lax.broadcasted_iota(I32, (1, n), 1).astype(F32)
    cc = min(cap, LANES)

    def body(e, chosen_by_earlier):
        srow = slot_s[pl.ds(e, 1), :]
        arow = aff_s[pl.ds(e, 1), :]
        drow = start + chosen_by_earlier
        for c0 in range(0, cap, cc):
            want = (lax.broadcasted_iota(I32, (cc, 1), 0) + c0).astype(F32)
            hit = srow == want
            pick = lambda row: jnp.sum(jnp.where(hit, row, 0.0), axis=1, keepdims=True)
            idx_ref[e, pl.ds(c0, cc), :] = pick(tok).astype(I32)
            dest_ref[e, pl.ds(c0, cc), :] = pick(drow).astype(I32)
            gate_ref[e, pl.ds(c0, cc), :] = pick(arow)
        return chosen_by_earlier + (srow >= 0.0).astype(F32)

    lax.fori_loop(0, e_n, body, jnp.zeros((1, n), F32))


def expert_choice_route(logits_t, *, batch, n, col0):
    e_n = logits_t.shape[0]
    cap = CAPACITY_FACTOR * n // e_n
    cb0 = col0 // n
    per_slot = lambda dt: jax.ShapeDtypeStruct((batch, e_n, cap, 1), dt)
    per_tok = jax.ShapeDtypeStruct((batch, 1, n), I32)
    slot_spec = pl.BlockSpec((None, e_n, cap, 1), lambda b: (b, 0, 0, 0))
    tok_spec = pl.BlockSpec((None, 1, n), lambda b: (b, 0, 0))
    return pl.pallas_call(
        functools.partial(_route_kernel, cap=cap),
        grid=(batch,),
        in_specs=[pl.BlockSpec((e_n, n), lambda b: (0, cb0 + b))],
        out_specs=[slot_spec, slot_spec, slot_spec, tok_spec, tok_spec],
        out_shape=[per_slot(I32), per_slot(I32), per_slot(F32), per_tok, per_tok],
        scratch_shapes=[pltpu.VMEM((e_n, n), F32), pltpu.VMEM((e_n, n), F32)],
        compiler_params=_params("arbitrary"),
        name="expert_choice_route",
    )(logits_t)


ROWS_PER_STEP = 256


def _row_copies(n, make_copy):
    def start(i, carry):
        make_copy(i).start()
        return carry

    def wait(i, carry):
        make_copy(i).wait()
        return carry

    lax.fori_loop(0, n, start, 0)
    lax.fori_loop(0, n, wait, 0)


def _gather_kernel(idx_ref, src_ref, o_ref, sem):
    rows = o_ref.shape[0]
    base = pl.program_id(0) * rows
    _row_copies(rows, lambda i: pltpu.make_async_copy(
        src_ref.at[pl.ds(idx_ref[base + i], 1), :], o_ref.at[pl.ds(i, 1), :], sem))


def gather_rows(src, idx):
    n, d = idx.shape[0], src.shape[1]
    rows = min(n, ROWS_PER_STEP)
    return pl.pallas_call(
        _gather_kernel,
        grid_spec=pltpu.PrefetchScalarGridSpec(
            num_scalar_prefetch=1,
            grid=(n // rows,),
            in_specs=[pl.BlockSpec(memory_space=pl.ANY)],
            out_specs=pl.BlockSpec((rows, d), lambda i, idx_ref: (i, 0)),
            scratch_shapes=[pltpu.SemaphoreType.DMA(())],
        ),
        out_shape=jax.ShapeDtypeStruct((n, d), src.dtype),
        compiler_params=_params("arbitrary"),
        name="gather_rows",
    )(idx, src)


def _scatter_kernel(idx_ref, x_ref, o_ref, sem):
    rows = x_ref.shape[0]
    base = pl.program_id(0) * rows
    _row_copies(rows, lambda i: pltpu.make_async_copy(
        x_ref.at[pl.ds(i, 1), :], o_ref.at[pl.ds(idx_ref[base + i], 1), :], sem))


def scatter_rows(x, idx):
    n, d = x.shape
    rows = min(n, ROWS_PER_STEP)
    return pl.pallas_call(
        _scatter_kernel,
        grid_spec=pltpu.PrefetchScalarGridSpec(
            num_scalar_prefetch=1,
            grid=(n // rows,),
            in_specs=[pl.BlockSpec((rows, d), lambda i, idx_ref: (i, 0))],
            out_specs=pl.BlockSpec(memory_space=pl.ANY),
            scratch_shapes=[pltpu.SemaphoreType.DMA(())],
        ),
        out_shape=jax.ShapeDtypeStruct((n, d), x.dtype),
        compiler_params=_params("arbitrary"),
        name="scatter_rows",
    )(idx, x)


def _ffn_kernel(xg_ref, w1_ref, w3_ref, w2_ref, g_ref, y_ref, xs_s, hid_s, *, f0, f1, mchunk, tf):
    s = pl.program_id(1)

    @pl.when(s < f0)
    def _():
        r = pl.multiple_of(s * mchunk, mchunk)
        xs_s[pl.ds(r, mchunk), :] = xg_ref[...].astype(BF16)

    @pl.when(jnp.logical_and(s >= f0, s < f0 + f1))
    def _():
        xs = xs_s[...]
        h1 = jnp.dot(xs, w1_ref[...].astype(BF16), preferred_element_type=F32)
        h3 = jnp.dot(xs, w3_ref[...].astype(BF16), preferred_element_type=F32)
        hid_s[s - f0] = (jax.nn.silu(h1) * h3).astype(BF16)

    @pl.when(s >= f0 + f1)
    def _():
        acc = jnp.dot(hid_s[0], w2_ref[0:tf, :].astype(BF16), preferred_element_type=F32)
        for f in range(1, f1):
            acc += jnp.dot(hid_s[f], w2_ref[f * tf:(f + 1) * tf, :].astype(BF16), preferred_element_type=F32)
        y_ref[...] = acc * g_ref[...]


def expert_ffn(xg, w1, w3, w2, gate, layer):
    e_n, m, d = xg.shape
    ff = w1.shape[3]
    tf, tn = min(ff, 256), min(d, 512)
    mchunk = min(m, 256)
    f0, f1, f2 = m // mchunk, ff // tf, d // tn
    out_step = lambda s: jnp.maximum(s - f0 - f1, 0)
    return pl.pallas_call(
        functools.partial(_ffn_kernel, f0=f0, f1=f1, mchunk=mchunk, tf=tf),
        grid=(e_n, f0 + f1 + f2),
        in_specs=[
            pl.BlockSpec((None, mchunk, d), lambda e, s: (e, jnp.minimum(s, f0 - 1), 0)),
            pl.BlockSpec((None, None, d, tf), lambda e, s: (layer, e, 0, jnp.clip(s - f0, 0, f1 - 1))),
            pl.BlockSpec((None, None, d, tf), lambda e, s: (layer, e, 0, jnp.clip(s - f0, 0, f1 - 1))),
            pl.BlockSpec((None, None, ff, tn), lambda e, s: (layer, e, 0, out_step(s))),
            pl.BlockSpec((None, m, 1), lambda e, s: (e, 0, 0)),
        ],
        out_specs=pl.BlockSpec((None, m, tn), lambda e, s: (e, 0, out_step(s))),
        out_shape=jax.ShapeDtypeStruct((e_n, m, d), F32),
        scratch_shapes=[pltpu.VMEM((m, d), BF16), pltpu.VMEM((f1, m, tf), BF16)],
        compiler_params=_params("arbitrary", "arbitrary"),
        name="expert_ffn",
    )(xg, w1, w3, w2, gate)


def _combine_kernel(cf_ref, cn_ref, yg_ref, st_ref, ct_ref, x_ref, gt_ref, o_ref, acc_s, *, tiles, ch, kmax):
    b, t, k = pl.program_id(0), pl.program_id(1), pl.program_id(2)
    tile = b * tiles + t

    @pl.when(k == 0)
    def _():
        acc_s[...] = jnp.zeros_like(acc_s)

    @pl.when(k < cn_ref[tile])
    def _():
        pair = (cf_ref[tile] + k) * ch + lax.broadcasted_iota(I32, (ch, 1), 0)
        rel = pair - st_ref[...]
        own_t = jnp.where(jnp.logical_and(rel >= 0, rel < ct_ref[...]), 1.0, 0.0).astype(BF16)
        acc_s[...] += lax.dot_general(own_t, yg_ref[...].astype(BF16), (((0,), (0,)), ((), ())),
                                      preferred_element_type=F32)

    @pl.when(k == kmax - 1)
    def _():
        o_ref[...] = x_ref[...] + gt_ref[0] * acc_s[...]


def combine_residual(yg, start, cnt, x, mod3, rows, k_gate, *, batch, n, row0):
    d = x.shape[1]
    pairs = yg.shape[0] // batch
    tt = min(n, 256)
    ch = min(pairs, 256)
    tiles = n // tt
    kmax = tt * N_EXPERTS // ch + 1
    n_chunks = pairs // ch
    first = start.reshape(batch, n)[:, ::tt]
    last = jnp.concatenate([first[:, 1:], jnp.full((batch, 1), pairs, I32)], axis=1)
    cfirst = jnp.minimum(first // ch, n_chunks - 1)
    cnum = jnp.where(last > first, (last - 1) // ch - first // ch + 1, 0)
    cfirst, cnum = cfirst.reshape(-1).astype(I32), cnum.reshape(-1).astype(I32)
    mod_row = rows.mod_row(tt)
    rt0 = row0 // tt
    gate_col = k_gate

    def yg_map(b, t, k, cf, cn):
        tile = b * tiles + t
        return (b * n_chunks + jnp.minimum(cf[tile] + k, cf[tile] + jnp.maximum(cn[tile] - 1, 0)), 0)

    row_map = lambda b, t, k, cf, cn: (rt0 + b * tiles + t, 0)
    tok_map = lambda b, t, k, cf, cn: (b, 0, t)
    return pl.pallas_call(
        functools.partial(_combine_kernel, tiles=tiles, ch=ch, kmax=kmax),
        grid_spec=pltpu.PrefetchScalarGridSpec(
            num_scalar_prefetch=2,
            grid=(batch, tiles, kmax),
            in_specs=[
                pl.BlockSpec((ch, d), yg_map),
                pl.BlockSpec((None, 1, tt), tok_map),
                pl.BlockSpec((None, 1, tt), tok_map),
                pl.BlockSpec((tt, d), row_map),
                pl.BlockSpec((1, 1, d), lambda b, t, k, cf, cn: (mod_row(rt0 + b * tiles + t), 0, gate_col)),
            ],
            out_specs=pl.BlockSpec((tt, d), row_map),
            scratch_shapes=[pltpu.VMEM((tt, d), F32)],
        ),
        out_shape=jax.ShapeDtypeStruct(x.shape, F32),
        input_output_aliases={5: 0},
        compiler_params=_params("arbitrary", "arbitrary", "arbitrary"),
        name="combine_residual",
    )(cfirst, cnum, yg, start, cnt, x, mod3)


def expert_choice_ffn_residual(x, h, logits_t, mod3, rows, k_gate, w1, w3, w2, layer, *, batch, n, row0):
    e_n = logits_t.shape[0]
    d = x.shape[1]
    idx, dest, gate, start, cnt = expert_choice_route(logits_t, batch=batch, n=n, col0=row0)
    cap = idx.shape[2]
    pairs = e_n * cap
    boff = jnp.arange(batch, dtype=I32)[:, None, None]
    src_rows = jnp.transpose(idx[..., 0] + row0 + boff * n, (1, 0, 2)).reshape(-1)
    pair_rows = jnp.transpose(dest[..., 0] + boff * pairs, (1, 0, 2)).reshape(-1)
    gate_col = jnp.transpose(gate, (1, 0, 2, 3)).reshape(e_n, batch * cap, 1)
    xg = gather_rows(h, src_rows).reshape(e_n, batch * cap, d)
    y = expert_ffn(xg, w1, w3, w2, gate_col, layer).reshape(batch * pairs, d)
    yg = scatter_rows(y, pair_rows)
    return combine_residual(yg, start, cnt, x, mod3, rows, k_gate, batch=batch, n=n, row0=row0)


def kernel(x, c, ctx, c_ctx, ada_w, ada_b, norm1_g, norm2_g, w_in, na_rpb, gmlp_norm_g, gmlp_ws, gmlp_bs,
           w_branch_a, w_branch_b, w_out, router_w, exp_w1, exp_w3, exp_w2, final_norm_g):
    batch, n_lat, d = x.shape
    n_ctx = ctx.shape[1]
    depth = ada_w.shape[0]
    rows = Rows(batch, n_lat, n_ctx)
    lat_rows, all_rows = rows.lat_rows, rows.all_rows
    off_k, off_v, off_u = NA_WIDTH, 2 * NA_WIDTH, 3 * NA_WIDTH
    off_gv = off_u + GMLP_WIDTH
    off_ga = off_gv + GMLP_WIDTH
    off_gb = off_ga + d
    in_cols = off_gb + d

    stream = jnp.concatenate([x.reshape(lat_rows, d), ctx.reshape(batch * n_ctx, d)], axis=0)
    cv = jnp.zeros((MOD_ROWS, d), F32).at[:batch].set(c).at[batch].set(c_ctx)

    for layer in range(depth):
        last = layer == depth - 1
        nrows = lat_rows if last else all_rows
        mod3 = ada_mod(cv, ada_w, ada_b, layer).reshape(MOD_ROWS, 1, N_MOD * d)

        h = norm_mod(stream, norm1_g, layer, mod3, rows, 0, 1, 0, nrows, BF16)
        qkv = matmul(h, w_in, layer, row0=0, nrows=nrows, col0=0, ncols=off_u, out_dtype=BF16)
        rest = matmul(h, w_in, layer, row0=0, nrows=nrows, col0=off_u, ncols=in_cols - off_u, out_dtype=F32)
        bias_table = build_col_table(na_rpb[layer])
        if last:
            h_ctx = norm_mod(stream, norm1_g, layer, mod3, rows, 0, 1, lat_rows, batch * n_ctx, BF16)
            kvc = matmul(h_ctx, w_in, layer, row0=0, nrows=batch * n_ctx, col0=off_k, ncols=2 * NA_WIDTH,
                         out_dtype=BF16)
            o_a = neighbourhood_attention(qkv, kvc, bias_table, batch=batch, n_lat=n_lat, n_ctx=n_ctx,
                                          kc_row0=0, kc_col0=0, vc_col0=NA_WIDTH)
        else:
            o_lat = neighbourhood_attention(qkv, qkv, bias_table, batch=batch, n_lat=n_lat, n_ctx=n_ctx,
                                            kc_row0=lat_rows, kc_col0=off_k, vc_col0=off_v)
            o_ctx = context_attention(qkv, batch=batch, n_ctx=n_ctx, row0=lat_rows)
            o_a = jnp.concatenate([o_lat, o_ctx], axis=0)
        o_b = spatial_gating(rest, gmlp_norm_g, gmlp_ws, gmlp_bs, layer, nrows=nrows, u_col0=0,
                             gv_col0=off_gv - off_u)
        y = gated_merge(o_a, o_b, w_branch_a, w_branch_b, rest, layer, nrows=nrows,
                        ga_col0=off_ga - off_u, gb_col0=off_gb - off_u)
        stream = outproj_residual(y, w_out, stream, mod3, layer, rows, 2, nrows=nrows)

        h2, logits_t = norm_mod_router(stream, norm2_g, layer, mod3, rows, 3, 4, router_w, nrows)
        stream = expert_choice_ffn_residual(stream, h2, logits_t, mod3, rows, 5, exp_w1, exp_w3, exp_w2, layer,
                                            batch=batch, n=n_lat, row0=0)
        if not last:
            stream = expert_choice_ffn_residual(stream, h2, logits_t, mod3, rows, 5, exp_w1, exp_w3, exp_w2, layer,
                                                batch=batch, n=n_ctx, row0=lat_rows)

    return final_norm(stream, final_norm_g, lat_rows).reshape(batch, n_lat, d)
```

```python
import functools

import jax
import jax.numpy as jnp
from jax import lax
from jax.experimental import pallas as pl
from jax.experimental.pallas import tpu as pltpu

F32, BF16, I32 = jnp.float32, jnp.bfloat16, jnp.int32

GRID_W = 64
NA_HEADS = 16
NA_HEAD_DIM = 128
NA_WIDTH = NA_HEADS * NA_HEAD_DIM
WIN_H = 8
WIN_W = 16
GMLP_GROUPS = 16
GMLP_WIDTH = 2048
CHUNK = 128
N_EXPERTS = 16
CAPACITY_FACTOR = 2
N_MOD = 6
EPS = 1e-6
MASK_BIAS = -1e30

V7X_VMEM_LIMIT_BYTES = 56 * 1024 * 1024
LANES = 128
MOD_ROWS = 8


def _params(*sem):
    return pltpu.CompilerParams(dimension_semantics=sem, vmem_limit_bytes=V7X_VMEM_LIMIT_BYTES)


def _cast_weight(w_ref, wb_ref):
    rows = w_ref.shape[0]
    step = min(rows, 512)

    def body(i, carry):
        r = pl.multiple_of(i * step, step)
        wb_ref[pl.ds(r, step), :] = w_ref[pl.ds(r, step), :].astype(BF16)
        return carry

    lax.fori_loop(0, rows // step, body, 0)


def _ada_kernel(c_ref, w_ref, b_ref, o_ref):
    c = c_ref[...]
    s = c * jax.nn.sigmoid(c)
    o_ref[...] = jnp.dot(s.astype(BF16), w_ref[...].astype(BF16), preferred_element_type=F32) + b_ref[...]


def ada_mod(cv, ada_w, ada_b, layer):
    _, d, m = ada_w.shape
    tn = min(m, 512)
    return pl.pallas_call(
        _ada_kernel,
        grid=(m // tn,),
        in_specs=[
            pl.BlockSpec((MOD_ROWS, d), lambda j: (0, 0)),
            pl.BlockSpec((None, d, tn), lambda j: (layer, 0, j)),
            pl.BlockSpec((None, 1, tn), lambda j: (layer, 0, j)),
        ],
        out_specs=pl.BlockSpec((MOD_ROWS, tn), lambda j: (0, j)),
        out_shape=jax.ShapeDtypeStruct((MOD_ROWS, m), F32),
        compiler_params=_params("arbitrary"),
        name="ada_mod",
    )(cv, ada_w, ada_b.reshape(ada_b.shape[0], 1, m))


class Rows:
    def __init__(self, batch, n_lat, n_ctx):
        self.batch, self.n_lat, self.n_ctx = batch, n_lat, n_ctx
        self.lat_rows = batch * n_lat
        self.all_rows = self.lat_rows + batch * n_ctx

    def mod_row(self, tile_rows):
        lat_tiles = self.lat_rows // tile_rows
        per_batch = self.n_lat // tile_rows
        batch = self.batch

        def f(i):
            return jnp.where(i < lat_tiles, i // per_batch, batch)

        return f


def _mod_spec(mod_row, k, d):
    return pl.BlockSpec((1, 1, d), lambda i: (mod_row(i), 0, k))


def _rms(x, g):
    return x * lax.rsqrt(jnp.mean(x * x, axis=-1, keepdims=True) + EPS) * g


def _norm_mod_kernel(x_ref, g_ref, sh_ref, sc_ref, o_ref):
    y = _rms(x_ref[...], g_ref[...])
    o_ref[...] = (y * (1.0 + sc_ref[0]) + sh_ref[0]).astype(o_ref.dtype)


def norm_mod(x, g, layer, mod3, rows, k_shift, k_scale, row0, nrows, out_dtype):
    d = x.shape[1]
    tr = 256
    mod_row = rows.mod_row(tr)
    t0 = row0 // tr
    return pl.pallas_call(
        _norm_mod_kernel,
        grid=(nrows // tr,),
        in_specs=[
            pl.BlockSpec((tr, d), lambda i: (i + t0, 0)),
            pl.BlockSpec((None, 1, d), lambda i: (layer, 0, 0)),
            pl.BlockSpec((1, 1, d), lambda i: (mod_row(i + t0), 0, k_shift)),
            pl.BlockSpec((1, 1, d), lambda i: (mod_row(i + t0), 0, k_scale)),
        ],
        out_specs=pl.BlockSpec((tr, d), lambda i: (i, 0)),
        out_shape=jax.ShapeDtypeStruct((nrows, d), out_dtype),
        compiler_params=_params("arbitrary"),
        name="norm_mod",
    )(x, g.reshape(g.shape[0], 1, d), mod3, mod3)


def _final_norm_kernel(x_ref, g_ref, o_ref):
    o_ref[...] = _rms(x_ref[...], g_ref[...])


def final_norm(x, g, nrows):
    d = x.shape[1]
    tr = 256
    return pl.pallas_call(
        _final_norm_kernel,
        grid=(nrows // tr,),
        in_specs=[pl.BlockSpec((tr, d), lambda i: (i, 0)), pl.BlockSpec((1, d), lambda i: (0, 0))],
        out_specs=pl.BlockSpec((tr, d), lambda i: (i, 0)),
        out_shape=jax.ShapeDtypeStruct((nrows, d), F32),
        compiler_params=_params("arbitrary"),
        name="final_norm",
    )(x, g.reshape(1, d))


WEIGHT_TILE_BUFFERS = pl.Buffered(1)

def _mm_kernel(a_ref, w_ref, o_ref, wb_ref):
    @pl.when(pl.program_id(1) == 0)
    def _():
        _cast_weight(w_ref, wb_ref)

    o_ref[...] = jnp.dot(a_ref[...], wb_ref[...], preferred_element_type=F32).astype(o_ref.dtype)


def matmul(a, w, layer, *, row0, nrows, col0, ncols, out_dtype, tm=512, tn=1024):
    k = a.shape[1]
    tm, tn = min(tm, nrows), min(tn, ncols)
    r0, c0 = row0 // tm, col0 // tn
    return pl.pallas_call(
        _mm_kernel,
        grid=(ncols // tn, nrows // tm),
        in_specs=[
            pl.BlockSpec((tm, k), lambda j, i: (i + r0, 0)),
            pl.BlockSpec((None, k, tn), lambda j, i: (layer, 0, j + c0), pipeline_mode=WEIGHT_TILE_BUFFERS),
        ],
        out_specs=pl.BlockSpec((tm, tn), lambda j, i: (i, j)),
        out_shape=jax.ShapeDtypeStruct((nrows, ncols), out_dtype),
        scratch_shapes=[pltpu.VMEM((k, tn), BF16)],
        compiler_params=_params("arbitrary", "arbitrary"),
        name="matmul",
    )(a, w)


def _merge_kernel(oa_ref, ob_ref, wa_ref, wb_ref, ga_ref, gb_ref, y_ref, wab_ref, wbb_ref):
    @pl.when(pl.program_id(1) == 0)
    def _():
        _cast_weight(wa_ref, wab_ref)
        _cast_weight(wb_ref, wbb_ref)

    ya = jnp.dot(oa_ref[...], wab_ref[...], preferred_element_type=F32)
    yb = jnp.dot(ob_ref[...], wbb_ref[...], preferred_element_type=F32)
    y = jax.nn.sigmoid(ga_ref[...]) * ya + jax.nn.sigmoid(gb_ref[...]) * yb
    y_ref[...] = y.astype(y_ref.dtype)


def gated_merge(o_a, o_b, w_a, w_b, gates, layer, *, nrows, ga_col0, gb_col0, tm=512, tn=1024):
    ka, kb = o_a.shape[1], o_b.shape[1]
    d = w_a.shape[2]
    tm, tn = min(tm, nrows), min(tn, d)
    ca, cb = ga_col0 // tn, gb_col0 // tn
    return pl.pallas_call(
        _merge_kernel,
        grid=(d // tn, nrows // tm),
        in_specs=[
            pl.BlockSpec((tm, ka), lambda j, i: (i, 0)),
            pl.BlockSpec((tm, kb), lambda j, i: (i, 0)),
            pl.BlockSpec((None, ka, tn), lambda j, i: (layer, 0, j), pipeline_mode=WEIGHT_TILE_BUFFERS),
            pl.BlockSpec((None, kb, tn), lambda j, i: (layer, 0, j), pipeline_mode=WEIGHT_TILE_BUFFERS),
            pl.BlockSpec((tm, tn), lambda j, i: (i, j + ca)),
            pl.BlockSpec((tm, tn), lambda j, i: (i, j + cb)),
        ],
        out_specs=pl.BlockSpec((tm, tn), lambda j, i: (i, j)),
        out_shape=jax.ShapeDtypeStruct((nrows, d), BF16),
        scratch_shapes=[pltpu.VMEM((ka, tn), BF16), pltpu.VMEM((kb, tn), BF16)],
        compiler_params=_params("arbitrary", "arbitrary"),
        name="gated_merge",
    )(o_a, o_b, w_a, w_b, gates, gates)


def _outproj_kernel(y_ref, w_ref, x_ref, gt_ref, o_ref, wb_ref):
    @pl.when(pl.program_id(1) == 0)
    def _():
        _cast_weight(w_ref, wb_ref)

    o_ref[...] = x_ref[...] + gt_ref[0] * jnp.dot(y_ref[...], wb_ref[...], preferred_element_type=F32)


def outproj_residual(y, w_out, x, mod3, layer, rows, k_gate, *, nrows, tm=512, tn=1024):
    k = y.shape[1]
    d = w_out.shape[2]
    tm, tn = min(tm, nrows), min(tn, d)
    mod_row = rows.mod_row(tm)
    gate_col0 = k_gate * (d // tn)
    return pl.pallas_call(
        _outproj_kernel,
        grid=(d // tn, nrows // tm),
        in_specs=[
            pl.BlockSpec((tm, k), lambda j, i: (i, 0)),
            pl.BlockSpec((None, k, tn), lambda j, i: (layer, 0, j), pipeline_mode=WEIGHT_TILE_BUFFERS),
            pl.BlockSpec((tm, tn), lambda j, i: (i, j)),
            pl.BlockSpec((1, 1, tn), lambda j, i: (mod_row(i), 0, j + gate_col0)),
        ],
        out_specs=pl.BlockSpec((tm, tn), lambda j, i: (i, j)),
        out_shape=jax.ShapeDtypeStruct((nrows, d), F32),
        scratch_shapes=[pltpu.VMEM((k, tn), BF16)],
        compiler_params=_params("arbitrary", "arbitrary"),
        name="outproj_residual",
    )(y, w_out, x, mod3)


Q_ROWS = 4
BAND_ROWS = WIN_H + Q_ROWS
N_PATTERNS = 3


def build_col_table(rpb):
    cols = jnp.arange(GRID_W)
    col_start = jnp.clip(cols - WIN_W // 2, 0, GRID_W - WIN_W)
    col_ok = (cols[None, :] >= col_start[:, None]) & (cols[None, :] < col_start[:, None] + WIN_W)
    col_off = jnp.clip(cols[None, :] - cols[:, None] + (WIN_W - 1), 0, 2 * WIN_W - 2)
    return jnp.where(col_ok[None, None], rpb[:, :, col_off], MASK_BIAS).astype(F32)


def _band_patterns(grid_rows):
    half = WIN_H // 2
    groups = grid_rows // Q_ROWS
    assert grid_rows % Q_ROWS == 0 and grid_rows >= BAND_ROWS
    patterns = []
    for g in (0, min(1, groups - 1), groups - 1):
        band_start = min(max(g * Q_ROWS - half, 0), grid_rows - BAND_ROWS)
        pattern = []
        for a in range(Q_ROWS):
            q_row = g * Q_ROWS + a
            win_start = min(max(q_row - half, 0), grid_rows - WIN_H)
            pattern.append([band_start + i - q_row + WIN_H - 1 if win_start <= band_start + i < win_start + WIN_H
                            else None for i in range(BAND_ROWS)])
        patterns.append(pattern)
    return patterns


def _softmax_pv(parts):
    m = functools.reduce(jnp.maximum, [jnp.max(s, axis=-1, keepdims=True) for s, _ in parts])
    es = [jnp.exp(s - m) for s, _ in parts]
    denom = functools.reduce(jnp.add, [jnp.sum(e, axis=-1, keepdims=True) for e in es])
    acc = functools.reduce(jnp.add, [jnp.dot(e.astype(BF16), v, preferred_element_type=F32)
                                     for e, (_, v) in zip(es, parts)])
    return acc / denom


def _qk(q, k):
    return lax.dot_general(q, k, (((1,), (1,)), ((), ())), preferred_element_type=F32) * (NA_HEAD_DIM ** -0.5)


def _natten_kernel(q_ref, k_ref, v_ref, kc_ref, vc_ref, ct_ref, o_ref, bt_ref, *, grid_rows):
    groups = grid_rows // Q_ROWS
    nq, band = Q_ROWS * GRID_W, BAND_ROWS * GRID_W
    kc = kc_ref[...]
    vc = vc_ref[...]

    @pl.when(jnp.logical_and(pl.program_id(0) == 0, pl.program_id(1) == 0))
    def _():
        bt_ref[...] = jnp.full(bt_ref.shape, MASK_BIAS, F32)

    for p, pattern in enumerate(_band_patterns(grid_rows)):
        for a, row in enumerate(pattern):
            for i, off in enumerate(row):
                if off is not None:
                    bt_ref[p, a * GRID_W:(a + 1) * GRID_W, i * GRID_W:(i + 1) * GRID_W] = ct_ref[off]

    def body(g, carry):
        band_start = jnp.clip(g * Q_ROWS - WIN_H // 2, 0, grid_rows - BAND_ROWS)
        pattern = jnp.where(g == 0, 0, jnp.where(g == groups - 1, 2, 1))
        q0 = pl.multiple_of(g * nq, nq)
        k0 = pl.multiple_of(band_start * GRID_W, GRID_W)
        q = q_ref[pl.ds(q0, nq), :]
        kb = k_ref[pl.ds(k0, band), :]
        vb = v_ref[pl.ds(k0, band), :]
        s_loc = _qk(q, kb) + bt_ref[pattern]
        s_ctx = _qk(q, kc)
        o_ref[pl.ds(q0, nq), :] = _softmax_pv([(s_loc, vb), (s_ctx, vc)]).astype(o_ref.dtype)
        return carry

    lax.fori_loop(0, groups, body, 0)


def neighbourhood_attention(qkv, kvc, col_table, *, batch, n_lat, n_ctx, kc_row0, kc_col0, vc_col0):
    dh, h = NA_HEAD_DIM, NA_HEADS
    grid_rows = n_lat // GRID_W
    rb0, ck, cv = kc_row0 // n_ctx, kc_col0 // dh, vc_col0 // dh
    return pl.pallas_call(
        functools.partial(_natten_kernel, grid_rows=grid_rows),
        grid=(batch, h),
        in_specs=[
            pl.BlockSpec((n_lat, dh), lambda b, hh: (b, hh)),
            pl.BlockSpec((n_lat, dh), lambda b, hh: (b, h + hh)),
            pl.BlockSpec((n_lat, dh), lambda b, hh: (b, 2 * h + hh)),
            pl.BlockSpec((n_ctx, dh), lambda b, hh: (rb0 + b, ck + hh)),
            pl.BlockSpec((n_ctx, dh), lambda b, hh: (rb0 + b, cv + hh)),
            pl.BlockSpec((None, 2 * WIN_H - 1, GRID_W, GRID_W), lambda b, hh: (hh, 0, 0, 0)),
        ],
        out_specs=pl.BlockSpec((n_lat, dh), lambda b, hh: (b, hh)),
        out_shape=jax.ShapeDtypeStruct((batch * n_lat, NA_WIDTH), BF16),
        scratch_shapes=[pltpu.VMEM((N_PATTERNS, Q_ROWS * GRID_W, BAND_ROWS * GRID_W), F32)],
        compiler_params=_params("arbitrary", "arbitrary"),
        name="neighbourhood_attention",
    )(qkv, qkv, qkv, kvc, kvc, col_table)


def _ctx_attn_kernel(q_ref, k_ref, v_ref, o_ref):
    o_ref[...] = _softmax_pv([(_qk(q_ref[...], k_ref[...]), v_ref[...])]).astype(o_ref.dtype)


def context_attention(qkv, *, batch, n_ctx, row0):
    dh, h = NA_HEAD_DIM, NA_HEADS
    rb0 = row0 // n_ctx
    return pl.pallas_call(
        _ctx_attn_kernel,
        grid=(batch, h),
        in_specs=[
            pl.BlockSpec((n_ctx, dh), lambda b, hh: (rb0 + b, hh)),
            pl.BlockSpec((n_ctx, dh), lambda b, hh: (rb0 + b, h + hh)),
            pl.BlockSpec((n_ctx, dh), lambda b, hh: (rb0 + b, 2 * h + hh)),
        ],
        out_specs=pl.BlockSpec((n_ctx, dh), lambda b, hh: (b, hh)),
        out_shape=jax.ShapeDtypeStruct((batch * n_ctx, NA_WIDTH), BF16),
        compiler_params=_params("arbitrary", "arbitrary"),
        name="context_attention",
    )(qkv, qkv, qkv)


def _gmlp_kernel(u_ref, gv_ref, ng_ref, ws_ref, bst_ref, o_ref):
    v = jax.nn.gelu(gv_ref[...])
    vc = v - jnp.mean(v, axis=-1, keepdims=True)
    y = vc * lax.rsqrt(jnp.mean(vc * vc, axis=-1, keepdims=True) + EPS) * ng_ref[...]
    yb = y.astype(BF16)
    gd = GMLP_WIDTH // GMLP_GROUPS
    for g in range(GMLP_GROUPS):
        sl = slice(g * gd, (g + 1) * gd)
        mixed = jnp.dot(ws_ref[g].astype(BF16), yb[:, sl], preferred_element_type=F32) + bst_ref[:, g:g + 1]
        o_ref[:, sl] = (jax.nn.gelu(u_ref[:, sl]) * mixed).astype(o_ref.dtype)


def spatial_gating(p, norm_g, ws, bs, layer, *, nrows, u_col0, gv_col0):
    w = GMLP_WIDTH
    bst = jnp.transpose(bs[layer])
    return pl.pallas_call(
        _gmlp_kernel,
        grid=(nrows // CHUNK,),
        in_specs=[
            pl.BlockSpec((CHUNK, w), lambda i: (i, u_col0 // w)),
            pl.BlockSpec((CHUNK, w), lambda i: (i, gv_col0 // w)),
            pl.BlockSpec((None, 1, w), lambda i: (layer, 0, 0)),
            pl.BlockSpec((None, GMLP_GROUPS, CHUNK, CHUNK), lambda i: (layer, 0, 0, 0)),
            pl.BlockSpec((CHUNK, GMLP_GROUPS), lambda i: (0, 0)),
        ],
        out_specs=pl.BlockSpec((CHUNK, w), lambda i: (i, 0)),
        out_shape=jax.ShapeDtypeStruct((nrows, w), BF16),
        compiler_params=_params("arbitrary"),
        name="spatial_gating",
    )(p, p, norm_g.reshape(norm_g.shape[0], 1, w), ws, bst)


def _split_bf16(x):
    hi = x.astype(BF16)
    return hi, (x - hi.astype(F32)).astype(BF16)


def _norm_router_kernel(x_ref, g_ref, sh_ref, sc_ref, rw_ref, h_ref, lg_ref):
    h = _rms(x_ref[...], g_ref[...]) * (1.0 + sc_ref[0]) + sh_ref[0]
    h_ref[...] = h
    hh, hl = _split_bf16(h)
    wh, wl = _split_bf16(rw_ref[...])
    lg = (jnp.dot(hh, wh, preferred_element_type=F32) + jnp.dot(hl, wh, preferred_element_type=F32)
          + jnp.dot(hh, wl, preferred_element_type=F32))
    lg_ref[...] = jnp.transpose(lg)[:N_EXPERTS, :]


def norm_mod_router(x, g, layer, mod3, rows, k_shift, k_scale, router_w, nrows):
    d = x.shape[1]
    tr = 256
    mod_row = rows.mod_row(tr)
    rw = jnp.pad(router_w[layer], ((0, 0), (0, LANES - N_EXPERTS)))
    return pl.pallas_call(
        _norm_router_kernel,
        grid=(nrows // tr,),
        in_specs=[
            pl.BlockSpec((tr, d), lambda i: (i, 0)),
            pl.BlockSpec((None, 1, d), lambda i: (layer, 0, 0)),
            pl.BlockSpec((1, 1, d), lambda i: (mod_row(i), 0, k_shift)),
            pl.BlockSpec((1, 1, d), lambda i: (mod_row(i), 0, k_scale)),
            pl.BlockSpec((d, LANES), lambda i: (0, 0)),
        ],
        out_specs=[pl.BlockSpec((tr, d), lambda i: (i, 0)), pl.BlockSpec((N_EXPERTS, tr), lambda i: (0, i))],
        out_shape=[jax.ShapeDtypeStruct((nrows, d), F32), jax.ShapeDtypeStruct((N_EXPERTS, nrows), F32)],
        compiler_params=_params("arbitrary"),
        name="norm_mod_router",
    )(x, g.reshape(g.shape[0], 1, d), mod3, mod3, rw)


def _cumsum_lanes(x):
    r, n = x.shape
    blk = min(n, LANES)
    tri = (lax.broadcasted_iota(I32, (blk, blk), 0) <= lax.broadcasted_iota(I32, (blk, blk), 1)).astype(BF16)
    off = jnp.zeros((r, 1), F32)
    out = []
    for j in range(n // blk):
        cs = jnp.dot(x[:, j * blk:(j + 1) * blk].astype(BF16), tri, preferred_element_type=F32) + off
        out.append(cs)
        off = cs[:, blk - 1:blk]
    return jnp.concatenate(out, axis=1) if len(out) > 1 else out[0]


def _route_kernel(lg_ref, idx_ref, dest_ref, gate_ref, start_ref, cnt_ref, slot_s, aff_s, *, cap):
    e_n, n = lg_ref.shape
    lg = lg_ref[...]
    ex = jnp.exp(lg - jnp.max(lg, axis=0, keepdims=True))
    aff = ex / jnp.sum(ex, axis=0, keepdims=True)
    bits = lax.bitcast_convert_type(aff, I32)

    thr = jnp.zeros((e_n, 1), I32)
    for bit in range(30, -1, -1):
        cand = thr | (1 << bit)
        n_ge = jnp.sum((bits >= cand).astype(F32), axis=1, keepdims=True)
        thr = jnp.where(n_ge >= cap, cand, thr)
    above = bits > thr
    tied = (bits == thr).astype(F32)
    need = cap - jnp.sum(above.astype(F32), axis=1, keepdims=True)
    tied_before = _cumsum_lanes(tied) - tied
    sel = jnp.where(above, 1.0, jnp.where(tied_before < need, tied, 0.0))

    slot = _cumsum_lanes(sel) - sel
    per_tok = jnp.sum(sel, axis=0, keepdims=True)
    per_tok8 = jnp.broadcast_to(per_tok, (8, n))
    start = (_cumsum_lanes(per_tok8) - per_tok8)[0:1]
    start_ref[...] = start.astype(I32)
    cnt_ref[...] = per_tok.astype(I32)
    slot_s[...] = jnp.where(sel > 0.0, slot, -1.0)
    aff_s[...] = aff

    tok = lax.broadcasted_iota(I32, (1, n), 1).astype(F32)
    cc = min(cap, LANES)

    def body(e, chosen_by_earlier):
        srow = slot_s[pl.ds(e, 1), :]
        arow = aff_s[pl.ds(e, 1), :]
        drow = start + chosen_by_earlier
        for c0 in range(0, cap, cc):
            want = (lax.broadcasted_iota(I32, (cc, 1), 0) + c0).astype(F32)
            hit = srow == want
            pick = lambda row: jnp.sum(jnp.where(hit, row, 0.0), axis=1, keepdims=True)
            idx_ref[e, pl.ds(c0, cc), :] = pick(tok).astype(I32)
            dest_ref[e, pl.ds(c0, cc), :] = pick(drow).astype(I32)
            gate_ref[e, pl.ds(c0, cc), :] = pick(arow)
        return chosen_by_earlier + (srow >= 0.0).astype(F32)

    lax.fori_loop(0, e_n, body, jnp.zeros((1, n), F32))


def expert_choice_route(logits_t, *, batch, n, col0):
    e_n = logits_t.shape[0]
    cap = CAPACITY_FACTOR * n // e_n
    cb0 = col0 // n
    per_slot = lambda dt: jax.ShapeDtypeStruct((batch, e_n, cap, 1), dt)
    per_tok = jax.ShapeDtypeStruct((batch, 1, n), I32)
    slot_spec = pl.BlockSpec((None, e_n, cap, 1), lambda b: (b, 0, 0, 0))
    tok_spec = pl.BlockSpec((None, 1, n), lambda b: (b, 0, 0))
    return pl.pallas_call(
        functools.partial(_route_kernel, cap=cap),
        grid=(batch,),
        in_specs=[pl.BlockSpec((e_n, n), lambda b: (0, cb0 + b))],
        out_specs=[slot_spec, slot_spec, slot_spec, tok_spec, tok_spec],
        out_shape=[per_slot(I32), per_slot(I32), per_slot(F32), per_tok, per_tok],
        scratch_shapes=[pltpu.VMEM((e_n, n), F32), pltpu.VMEM((e_n, n), F32)],
        compiler_params=_params("arbitrary"),
        name="expert_choice_route",
    )(logits_t)


ROWS_PER_STEP = 256


def _row_copies(n, make_copy):
    def start(i, carry):
        make_copy(i).start()
        return carry

    def wait(i, carry):
        make_copy(i).wait()
        return carry

    lax.fori_loop(0, n, start, 0)
    lax.fori_loop(0, n, wait, 0)


def _gather_kernel(idx_ref, src_ref, o_ref, stage_ref, sem):
    rows = o_ref.shape[0]
    base = pl.program_id(0) * rows
    _row_copies(rows, lambda i: pltpu.make_async_copy(
        src_ref.at[pl.ds(idx_ref[base + i], 1), :], stage_ref.at[pl.ds(i, 1), :], sem))
    o_ref[...] = stage_ref[...].astype(o_ref.dtype)


def gather_rows(src, idx, out_dtype):
    n, d = idx.shape[0], src.shape[1]
    rows = min(n, ROWS_PER_STEP)
    return pl.pallas_call(
        _gather_kernel,
        grid_spec=pltpu.PrefetchScalarGridSpec(
            num_scalar_prefetch=1,
            grid=(n // rows,),
            in_specs=[pl.BlockSpec(memory_space=pl.ANY)],
            out_specs=pl.BlockSpec((rows, d), lambda i, idx_ref: (i, 0)),
            scratch_shapes=[pltpu.VMEM((rows, d), src.dtype), pltpu.SemaphoreType.DMA(())],
        ),
        out_shape=jax.ShapeDtypeStruct((n, d), out_dtype),
        compiler_params=_params("arbitrary"),
        name="gather_rows",
    )(idx, src)


def _scatter_kernel(idx_ref, x_ref, o_ref, sem):
    rows = x_ref.shape[0]
    base = pl.program_id(0) * rows
    _row_copies(rows, lambda i: pltpu.make_async_copy(
        x_ref.at[pl.ds(i, 1), :], o_ref.at[pl.ds(idx_ref[base + i], 1), :], sem))


def scatter_rows(x, idx):
    n, d = x.shape
    rows = min(n, ROWS_PER_STEP)
    return pl.pallas_call(
        _scatter_kernel,
        grid_spec=pltpu.PrefetchScalarGridSpec(
            num_scalar_prefetch=1,
            grid=(n // rows,),
            in_specs=[pl.BlockSpec((rows, d), lambda i, idx_ref: (i, 0))],
            out_specs=pl.BlockSpec(memory_space=pl.ANY),
            scratch_shapes=[pltpu.SemaphoreType.DMA(())],
        ),
        out_shape=jax.ShapeDtypeStruct((n, d), x.dtype),
        compiler_params=_params("arbitrary"),
        name="scatter_rows",
    )(idx, x)


def _ffn_kernel(xg_ref, w1_ref, w3_ref, w2_ref, g_ref, y_ref, hid_s, *, f1, tf):
    s = pl.program_id(1)

    @pl.when(s < f1)
    def _():
        xs = xg_ref[...]
        h1 = jnp.dot(xs, w1_ref[...].astype(BF16), preferred_element_type=F32)
        h3 = jnp.dot(xs, w3_ref[...].astype(BF16), preferred_element_type=F32)
        hid_s[s] = (jax.nn.silu(h1) * h3).astype(BF16)

    @pl.when(s >= f1)
    def _():
        acc = jnp.dot(hid_s[0], w2_ref[0:tf, :].astype(BF16), preferred_element_type=F32)
        for f in range(1, f1):
            acc += jnp.dot(hid_s[f], w2_ref[f * tf:(f + 1) * tf, :].astype(BF16), preferred_element_type=F32)
        y_ref[...] = acc * g_ref[...]


def expert_ffn(xg, w1, w3, w2, gate, layer):
    e_n, m, d = xg.shape
    ff = w1.shape[3]
    tf, tn = min(ff, 256), min(d, 512)
    f1, f2 = ff // tf, d // tn
    out_step = lambda s: jnp.maximum(s - f1, 0)
    return pl.pallas_call(
        functools.partial(_ffn_kernel, f1=f1, tf=tf),
        grid=(e_n, f1 + f2),
        in_specs=[
            pl.BlockSpec((None, m, d), lambda e, s: (e, 0, 0)),
            pl.BlockSpec((None, None, d, tf), lambda e, s: (layer, e, 0, jnp.minimum(s, f1 - 1))),
            pl.BlockSpec((None, None, d, tf), lambda e, s: (layer, e, 0, jnp.minimum(s, f1 - 1))),
            pl.BlockSpec((None, None, ff, tn), lambda e, s: (layer, e, 0, out_step(s))),
            pl.BlockSpec((None, m, 1), lambda e, s: (e, 0, 0)),
        ],
        out_specs=pl.BlockSpec((None, m, tn), lambda e, s: (e, 0, out_step(s))),
        out_shape=jax.ShapeDtypeStruct((e_n, m, d), F32),
        scratch_shapes=[pltpu.VMEM((f1, m, tf), BF16)],
        compiler_params=_params("arbitrary", "arbitrary"),
        name="expert_ffn",
    )(xg, w1, w3, w2, gate)


def _combine_kernel(cf_ref, cn_ref, yg_ref, st_ref, ct_ref, x_ref, gt_ref, o_ref, acc_s, *, tiles, ch, kmax):
    b, t, k = pl.program_id(0), pl.program_id(1), pl.program_id(2)
    tile = b * tiles + t

    @pl.when(k == 0)
    def _():
        acc_s[...] = jnp.zeros_like(acc_s)

    @pl.when(k < cn_ref[tile])
    def _():
        pair = (cf_ref[tile] + k) * ch + lax.broadcasted_iota(I32, (ch, 1), 0)
        rel = pair - st_ref[...]
        own_t = jnp.where(jnp.logical_and(rel >= 0, rel < ct_ref[...]), 1.0, 0.0).astype(BF16)
        acc_s[...] += lax.dot_general(own_t, yg_ref[...].astype(BF16), (((0,), (0,)), ((), ())),
                                      preferred_element_type=F32)

    @pl.when(k == kmax - 1)
    def _():
        o_ref[...] = x_ref[...] + gt_ref[0] * acc_s[...]


def combine_residual(yg, start, cnt, x, mod3, rows, k_gate, *, batch, n, row0, pairs, pair_row0):
    d = x.shape[1]
    tt = min(n, 256)
    ch = min(pairs, 256)
    tiles = n // tt
    kmax = tt * N_EXPERTS // ch + 1
    n_chunks = pairs // ch
    chunk0 = pair_row0 // ch
    first = start.reshape(batch, n)[:, ::tt]
    last = jnp.concatenate([first[:, 1:], jnp.full((batch, 1), pairs, I32)], axis=1)
    cfirst = jnp.minimum(first // ch, n_chunks - 1)
    cnum = jnp.where(last > first, (last - 1) // ch - first // ch + 1, 0)
    cfirst, cnum = cfirst.reshape(-1).astype(I32), cnum.reshape(-1).astype(I32)
    mod_row = rows.mod_row(tt)
    rt0 = row0 // tt
    gate_col = k_gate

    def yg_map(b, t, k, cf, cn):
        tile = b * tiles + t
        return (chunk0 + b * n_chunks + jnp.minimum(cf[tile] + k, cf[tile] + jnp.maximum(cn[tile] - 1, 0)), 0)

    row_map = lambda b, t, k, cf, cn: (rt0 + b * tiles + t, 0)
    tok_map = lambda b, t, k, cf, cn: (b, 0, t)
    return pl.pallas_call(
        functools.partial(_combine_kernel, tiles=tiles, ch=ch, kmax=kmax),
        grid_spec=pltpu.PrefetchScalarGridSpec(
            num_scalar_prefetch=2,
            grid=(batch, tiles, kmax),
            in_specs=[
                pl.BlockSpec((ch, d), yg_map),
                pl.BlockSpec((None, 1, tt), tok_map),
                pl.BlockSpec((None, 1, tt), tok_map),
                pl.BlockSpec((tt, d), row_map),
                pl.BlockSpec((1, 1, d), lambda b, t, k, cf, cn: (mod_row(rt0 + b * tiles + t), 0, gate_col)),
            ],
            out_specs=pl.BlockSpec((tt, d), row_map),
            scratch_shapes=[pltpu.VMEM((tt, d), F32)],
        ),
        out_shape=jax.ShapeDtypeStruct(x.shape, F32),
        input_output_aliases={5: 0},
        compiler_params=_params("arbitrary", "arbitrary", "arbitrary"),
        name="combine_residual",
    )(cfirst, cnum, yg, start, cnt, x, mod3)


def expert_choice_ffn_residual(x, h, logits_t, mod3, rows, k_gate, w1, w3, w2, layer, *, batch, sets):
    e_n = logits_t.shape[0]
    d = x.shape[1]
    boff = jnp.arange(batch, dtype=I32)[:, None, None]
    routed, src_rows, pair_rows, gate_cols, pair_row0 = [], [], [], [], 0
    for n, row0 in sets:
        idx, dest, gate, start, cnt = expert_choice_route(logits_t, batch=batch, n=n, col0=row0)
        cap = idx.shape[2]
        pairs = e_n * cap
        src_rows.append(jnp.transpose(idx[..., 0] + row0 + boff * n, (1, 0, 2)).reshape(e_n, batch * cap))
        pair_rows.append(jnp.transpose(dest[..., 0] + pair_row0 + boff * pairs, (1, 0, 2)).reshape(e_n, batch * cap))
        gate_cols.append(jnp.transpose(gate, (1, 0, 2, 3)).reshape(e_n, batch * cap, 1))
        routed.append((start, cnt, n, row0, pairs, pair_row0))
        pair_row0 += batch * pairs
    src_rows = jnp.concatenate(src_rows, axis=1)
    m = src_rows.shape[1]
    xg = gather_rows(h, src_rows.reshape(-1), BF16).reshape(e_n, m, d)
    y = expert_ffn(xg, w1, w3, w2, jnp.concatenate(gate_cols, axis=1), layer).reshape(e_n * m, d)
    yg = scatter_rows(y, jnp.concatenate(pair_rows, axis=1).reshape(-1))
    for start, cnt, n, row0, pairs, pair_row0 in routed:
        x = combine_residual(yg, start, cnt, x, mod3, rows, k_gate, batch=batch, n=n, row0=row0, pairs=pairs,
                             pair_row0=pair_row0)
    return x


def kernel(x, c, ctx, c_ctx, ada_w, ada_b, norm1_g, norm2_g, w_in, na_rpb, gmlp_norm_g, gmlp_ws, gmlp_bs,
           w_branch_a, w_branch_b, w_out, router_w, exp_w1, exp_w3, exp_w2, final_norm_g):
    batch, n_lat, d = x.shape
    n_ctx = ctx.shape[1]
    depth = ada_w.shape[0]
    rows = Rows(batch, n_lat, n_ctx)
    lat_rows, all_rows = rows.lat_rows, rows.all_rows
    off_k, off_v, off_u = NA_WIDTH, 2 * NA_WIDTH, 3 * NA_WIDTH
    off_gv = off_u + GMLP_WIDTH
    off_ga = off_gv + GMLP_WIDTH
    off_gb = off_ga + d
    in_cols = off_gb + d

    stream = jnp.concatenate([x.reshape(lat_rows, d), ctx.reshape(batch * n_ctx, d)], axis=0)
    cv = jnp.zeros((MOD_ROWS, d), F32).at[:batch].set(c).at[batch].set(c_ctx)

    for layer in range(depth):
        last = layer == depth - 1
        nrows = lat_rows if last else all_rows
        mod3 = ada_mod(cv, ada_w, ada_b, layer).reshape(MOD_ROWS, 1, N_MOD * d)

        h = norm_mod(stream, norm1_g, layer, mod3, rows, 0, 1, 0, nrows, BF16)
        qkv = matmul(h, w_in, layer, row0=0, nrows=nrows, col0=0, ncols=off_u, out_dtype=BF16)
        rest = matmul(h, w_in, layer, row0=0, nrows=nrows, col0=off_u, ncols=in_cols - off_u, out_dtype=F32)
        bias_table = build_col_table(na_rpb[layer])
        if last:
            h_ctx = norm_mod(stream, norm1_g, layer, mod3, rows, 0, 1, lat_rows, batch * n_ctx, BF16)
            kvc = matmul(h_ctx, w_in, layer, row0=0, nrows=batch * n_ctx, col0=off_k, ncols=2 * NA_WIDTH,
                         out_dtype=BF16)
            o_a = neighbourhood_attention(qkv, kvc, bias_table, batch=batch, n_lat=n_lat, n_ctx=n_ctx,
                                          kc_row0=0, kc_col0=0, vc_col0=NA_WIDTH)
        else:
            o_lat = neighbourhood_attention(qkv, qkv, bias_table, batch=batch, n_lat=n_lat, n_ctx=n_ctx,
                                            kc_row0=lat_rows, kc_col0=off_k, vc_col0=off_v)
            o_ctx = context_attention(qkv, batch=batch, n_ctx=n_ctx, row0=lat_rows)
            o_a = jnp.concatenate([o_lat, o_ctx], axis=0)
        o_b = spatial_gating(rest, gmlp_norm_g, gmlp_ws, gmlp_bs, layer, nrows=nrows, u_col0=0,
                             gv_col0=off_gv - off_u)
        y = gated_merge(o_a, o_b, w_branch_a, w_branch_b, rest, layer, nrows=nrows,
                        ga_col0=off_ga - off_u, gb_col0=off_gb - off_u)
        stream = outproj_residual(y, w_out, stream, mod3, layer, rows, 2, nrows=nrows)

        h2, logits_t = norm_mod_router(stream, norm2_g, layer, mod3, rows, 3, 4, router_w, nrows)
        sets = [(n_lat, 0)] if last else [(n_lat, 0), (n_ctx, lat_rows)]
        stream = expert_choice_ffn_residual(stream, h2, logits_t, mod3, rows, 5, exp_w1, exp_w3, exp_w2, layer,
                                            batch=batch, sets=sets)

    return final_norm(stream, final_norm_g, lat_rows).reshape(batch, n_lat, d)
```

```python
import functools

import jax
import jax.numpy as jnp
from jax import lax
from jax.experimental import pallas as pl
from jax.experimental.pallas import tpu as pltpu

F32, BF16, I32 = jnp.float32, jnp.bfloat16, jnp.int32

GRID_W = 64
NA_HEADS = 16
NA_HEAD_DIM = 128
NA_WIDTH = NA_HEADS * NA_HEAD_DIM
WIN_H = 8
WIN_W = 16
GMLP_GROUPS = 16
GMLP_WIDTH = 2048
CHUNK = 128
N_EXPERTS = 16
CAPACITY_FACTOR = 2
N_MOD = 6
EPS = 1e-6
MASK_BIAS = -1e30

V7X_VMEM_LIMIT_BYTES = 56 * 1024 * 1024
LANES = 128
MOD_ROWS = 8


def _params(*sem):
    return pltpu.CompilerParams(dimension_semantics=sem, vmem_limit_bytes=V7X_VMEM_LIMIT_BYTES)


def _cast_weight(w_ref, wb_ref):
    rows = w_ref.shape[0]
    step = min(rows, 512)

    def body(i, carry):
        r = pl.multiple_of(i * step, step)
        wb_ref[pl.ds(r, step), :] = w_ref[pl.ds(r, step), :].astype(BF16)
        return carry

    lax.fori_loop(0, rows // step, body, 0)


def _ada_kernel(c_ref, w_ref, b_ref, o_ref):
    c = c_ref[...]
    s = c * jax.nn.sigmoid(c)
    o_ref[...] = jnp.dot(s.astype(BF16), w_ref[...].astype(BF16), preferred_element_type=F32) + b_ref[...]


def ada_mod(cv, ada_w, ada_b, layer):
    _, d, m = ada_w.shape
    tn = min(m, 512)
    return pl.pallas_call(
        _ada_kernel,
        grid=(m // tn,),
        in_specs=[
            pl.BlockSpec((MOD_ROWS, d), lambda j: (0, 0)),
            pl.BlockSpec((None, d, tn), lambda j: (layer, 0, j)),
            pl.BlockSpec((None, 1, tn), lambda j: (layer, 0, j)),
        ],
        out_specs=pl.BlockSpec((MOD_ROWS, tn), lambda j: (0, j)),
        out_shape=jax.ShapeDtypeStruct((MOD_ROWS, m), F32),
        compiler_params=_params("arbitrary"),
        name="ada_mod",
    )(cv, ada_w, ada_b.reshape(ada_b.shape[0], 1, m))


class Rows:
    def __init__(self, batch, n_lat, n_ctx):
        self.batch, self.n_lat, self.n_ctx = batch, n_lat, n_ctx
        self.lat_rows = batch * n_lat
        self.all_rows = self.lat_rows + batch * n_ctx

    def mod_row(self, tile_rows):
        lat_tiles = self.lat_rows // tile_rows
        per_batch = self.n_lat // tile_rows
        batch = self.batch

        def f(i):
            return jnp.where(i < lat_tiles, i // per_batch, batch)

        return f


def _mod_spec(mod_row, k, d):
    return pl.BlockSpec((1, 1, d), lambda i: (mod_row(i), 0, k))


def _rms(x, g):
    return x * lax.rsqrt(jnp.mean(x * x, axis=-1, keepdims=True) + EPS) * g


def _norm_mod_kernel(x_ref, g_ref, sh_ref, sc_ref, o_ref):
    y = _rms(x_ref[...], g_ref[...])
    o_ref[...] = (y * (1.0 + sc_ref[0]) + sh_ref[0]).astype(o_ref.dtype)


def norm_mod(x, g, layer, mod3, rows, k_shift, k_scale, row0, nrows, out_dtype):
    d = x.shape[1]
    tr = 256
    mod_row = rows.mod_row(tr)
    t0 = row0 // tr
    return pl.pallas_call(
        _norm_mod_kernel,
        grid=(nrows // tr,),
        in_specs=[
            pl.BlockSpec((tr, d), lambda i: (i + t0, 0)),
            pl.BlockSpec((None, 1, d), lambda i: (layer, 0, 0)),
            pl.BlockSpec((1, 1, d), lambda i: (mod_row(i + t0), 0, k_shift)),
            pl.BlockSpec((1, 1, d), lambda i: (mod_row(i + t0), 0, k_scale)),
        ],
        out_specs=pl.BlockSpec((tr, d), lambda i: (i, 0)),
        out_shape=jax.ShapeDtypeStruct((nrows, d), out_dtype),
        compiler_params=_params("arbitrary"),
        name="norm_mod",
    )(x, g.reshape(g.shape[0], 1, d), mod3, mod3)


def _final_norm_kernel(x_ref, g_ref, o_ref):
    o_ref[...] = _rms(x_ref[...], g_ref[...])


def final_norm(x, g, nrows):
    d = x.shape[1]
    tr = 256
    return pl.pallas_call(
        _final_norm_kernel,
        grid=(nrows // tr,),
        in_specs=[pl.BlockSpec((tr, d), lambda i: (i, 0)), pl.BlockSpec((1, d), lambda i: (0, 0))],
        out_specs=pl.BlockSpec((tr, d), lambda i: (i, 0)),
        out_shape=jax.ShapeDtypeStruct((nrows, d), F32),
        compiler_params=_params("arbitrary"),
        name="final_norm",
    )(x, g.reshape(1, d))


WEIGHT_TILE_BUFFERS = pl.Buffered(1)

def _mm_kernel(a_ref, w_ref, o_ref, wb_ref):
    @pl.when(pl.program_id(1) == 0)
    def _():
        _cast_weight(w_ref, wb_ref)

    o_ref[...] = jnp.dot(a_ref[...], wb_ref[...], preferred_element_type=F32).astype(o_ref.dtype)


def matmul(a, w, layer, *, row0, nrows, col0, ncols, out_dtype, tm=512, tn=1024):
    k = a.shape[1]
    tm, tn = min(tm, nrows), min(tn, ncols)
    r0, c0 = row0 // tm, col0 // tn
    return pl.pallas_call(
        _mm_kernel,
        grid=(ncols // tn, nrows // tm),
        in_specs=[
            pl.BlockSpec((tm, k), lambda j, i: (i + r0, 0)),
            pl.BlockSpec((None, k, tn), lambda j, i: (layer, 0, j + c0), pipeline_mode=WEIGHT_TILE_BUFFERS),
        ],
        out_specs=pl.BlockSpec((tm, tn), lambda j, i: (i, j)),
        out_shape=jax.ShapeDtypeStruct((nrows, ncols), out_dtype),
        scratch_shapes=[pltpu.VMEM((k, tn), BF16)],
        compiler_params=_params("arbitrary", "arbitrary"),
        name="matmul",
    )(a, w)


def _merge_kernel(oa_ref, ob_ref, wa_ref, wb_ref, ga_ref, gb_ref, y_ref, wab_ref, wbb_ref):
    @pl.when(pl.program_id(1) == 0)
    def _():
        _cast_weight(wa_ref, wab_ref)
        _cast_weight(wb_ref, wbb_ref)

    ya = jnp.dot(oa_ref[...], wab_ref[...], preferred_element_type=F32)
    yb = jnp.dot(ob_ref[...], wbb_ref[...], preferred_element_type=F32)
    y = jax.nn.sigmoid(ga_ref[...]) * ya + jax.nn.sigmoid(gb_ref[...]) * yb
    y_ref[...] = y.astype(y_ref.dtype)


def gated_merge(o_a, o_b, w_a, w_b, gates, layer, *, nrows, ga_col0, gb_col0, tm=512, tn=1024):
    ka, kb = o_a.shape[1], o_b.shape[1]
    d = w_a.shape[2]
    tm, tn = min(tm, nrows), min(tn, d)
    ca, cb = ga_col0 // tn, gb_col0 // tn
    return pl.pallas_call(
        _merge_kernel,
        grid=(d // tn, nrows // tm),
        in_specs=[
            pl.BlockSpec((tm, ka), lambda j, i: (i, 0)),
            pl.BlockSpec((tm, kb), lambda j, i: (i, 0)),
            pl.BlockSpec((None, ka, tn), lambda j, i: (layer, 0, j), pipeline_mode=WEIGHT_TILE_BUFFERS),
            pl.BlockSpec((None, kb, tn), lambda j, i: (layer, 0, j), pipeline_mode=WEIGHT_TILE_BUFFERS),
            pl.BlockSpec((tm, tn), lambda j, i: (i, j + ca)),
            pl.BlockSpec((tm, tn), lambda j, i: (i, j + cb)),
        ],
        out_specs=pl.BlockSpec((tm, tn), lambda j, i: (i, j)),
        out_shape=jax.ShapeDtypeStruct((nrows, d), BF16),
        scratch_shapes=[pltpu.VMEM((ka, tn), BF16), pltpu.VMEM((kb, tn), BF16)],
        compiler_params=_params("arbitrary", "arbitrary"),
        name="gated_merge",
    )(o_a, o_b, w_a, w_b, gates, gates)


def _outproj_kernel(y_ref, w_ref, x_ref, gt_ref, o_ref, wb_ref):
    @pl.when(pl.program_id(1) == 0)
    def _():
        _cast_weight(w_ref, wb_ref)

    o_ref[...] = x_ref[...] + gt_ref[0] * jnp.dot(y_ref[...], wb_ref[...], preferred_element_type=F32)


def outproj_residual(y, w_out, x, mod3, layer, rows, k_gate, *, nrows, tm=512, tn=1024):
    k = y.shape[1]
    d = w_out.shape[2]
    tm, tn = min(tm, nrows), min(tn, d)
    mod_row = rows.mod_row(tm)
    gate_col0 = k_gate * (d // tn)
    return pl.pallas_call(
        _outproj_kernel,
        grid=(d // tn, nrows // tm),
        in_specs=[
            pl.BlockSpec((tm, k), lambda j, i: (i, 0)),
            pl.BlockSpec((None, k, tn), lambda j, i: (layer, 0, j), pipeline_mode=WEIGHT_TILE_BUFFERS),
            pl.BlockSpec((tm, tn), lambda j, i: (i, j)),
            pl.BlockSpec((1, 1, tn), lambda j, i: (mod_row(i), 0, j + gate_col0)),
        ],
        out_specs=pl.BlockSpec((tm, tn), lambda j, i: (i, j)),
        out_shape=jax.ShapeDtypeStruct((nrows, d), F32),
        scratch_shapes=[pltpu.VMEM((k, tn), BF16)],
        compiler_params=_params("arbitrary", "arbitrary"),
        name="outproj_residual",
    )(y, w_out, x, mod3)


Q_ROWS = 4
BAND_ROWS = WIN_H + Q_ROWS
N_PATTERNS = 3


def build_col_table(rpb):
    cols = jnp.arange(GRID_W)
    col_start = jnp.clip(cols - WIN_W // 2, 0, GRID_W - WIN_W)
    col_ok = (cols[None, :] >= col_start[:, None]) & (cols[None, :] < col_start[:, None] + WIN_W)
    col_off = jnp.clip(cols[None, :] - cols[:, None] + (WIN_W - 1), 0, 2 * WIN_W - 2)
    return jnp.where(col_ok[None, None], rpb[:, :, col_off], MASK_BIAS).astype(F32)


def _band_patterns(grid_rows):
    half = WIN_H // 2
    groups = grid_rows // Q_ROWS
    assert grid_rows % Q_ROWS == 0 and grid_rows >= BAND_ROWS
    patterns = []
    for g in (0, min(1, groups - 1), groups - 1):
        band_start = min(max(g * Q_ROWS - half, 0), grid_rows - BAND_ROWS)
        pattern = []
        for a in range(Q_ROWS):
            q_row = g * Q_ROWS + a
            win_start = min(max(q_row - half, 0), grid_rows - WIN_H)
            pattern.append([band_start + i - q_row + WIN_H - 1 if win_start <= band_start + i < win_start + WIN_H
                            else None for i in range(BAND_ROWS)])
        patterns.append(pattern)
    return patterns


def _softmax_pv(parts):
    m = functools.reduce(jnp.maximum, [jnp.max(s, axis=-1, keepdims=True) for s, _ in parts])
    es = [jnp.exp(s - m) for s, _ in parts]
    denom = functools.reduce(jnp.add, [jnp.sum(e, axis=-1, keepdims=True) for e in es])
    acc = functools.reduce(jnp.add, [jnp.dot(e.astype(BF16), v, preferred_element_type=F32)
                                     for e, (_, v) in zip(es, parts)])
    return acc / denom


def _qk(q, k):
    return lax.dot_general(q, k, (((1,), (1,)), ((), ())), preferred_element_type=F32) * (NA_HEAD_DIM ** -0.5)


def _natten_kernel(q_ref, k_ref, v_ref, kc_ref, vc_ref, ct_ref, o_ref, bt_ref, *, grid_rows):
    groups = grid_rows // Q_ROWS
    nq, band = Q_ROWS * GRID_W, BAND_ROWS * GRID_W
    kc = kc_ref[...]
    vc = vc_ref[...]

    @pl.when(jnp.logical_and(pl.program_id(0) == 0, pl.program_id(1) == 0))
    def _():
        bt_ref[...] = jnp.full(bt_ref.shape, MASK_BIAS, F32)

    for p, pattern in enumerate(_band_patterns(grid_rows)):
        for a, row in enumerate(pattern):
            for i, off in enumerate(row):
                if off is not None:
                    bt_ref[p, a * GRID_W:(a + 1) * GRID_W, i * GRID_W:(i + 1) * GRID_W] = ct_ref[off]

    def body(g, carry):
        band_start = jnp.clip(g * Q_ROWS - WIN_H // 2, 0, grid_rows - BAND_ROWS)
        pattern = jnp.where(g == 0, 0, jnp.where(g == groups - 1, 2, 1))
        q0 = pl.multiple_of(g * nq, nq)
        k0 = pl.multiple_of(band_start * GRID_W, GRID_W)
        q = q_ref[pl.ds(q0, nq), :]
        kb = k_ref[pl.ds(k0, band), :]
        vb = v_ref[pl.ds(k0, band), :]
        s_loc = _qk(q, kb) + bt_ref[pattern]
        s_ctx = _qk(q, kc)
        o_ref[pl.ds(q0, nq), :] = _softmax_pv([(s_loc, vb), (s_ctx, vc)]).astype(o_ref.dtype)
        return carry

    lax.fori_loop(0, groups, body, 0)


def neighbourhood_attention(qkv, kvc, col_table, *, batch, n_lat, n_ctx, kc_row0, kc_col0, vc_col0):
    dh, h = NA_HEAD_DIM, NA_HEADS
    grid_rows = n_lat // GRID_W
    rb0, ck, cv = kc_row0 // n_ctx, kc_col0 // dh, vc_col0 // dh
    return pl.pallas_call(
        functools.partial(_natten_kernel, grid_rows=grid_rows),
        grid=(batch, h),
        in_specs=[
            pl.BlockSpec((n_lat, dh), lambda b, hh: (b, hh)),
            pl.BlockSpec((n_lat, dh), lambda b, hh: (b, h + hh)),
            pl.BlockSpec((n_lat, dh), lambda b, hh: (b, 2 * h + hh)),
            pl.BlockSpec((n_ctx, dh), lambda b, hh: (rb0 + b, ck + hh)),
            pl.BlockSpec((n_ctx, dh), lambda b, hh: (rb0 + b, cv + hh)),
            pl.BlockSpec((None, 2 * WIN_H - 1, GRID_W, GRID_W), lambda b, hh: (hh, 0, 0, 0)),
        ],
        out_specs=pl.BlockSpec((n_lat, dh), lambda b, hh: (b, hh)),
        out_shape=jax.ShapeDtypeStruct((batch * n_lat, NA_WIDTH), BF16),
        scratch_shapes=[pltpu.VMEM((N_PATTERNS, Q_ROWS * GRID_W, BAND_ROWS * GRID_W), F32)],
        compiler_params=_params("arbitrary", "arbitrary"),
        name="neighbourhood_attention",
    )(qkv, qkv, qkv, kvc, kvc, col_table)


def _ctx_attn_kernel(q_ref, k_ref, v_ref, o_ref):
    o_ref[...] = _softmax_pv([(_qk(q_ref[...], k_ref[...]), v_ref[...])]).astype(o_ref.dtype)


def context_attention(qkv, *, batch, n_ctx, row0):
    dh, h = NA_HEAD_DIM, NA_HEADS
    rb0 = row0 // n_ctx
    return pl.pallas_call(
        _ctx_attn_kernel,
        grid=(batch, h),
        in_specs=[
            pl.BlockSpec((n_ctx, dh), lambda b, hh: (rb0 + b, hh)),
            pl.BlockSpec((n_ctx, dh), lambda b, hh: (rb0 + b, h + hh)),
            pl.BlockSpec((n_ctx, dh), lambda b, hh: (rb0 + b, 2 * h + hh)),
        ],
        out_specs=pl.BlockSpec((n_ctx, dh), lambda b, hh: (b, hh)),
        out_shape=jax.ShapeDtypeStruct((batch * n_ctx, NA_WIDTH), BF16),
        compiler_params=_params("arbitrary", "arbitrary"),
        name="context_attention",
    )(qkv, qkv, qkv)


def _gmlp_kernel(u_ref, gv_ref, ng_ref, ws_ref, bst_ref, o_ref):
    v = jax.nn.gelu(gv_ref[...])
    vc = v - jnp.mean(v, axis=-1, keepdims=True)
    y = vc * lax.rsqrt(jnp.mean(vc * vc, axis=-1, keepdims=True) + EPS) * ng_ref[...]
    yb = y.astype(BF16)
    gd = GMLP_WIDTH // GMLP_GROUPS
    for g in range(GMLP_GROUPS):
        sl = slice(g * gd, (g + 1) * gd)
        mixed = jnp.dot(ws_ref[g].astype(BF16), yb[:, sl], preferred_element_type=F32) + bst_ref[:, g:g + 1]
        o_ref[:, sl] = (jax.nn.gelu(u_ref[:, sl]) * mixed).astype(o_ref.dtype)


def spatial_gating(p, norm_g, ws, bs, layer, *, nrows, u_col0, gv_col0):
    w = GMLP_WIDTH
    bst = jnp.transpose(bs[layer])
    return pl.pallas_call(
        _gmlp_kernel,
        grid=(nrows // CHUNK,),
        in_specs=[
            pl.BlockSpec((CHUNK, w), lambda i: (i, u_col0 // w)),
            pl.BlockSpec((CHUNK, w), lambda i: (i, gv_col0 // w)),
            pl.BlockSpec((None, 1, w), lambda i: (layer, 0, 0)),
            pl.BlockSpec((None, GMLP_GROUPS, CHUNK, CHUNK), lambda i: (layer, 0, 0, 0)),
            pl.BlockSpec((CHUNK, GMLP_GROUPS), lambda i: (0, 0)),
        ],
        out_specs=pl.BlockSpec((CHUNK, w), lambda i: (i, 0)),
        out_shape=jax.ShapeDtypeStruct((nrows, w), BF16),
        compiler_params=_params("arbitrary"),
        name="spatial_gating",
    )(p, p, norm_g.reshape(norm_g.shape[0], 1, w), ws, bst)


def _split_bf16(x):
    hi = x.astype(BF16)
    return hi, (x - hi.astype(F32)).astype(BF16)


def _norm_router_kernel(x_ref, g_ref, sh_ref, sc_ref, rw_ref, h_ref, lg_ref):
    h = _rms(x_ref[...], g_ref[...]) * (1.0 + sc_ref[0]) + sh_ref[0]
    h_ref[...] = h
    hh, hl = _split_bf16(h)
    wh, wl = _split_bf16(rw_ref[...])
    lg = (jnp.dot(hh, wh, preferred_element_type=F32) + jnp.dot(hl, wh, preferred_element_type=F32)
          + jnp.dot(hh, wl, preferred_element_type=F32))
    lg_ref[...] = jnp.transpose(lg)[:N_EXPERTS, :]


def norm_mod_router(x, g, layer, mod3, rows, k_shift, k_scale, router_w, nrows):
    d = x.shape[1]
    tr = 256
    mod_row = rows.mod_row(tr)
    rw = jnp.pad(router_w[layer], ((0, 0), (0, LANES - N_EXPERTS)))
    return pl.pallas_call(
        _norm_router_kernel,
        grid=(nrows // tr,),
        in_specs=[
            pl.BlockSpec((tr, d), lambda i: (i, 0)),
            pl.BlockSpec((None, 1, d), lambda i: (layer, 0, 0)),
            pl.BlockSpec((1, 1, d), lambda i: (mod_row(i), 0, k_shift)),
            pl.BlockSpec((1, 1, d), lambda i: (mod_row(i), 0, k_scale)),
            pl.BlockSpec((d, LANES), lambda i: (0, 0)),
        ],
        out_specs=[pl.BlockSpec((tr, d), lambda i: (i, 0)), pl.BlockSpec((N_EXPERTS, tr), lambda i: (0, i))],
        out_shape=[jax.ShapeDtypeStruct((nrows, d), F32), jax.ShapeDtypeStruct((N_EXPERTS, nrows), F32)],
        compiler_params=_params("arbitrary"),
        name="norm_mod_router",
    )(x, g.reshape(g.shape[0], 1, d), mod3, mod3, rw)


def _cumsum_lanes(x):
    r, n = x.shape
    blk = min(n, LANES)
    tri = (lax.broadcasted_iota(I32, (blk, blk), 0) <= lax.broadcasted_iota(I32, (blk, blk), 1)).astype(BF16)
    off = jnp.zeros((r, 1), F32)
    out = []
    for j in range(n // blk):
        cs = jnp.dot(x[:, j * blk:(j + 1) * blk].astype(BF16), tri, preferred_element_type=F32) + off
        out.append(cs)
        off = cs[:, blk - 1:blk]
    return jnp.concatenate(out, axis=1) if len(out) > 1 else out[0]


def _route_kernel(lg_ref, idx_ref, dest_ref, gate_ref, start_ref, cnt_ref, slot_s, aff_s, *, cap):
    e_n, n = lg_ref.shape
    lg = lg_ref[...]
    ex = jnp.exp(lg - jnp.max(lg, axis=0, keepdims=True))
    aff = ex / jnp.sum(ex, axis=0, keepdims=True)
    bits = lax.bitcast_convert_type(aff, I32)

    thr = jnp.zeros((e_n, 1), I32)
    for bit in range(30, -1, -1):
        cand = thr | (1 << bit)
        n_ge = jnp.sum((bits >= cand).astype(F32), axis=1, keepdims=True)
        thr = jnp.where(n_ge >= cap, cand, thr)
    above = bits > thr
    tied = (bits == thr).astype(F32)
    need = cap - jnp.sum(above.astype(F32), axis=1, keepdims=True)
    tied_before = _cumsum_lanes(tied) - tied
    sel = jnp.where(above, 1.0, jnp.where(tied_before < need, tied, 0.0))

    slot = _cumsum_lanes(sel) - sel
    per_tok = jnp.sum(sel, axis=0, keepdims=True)
    per_tok8 = jnp.broadcast_to(per_tok, (8, n))
    start = (_cumsum_lanes(per_tok8) - per_tok8)[0:1]
    start_ref[...] = start.astype(I32)
    cnt_ref[...] = per_tok.astype(I32)
    slot_s[...] = jnp.where(sel > 0.0, slot, -1.0)
    aff_s[...] = aff

    tok = lax.broadcasted_iota(I32, (1, n), 1).astype(F32)
    cc = min(cap, LANES)

    def body(e, chosen_by_earlier):
        srow = slot_s[pl.ds(e, 1), :]
        arow = aff_s[pl.ds(e, 1), :]
        drow = start + chosen_by_earlier
        for c0 in range(0, cap, cc):
            want = (lax.broadcasted_iota(I32, (cc, 1), 0) + c0).astype(F32)
            hit = srow == want
            pick = lambda row: jnp.sum(jnp.where(hit, row, 0.0), axis=1, keepdims=True)
            idx_ref[e, pl.ds(c0, cc), :] = pick(tok).astype(I32)
            dest_ref[e, pl.ds(c0, cc), :] = pick(drow).astype(I32)
            gate_ref[e, pl.ds(c0, cc), :] = pick(arow)
        return chosen_by_earlier + (srow >= 0.0).astype(F32)

    lax.fori_loop(0, e_n, body, jnp.zeros((1, n), F32))


def expert_choice_route(logits_t, *, batch, n, col0):
    e_n = logits_t.shape[0]
    cap = CAPACITY_FACTOR * n // e_n
    cb0 = col0 // n
    per_slot = lambda dt: jax.ShapeDtypeStruct((batch, e_n, cap, 1), dt)
    per_tok = jax.ShapeDtypeStruct((batch, 1, n), I32)
    slot_spec = pl.BlockSpec((None, e_n, cap, 1), lambda b: (b, 0, 0, 0))
    tok_spec = pl.BlockSpec((None, 1, n), lambda b: (b, 0, 0))
    return pl.pallas_call(
        functools.partial(_route_kernel, cap=cap),
        grid=(batch,),
        in_specs=[pl.BlockSpec((e_n, n), lambda b: (0, cb0 + b))],
        out_specs=[slot_spec, slot_spec, slot_spec, tok_spec, tok_spec],
        out_shape=[per_slot(I32), per_slot(I32), per_slot(F32), per_tok, per_tok],
        scratch_shapes=[pltpu.VMEM((e_n, n), F32), pltpu.VMEM((e_n, n), F32)],
        compiler_params=_params("arbitrary"),
        name="expert_choice_route",
    )(logits_t)


ROWS_PER_STEP = 256
ROW_COPY_UNROLL = 8


def _row_copies(n, make_copy):
    def start(i, carry):
        make_copy(i).start()
        return carry

    def wait(i, carry):
        make_copy(i).wait()
        return carry

    lax.fori_loop(0, n, start, 0, unroll=ROW_COPY_UNROLL)
    lax.fori_loop(0, n, wait, 0, unroll=ROW_COPY_UNROLL)


def _gather_kernel(idx_ref, src_ref, o_ref, stage_ref, sem):
    rows = o_ref.shape[0]
    base = pl.program_id(0) * rows
    _row_copies(rows, lambda i: pltpu.make_async_copy(
        src_ref.at[pl.ds(idx_ref[base + i], 1), :], stage_ref.at[pl.ds(i, 1), :], sem))
    o_ref[...] = stage_ref[...].astype(o_ref.dtype)


def gather_rows(src, idx, out_dtype):
    n, d = idx.shape[0], src.shape[1]
    rows = min(n, ROWS_PER_STEP)
    return pl.pallas_call(
        _gather_kernel,
        grid_spec=pltpu.PrefetchScalarGridSpec(
            num_scalar_prefetch=1,
            grid=(n // rows,),
            in_specs=[pl.BlockSpec(memory_space=pl.ANY)],
            out_specs=pl.BlockSpec((rows, d), lambda i, idx_ref: (i, 0)),
            scratch_shapes=[pltpu.VMEM((rows, d), src.dtype), pltpu.SemaphoreType.DMA(())],
        ),
        out_shape=jax.ShapeDtypeStruct((n, d), out_dtype),
        compiler_params=_params("arbitrary"),
        name="gather_rows",
    )(idx, src)


def _scatter_kernel(idx_ref, x_ref, o_ref, sem):
    rows = x_ref.shape[0]
    base = pl.program_id(0) * rows
    _row_copies(rows, lambda i: pltpu.make_async_copy(
        x_ref.at[pl.ds(i, 1), :], o_ref.at[pl.ds(idx_ref[base + i], 1), :], sem))


def scatter_rows(x, idx):
    n, d = x.shape
    rows = min(n, ROWS_PER_STEP)
    return pl.pallas_call(
        _scatter_kernel,
        grid_spec=pltpu.PrefetchScalarGridSpec(
            num_scalar_prefetch=1,
            grid=(n // rows,),
            in_specs=[pl.BlockSpec((rows, d), lambda i, idx_ref: (i, 0))],
            out_specs=pl.BlockSpec(memory_space=pl.ANY),
            scratch_shapes=[pltpu.SemaphoreType.DMA(())],
        ),
        out_shape=jax.ShapeDtypeStruct((n, d), x.dtype),
        compiler_params=_params("arbitrary"),
        name="scatter_rows",
    )(idx, x)


def _ffn_kernel(xg_ref, w1_ref, w3_ref, w2_ref, g_ref, y_ref, hid_s, *, f1, tf):
    s = pl.program_id(1)

    @pl.when(s < f1)
    def _():
        xs = xg_ref[...]
        h1 = jnp.dot(xs, w1_ref[...].astype(BF16), preferred_element_type=F32)
        h3 = jnp.dot(xs, w3_ref[...].astype(BF16), preferred_element_type=F32)
        hid_s[s] = (jax.nn.silu(h1) * h3).astype(BF16)

    @pl.when(s >= f1)
    def _():
        acc = jnp.dot(hid_s[0], w2_ref[0:tf, :].astype(BF16), preferred_element_type=F32)
        for f in range(1, f1):
            acc += jnp.dot(hid_s[f], w2_ref[f * tf:(f + 1) * tf, :].astype(BF16), preferred_element_type=F32)
        y_ref[...] = acc * g_ref[...]


def expert_ffn(xg, w1, w3, w2, gate, layer):
    e_n, m, d = xg.shape
    ff = w1.shape[3]
    tf, tn = min(ff, 256), min(d, 512)
    f1, f2 = ff // tf, d // tn
    out_step = lambda s: jnp.maximum(s - f1, 0)
    return pl.pallas_call(
        functools.partial(_ffn_kernel, f1=f1, tf=tf),
        grid=(e_n, f1 + f2),
        in_specs=[
            pl.BlockSpec((None, m, d), lambda e, s: (e, 0, 0)),
            pl.BlockSpec((None, None, d, tf), lambda e, s: (layer, e, 0, jnp.minimum(s, f1 - 1))),
            pl.BlockSpec((None, None, d, tf), lambda e, s: (layer, e, 0, jnp.minimum(s, f1 - 1))),
            pl.BlockSpec((None, None, ff, tn), lambda e, s: (layer, e, 0, out_step(s))),
            pl.BlockSpec((None, m, 1), lambda e, s: (e, 0, 0)),
        ],
        out_specs=pl.BlockSpec((None, m, tn), lambda e, s: (e, 0, out_step(s))),
        out_shape=jax.ShapeDtypeStruct((e_n, m, d), F32),
        scratch_shapes=[pltpu.VMEM((f1, m, tf), BF16)],
        compiler_params=_params("arbitrary", "arbitrary"),
        name="expert_ffn",
    )(xg, w1, w3, w2, gate)


ITEM_ADD, ITEM_FIRST, ITEM_LAST = 1, 2, 4


def _combine_kernel(tile_ref, chunk_ref, flag_ref, yg_ref, st_ref, ct_ref, x_ref, gt_ref, o_ref, acc_s, *, ch):
    del tile_ref
    item = pl.program_id(0) * pl.num_programs(1) + pl.program_id(1)
    flags = flag_ref[item]

    @pl.when((flags & ITEM_FIRST) != 0)
    def _():
        acc_s[...] = jnp.zeros_like(acc_s)

    @pl.when((flags & ITEM_ADD) != 0)
    def _():
        pair = chunk_ref[item] * ch + lax.broadcasted_iota(I32, (ch, 1), 0)
        rel = pair - st_ref[...]
        own_t = jnp.where(jnp.logical_and(rel >= 0, rel < ct_ref[...]), 1.0, 0.0).astype(BF16)
        acc_s[...] += lax.dot_general(own_t, yg_ref[...].astype(BF16), (((0,), (0,)), ((), ())),
                                      preferred_element_type=F32)

    @pl.when((flags & ITEM_LAST) != 0)
    def _():
        o_ref[...] = x_ref[...] + gt_ref[0] * acc_s[...]


def _combine_work_items(start, *, batch, n, tt, pairs, ch):
    tiles, n_chunks = n // tt, pairs // ch
    n_items = tiles + n_chunks
    first = start.reshape(batch, n)[:, ::tt]
    last = jnp.concatenate([first[:, 1:], jnp.full((batch, 1), pairs, I32)], axis=1)
    c_first = jnp.minimum(first // ch, n_chunks - 1)
    c_num = jnp.where(last > first, (last - 1) // ch - first // ch + 1, 0)
    t_items = jnp.maximum(c_num, 1)
    ends = jnp.cumsum(t_items, axis=1)
    k = jnp.arange(n_items, dtype=I32)[None, :]
    tile = jnp.minimum(jnp.sum((ends[:, None, :] <= k[:, :, None]).astype(I32), axis=-1), tiles - 1)
    at = lambda v: jnp.take_along_axis(v, tile, axis=1)
    j = k - at(ends - t_items)
    real = k < ends[:, -1:]
    chunk = at(c_first) + jnp.minimum(j, jnp.maximum(at(c_num) - 1, 0))
    flags = (ITEM_ADD * (real & (j < at(c_num))) + ITEM_FIRST * (real & (j == 0))
             + ITEM_LAST * (real & (j == at(t_items) - 1)))
    flat = lambda v: v.reshape(-1).astype(I32)
    return n_items, flat(tile), flat(chunk), flat(flags)


def combine_residual(yg, start, cnt, x, mod3, rows, k_gate, *, batch, n, row0, pairs, pair_row0):
    d = x.shape[1]
    tt = min(n, 256)
    ch = min(pairs, 256)
    tiles, n_chunks = n // tt, pairs // ch
    n_items, item_tile, item_chunk, item_flags = _combine_work_items(start, batch=batch, n=n, tt=tt, pairs=pairs,
                                                                     ch=ch)
    mod_row = rows.mod_row(tt)
    rt0, chunk0 = row0 // tt, pair_row0 // ch

    def tile_of(b, k, tile_ref):
        return tile_ref[b * n_items + k]

    yg_map = lambda b, k, tile_ref, chunk_ref, flag_ref: (chunk0 + b * n_chunks + chunk_ref[b * n_items + k], 0)
    row_map = lambda b, k, tile_ref, chunk_ref, flag_ref: (rt0 + b * tiles + tile_of(b, k, tile_ref), 0)
    tok_map = lambda b, k, tile_ref, chunk_ref, flag_ref: (b, 0, tile_of(b, k, tile_ref))
    gate_map = lambda b, k, tile_ref, chunk_ref, flag_ref: (
        mod_row(rt0 + b * tiles + tile_of(b, k, tile_ref)), 0, k_gate)
    return pl.pallas_call(
        functools.partial(_combine_kernel, ch=ch),
        grid_spec=pltpu.PrefetchScalarGridSpec(
            num_scalar_prefetch=3,
            grid=(batch, n_items),
            in_specs=[
                pl.BlockSpec((ch, d), yg_map),
                pl.BlockSpec((None, 1, tt), tok_map),
                pl.BlockSpec((None, 1, tt), tok_map),
                pl.BlockSpec((tt, d), row_map),
                pl.BlockSpec((1, 1, d), gate_map),
            ],
            out_specs=pl.BlockSpec((tt, d), row_map),
            scratch_shapes=[pltpu.VMEM((tt, d), F32)],
        ),
        out_shape=jax.ShapeDtypeStruct(x.shape, F32),
        input_output_aliases={6: 0},
        compiler_params=_params("arbitrary", "arbitrary"),
        name="combine_residual",
    )(item_tile, item_chunk, item_flags, yg, start, cnt, x, mod3)


def expert_choice_ffn_residual(x, h, logits_t, mod3, rows, k_gate, w1, w3, w2, layer, *, batch, sets):
    e_n = logits_t.shape[0]
    d = x.shape[1]
    boff = jnp.arange(batch, dtype=I32)[:, None, None]
    routed, src_rows, pair_rows, gate_cols, pair_row0 = [], [], [], [], 0
    for n, row0 in sets:
        idx, dest, gate, start, cnt = expert_choice_route(logits_t, batch=batch, n=n, col0=row0)
        cap = idx.shape[2]
        pairs = e_n * cap
        src_rows.append(jnp.transpose(idx[..., 0] + row0 + boff * n, (1, 0, 2)).reshape(e_n, batch * cap))
        pair_rows.append(jnp.transpose(dest[..., 0] + pair_row0 + boff * pairs, (1, 0, 2)).reshape(e_n, batch * cap))
        gate_cols.append(jnp.transpose(gate, (1, 0, 2, 3)).reshape(e_n, batch * cap, 1))
        routed.append((start, cnt, n, row0, pairs, pair_row0))
        pair_row0 += batch * pairs
    src_rows = jnp.concatenate(src_rows, axis=1)
    m = src_rows.shape[1]
    xg = gather_rows(h, src_rows.reshape(-1), BF16).reshape(e_n, m, d)
    y = expert_ffn(xg, w1, w3, w2, jnp.concatenate(gate_cols, axis=1), layer).reshape(e_n * m, d)
    yg = scatter_rows(y, jnp.concatenate(pair_rows, axis=1).reshape(-1))
    for start, cnt, n, row0, pairs, pair_row0 in routed:
        x = combine_residual(yg, start, cnt, x, mod3, rows, k_gate, batch=batch, n=n, row0=row0, pairs=pairs,
                             pair_row0=pair_row0)
    return x


def kernel(x, c, ctx, c_ctx, ada_w, ada_b, norm1_g, norm2_g, w_in, na_rpb, gmlp_norm_g, gmlp_ws, gmlp_bs,
           w_branch_a, w_branch_b, w_out, router_w, exp_w1, exp_w3, exp_w2, final_norm_g):
    batch, n_lat, d = x.shape
    n_ctx = ctx.shape[1]
    depth = ada_w.shape[0]
    rows = Rows(batch, n_lat, n_ctx)
    lat_rows, all_rows = rows.lat_rows, rows.all_rows
    off_k, off_v, off_u = NA_WIDTH, 2 * NA_WIDTH, 3 * NA_WIDTH
    off_gv = off_u + GMLP_WIDTH
    off_ga = off_gv + GMLP_WIDTH
    off_gb = off_ga + d
    in_cols = off_gb + d

    stream = jnp.concatenate([x.reshape(lat_rows, d), ctx.reshape(batch * n_ctx, d)], axis=0)
    cv = jnp.zeros((MOD_ROWS, d), F32).at[:batch].set(c).at[batch].set(c_ctx)

    for layer in range(depth):
        last = layer == depth - 1
        nrows = lat_rows if last else all_rows
        mod3 = ada_mod(cv, ada_w, ada_b, layer).reshape(MOD_ROWS, 1, N_MOD * d)

        h = norm_mod(stream, norm1_g, layer, mod3, rows, 0, 1, 0, nrows, BF16)
        qkv = matmul(h, w_in, layer, row0=0, nrows=nrows, col0=0, ncols=off_u, out_dtype=BF16)
        rest = matmul(h, w_in, layer, row0=0, nrows=nrows, col0=off_u, ncols=in_cols - off_u, out_dtype=F32)
        bias_table = build_col_table(na_rpb[layer])
        if last:
            h_ctx = norm_mod(stream, norm1_g, layer, mod3, rows, 0, 1, lat_rows, batch * n_ctx, BF16)
            kvc = matmul(h_ctx, w_in, layer, row0=0, nrows=batch * n_ctx, col0=off_k, ncols=2 * NA_WIDTH,
                         out_dtype=BF16)
            o_a = neighbourhood_attention(qkv, kvc, bias_table, batch=batch, n_lat=n_lat, n_ctx=n_ctx,
                                          kc_row0=0, kc_col0=0, vc_col0=NA_WIDTH)
        else:
            o_lat = neighbourhood_attention(qkv, qkv, bias_table, batch=batch, n_lat=n_lat, n_ctx=n_ctx,
                                            kc_row0=lat_rows, kc_col0=off_k, vc_col0=off_v)
            o_ctx = context_attention(qkv, batch=batch, n_ctx=n_ctx, row0=lat_rows)
            o_a = jnp.concatenate([o_lat, o_ctx], axis=0)
        o_b = spatial_gating(rest, gmlp_norm_g, gmlp_ws, gmlp_bs, layer, nrows=nrows, u_col0=0,
                             gv_col0=off_gv - off_u)
        y = gated_merge(o_a, o_b, w_branch_a, w_branch_b, rest, layer, nrows=nrows,
                        ga_col0=off_ga - off_u, gb_col0=off_gb - off_u)
        stream = outproj_residual(y, w_out, stream, mod3, layer, rows, 2, nrows=nrows)

        h2, logits_t = norm_mod_router(stream, norm2_g, layer, mod3, rows, 3, 4, router_w, nrows)
        sets = [(n_lat, 0)] if last else [(n_lat, 0), (n_ctx, lat_rows)]
        stream = expert_choice_ffn_residual(stream, h2, logits_t, mod3, rows, 5, exp_w1, exp_w3, exp_w2, layer,
                                            batch=batch, sets=sets)

    return final_norm(stream, final_norm_g, lat_rows).reshape(batch, n_lat, d)
```

```python
import functools

import jax
import jax.numpy as jnp
from jax import lax
from jax.experimental import pallas as pl
from jax.experimental.pallas import tpu as pltpu

F32, BF16, I32 = jnp.float32, jnp.bfloat16, jnp.int32

GRID_W = 64
NA_HEADS = 16
NA_HEAD_DIM = 128
NA_WIDTH = NA_HEADS * NA_HEAD_DIM
WIN_H = 8
WIN_W = 16
GMLP_GROUPS = 16
GMLP_WIDTH = 2048
CHUNK = 128
N_EXPERTS = 16
CAPACITY_FACTOR = 2
N_MOD = 6
EPS = 1e-6
MASK_BIAS = -1e30

V7X_VMEM_LIMIT_BYTES = 56 * 1024 * 1024
LANES = 128
MOD_ROWS = 8


def _params(*sem):
    return pltpu.CompilerParams(dimension_semantics=sem, vmem_limit_bytes=V7X_VMEM_LIMIT_BYTES)


def _cast_weight(w_ref, wb_ref):
    rows = w_ref.shape[0]
    step = min(rows, 512)

    def body(i, carry):
        r = pl.multiple_of(i * step, step)
        wb_ref[pl.ds(r, step), :] = w_ref[pl.ds(r, step), :].astype(BF16)
        return carry

    lax.fori_loop(0, rows // step, body, 0)


def _ada_kernel(c_ref, w_ref, b_ref, o_ref):
    c = c_ref[...]
    s = c * jax.nn.sigmoid(c)
    o_ref[...] = jnp.dot(s.astype(BF16), w_ref[...].astype(BF16), preferred_element_type=F32) + b_ref[...]


def ada_mod(cv, ada_w, ada_b, layer):
    _, d, m = ada_w.shape
    tn = min(m, 512)
    return pl.pallas_call(
        _ada_kernel,
        grid=(m // tn,),
        in_specs=[
            pl.BlockSpec((MOD_ROWS, d), lambda j: (0, 0)),
            pl.BlockSpec((None, d, tn), lambda j: (layer, 0, j)),
            pl.BlockSpec((None, 1, tn), lambda j: (layer, 0, j)),
        ],
        out_specs=pl.BlockSpec((MOD_ROWS, tn), lambda j: (0, j)),
        out_shape=jax.ShapeDtypeStruct((MOD_ROWS, m), F32),
        compiler_params=_params("arbitrary"),
        name="ada_mod",
    )(cv, ada_w, ada_b.reshape(ada_b.shape[0], 1, m))


class Rows:
    def __init__(self, batch, n_lat, n_ctx):
        self.batch, self.n_lat, self.n_ctx = batch, n_lat, n_ctx
        self.lat_rows = batch * n_lat
        self.all_rows = self.lat_rows + batch * n_ctx

    def mod_row(self, tile_rows):
        lat_tiles = self.lat_rows // tile_rows
        per_batch = self.n_lat // tile_rows
        batch = self.batch

        def f(i):
            return jnp.where(i < lat_tiles, i // per_batch, batch)

        return f


def _mod_spec(mod_row, k, d):
    return pl.BlockSpec((1, 1, d), lambda i: (mod_row(i), 0, k))


def _rms(x, g):
    return x * lax.rsqrt(jnp.mean(x * x, axis=-1, keepdims=True) + EPS) * g


def _two_source_specs(block, head_tiles, tile_of):
    def head(*ids):
        t, c = tile_of(*ids)
        return (jnp.minimum(t, head_tiles - 1), c)

    def tail(*ids):
        t, c = tile_of(*ids)
        return (jnp.maximum(t - head_tiles, 0), c)

    return pl.BlockSpec(block, head), pl.BlockSpec(block, tail)


def _norm_mod_kernel(xh_ref, xt_ref, g_ref, sh_ref, sc_ref, o_ref, *, head_tiles, t0):
    def emit(x_ref):
        y = _rms(x_ref[...], g_ref[...])
        o_ref[...] = (y * (1.0 + sc_ref[0]) + sh_ref[0]).astype(o_ref.dtype)

    in_head = pl.program_id(0) + t0 < head_tiles
    pl.when(in_head)(lambda: emit(xh_ref))
    pl.when(jnp.logical_not(in_head))(lambda: emit(xt_ref))


def norm_mod(x_head, x_tail, g, layer, mod3, rows, k_shift, k_scale, row0, nrows, out_dtype):
    d = x_head.shape[1]
    tr = 256
    mod_row = rows.mod_row(tr)
    t0 = row0 // tr
    head_tiles = x_head.shape[0] // tr
    head_spec, tail_spec = _two_source_specs((tr, d), head_tiles, lambda i: (i + t0, 0))
    return pl.pallas_call(
        functools.partial(_norm_mod_kernel, head_tiles=head_tiles, t0=t0),
        grid=(nrows // tr,),
        in_specs=[
            head_spec,
            tail_spec,
            pl.BlockSpec((None, 1, d), lambda i: (layer, 0, 0)),
            pl.BlockSpec((1, 1, d), lambda i: (mod_row(i + t0), 0, k_shift)),
            pl.BlockSpec((1, 1, d), lambda i: (mod_row(i + t0), 0, k_scale)),
        ],
        out_specs=pl.BlockSpec((tr, d), lambda i: (i, 0)),
        out_shape=jax.ShapeDtypeStruct((nrows, d), out_dtype),
        compiler_params=_params("arbitrary"),
        name="norm_mod",
    )(x_head, x_tail, g.reshape(g.shape[0], 1, d), mod3, mod3)


def _final_norm_kernel(x_ref, g_ref, o_ref):
    o_ref[...] = _rms(x_ref[...], g_ref[...])


def final_norm(x, g, nrows):
    d = x.shape[1]
    tr = 256
    return pl.pallas_call(
        _final_norm_kernel,
        grid=(nrows // tr,),
        in_specs=[pl.BlockSpec((tr, d), lambda i: (i, 0)), pl.BlockSpec((1, d), lambda i: (0, 0))],
        out_specs=pl.BlockSpec((tr, d), lambda i: (i, 0)),
        out_shape=jax.ShapeDtypeStruct((nrows, d), F32),
        compiler_params=_params("arbitrary"),
        name="final_norm",
    )(x, g.reshape(1, d))


WEIGHT_TILE_BUFFERS = pl.Buffered(1)

def _mm_kernel(a_ref, w_ref, o_ref, wb_ref, *, scaled_tiles, scale):
    @pl.when(pl.program_id(1) == 0)
    def _():
        _cast_weight(w_ref, wb_ref)

    acc = jnp.dot(a_ref[...], wb_ref[...], preferred_element_type=F32)
    if scaled_tiles:
        acc = acc * jnp.where(pl.program_id(0) < scaled_tiles, scale, 1.0)
    o_ref[...] = acc.astype(o_ref.dtype)


def matmul(a, w, layer, *, row0, nrows, col0, ncols, out_dtype, scaled_cols=0, scale=1.0, tm=512, tn=1024):
    k = a.shape[1]
    tm, tn = min(tm, nrows), min(tn, ncols)
    r0, c0 = row0 // tm, col0 // tn
    assert scaled_cols % tn == 0
    return pl.pallas_call(
        functools.partial(_mm_kernel, scaled_tiles=scaled_cols // tn, scale=scale),
        grid=(ncols // tn, nrows // tm),
        in_specs=[
            pl.BlockSpec((tm, k), lambda j, i: (i + r0, 0)),
            pl.BlockSpec((None, k, tn), lambda j, i: (layer, 0, j + c0), pipeline_mode=WEIGHT_TILE_BUFFERS),
        ],
        out_specs=pl.BlockSpec((tm, tn), lambda j, i: (i, j)),
        out_shape=jax.ShapeDtypeStruct((nrows, ncols), out_dtype),
        scratch_shapes=[pltpu.VMEM((k, tn), BF16)],
        compiler_params=_params("arbitrary", "arbitrary"),
        name="matmul",
    )(a, w)


def _merge_kernel(oa_ref, ob_ref, wa_ref, wb_ref, ga_ref, gb_ref, y_ref, wab_ref, wbb_ref):
    @pl.when(pl.program_id(1) == 0)
    def _():
        _cast_weight(wa_ref, wab_ref)
        _cast_weight(wb_ref, wbb_ref)

    ya = jnp.dot(oa_ref[...], wab_ref[...], preferred_element_type=F32)
    yb = jnp.dot(ob_ref[...], wbb_ref[...], preferred_element_type=F32)
    y = jax.nn.sigmoid(ga_ref[...]) * ya + jax.nn.sigmoid(gb_ref[...]) * yb
    y_ref[...] = y.astype(y_ref.dtype)


def gated_merge(o_a, o_b, w_a, w_b, gates, layer, *, nrows, ga_col0, gb_col0, tm=512, tn=1024):
    ka, kb = o_a.shape[1], o_b.shape[1]
    d = w_a.shape[2]
    tm, tn = min(tm, nrows), min(tn, d)
    ca, cb = ga_col0 // tn, gb_col0 // tn
    return pl.pallas_call(
        _merge_kernel,
        grid=(d // tn, nrows // tm),
        in_specs=[
            pl.BlockSpec((tm, ka), lambda j, i: (i, 0)),
            pl.BlockSpec((tm, kb), lambda j, i: (i, 0)),
            pl.BlockSpec((None, ka, tn), lambda j, i: (layer, 0, j), pipeline_mode=WEIGHT_TILE_BUFFERS),
            pl.BlockSpec((None, kb, tn), lambda j, i: (layer, 0, j), pipeline_mode=WEIGHT_TILE_BUFFERS),
            pl.BlockSpec((tm, tn), lambda j, i: (i, j + ca)),
            pl.BlockSpec((tm, tn), lambda j, i: (i, j + cb)),
        ],
        out_specs=pl.BlockSpec((tm, tn), lambda j, i: (i, j)),
        out_shape=jax.ShapeDtypeStruct((nrows, d), BF16),
        scratch_shapes=[pltpu.VMEM((ka, tn), BF16), pltpu.VMEM((kb, tn), BF16)],
        compiler_params=_params("arbitrary", "arbitrary"),
        name="gated_merge",
    )(o_a, o_b, w_a, w_b, gates, gates)


def _outproj_kernel(y_ref, w_ref, xh_ref, xt_ref, gt_ref, o_ref, wb_ref, *, head_tiles):
    @pl.when(pl.program_id(1) == 0)
    def _():
        _cast_weight(w_ref, wb_ref)

    upd = gt_ref[0] * jnp.dot(y_ref[...], wb_ref[...], preferred_element_type=F32)

    def emit(x_ref):
        o_ref[...] = x_ref[...] + upd

    in_head = pl.program_id(1) < head_tiles
    pl.when(in_head)(lambda: emit(xh_ref))
    pl.when(jnp.logical_not(in_head))(lambda: emit(xt_ref))


def outproj_residual(y, w_out, x_head, x_tail, mod3, layer, rows, k_gate, *, nrows, tm=512, tn=1024):
    k = y.shape[1]
    d = w_out.shape[2]
    tm, tn = min(tm, nrows), min(tn, d)
    mod_row = rows.mod_row(tm)
    gate_col0 = k_gate * (d // tn)
    head_tiles = x_head.shape[0] // tm
    head_spec, tail_spec = _two_source_specs((tm, tn), head_tiles, lambda j, i: (i, j))
    return pl.pallas_call(
        functools.partial(_outproj_kernel, head_tiles=head_tiles),
        grid=(d // tn, nrows // tm),
        in_specs=[
            pl.BlockSpec((tm, k), lambda j, i: (i, 0)),
            pl.BlockSpec((None, k, tn), lambda j, i: (layer, 0, j), pipeline_mode=WEIGHT_TILE_BUFFERS),
            head_spec,
            tail_spec,
            pl.BlockSpec((1, 1, tn), lambda j, i: (mod_row(i), 0, j + gate_col0)),
        ],
        out_specs=pl.BlockSpec((tm, tn), lambda j, i: (i, j)),
        out_shape=jax.ShapeDtypeStruct((nrows, d), F32),
        scratch_shapes=[pltpu.VMEM((k, tn), BF16)],
        compiler_params=_params("arbitrary", "arbitrary"),
        name="outproj_residual",
    )(y, w_out, x_head, x_tail, mod3)


Q_ROWS = 4
BAND_ROWS = WIN_H + Q_ROWS
N_PATTERNS = 3


def build_col_table(rpb):
    cols = jnp.arange(GRID_W)
    col_start = jnp.clip(cols - WIN_W // 2, 0, GRID_W - WIN_W)
    col_ok = (cols[None, :] >= col_start[:, None]) & (cols[None, :] < col_start[:, None] + WIN_W)
    col_off = jnp.clip(cols[None, :] - cols[:, None] + (WIN_W - 1), 0, 2 * WIN_W - 2)
    return jnp.where(col_ok[None, None], rpb[:, :, col_off], MASK_BIAS).astype(F32)


def _band_patterns(grid_rows):
    half = WIN_H // 2
    groups = grid_rows // Q_ROWS
    assert grid_rows % Q_ROWS == 0 and grid_rows >= BAND_ROWS
    patterns = []
    for g in (0, min(1, groups - 1), groups - 1):
        band_start = min(max(g * Q_ROWS - half, 0), grid_rows - BAND_ROWS)
        pattern = []
        for a in range(Q_ROWS):
            q_row = g * Q_ROWS + a
            win_start = min(max(q_row - half, 0), grid_rows - WIN_H)
            pattern.append([band_start + i - q_row + WIN_H - 1 if win_start <= band_start + i < win_start + WIN_H
                            else None for i in range(BAND_ROWS)])
        patterns.append(pattern)
    return patterns


def _softmax_pv(parts):
    m = functools.reduce(jnp.maximum, [jnp.max(s, axis=-1, keepdims=True) for s, _ in parts])
    es = [jnp.exp(s - m) for s, _ in parts]
    denom = functools.reduce(jnp.add, [jnp.sum(e, axis=-1, keepdims=True) for e in es])
    acc = functools.reduce(jnp.add, [jnp.dot(e.astype(BF16), v, preferred_element_type=F32)
                                     for e, (_, v) in zip(es, parts)])
    return acc / denom


def _qk(q, k):
    return lax.dot_general(q, k, (((1,), (1,)), ((), ())), preferred_element_type=F32)


def _natten_kernel(q_ref, k_ref, v_ref, kc_ref, vc_ref, ct_ref, o_ref, bt_ref, *, grid_rows):
    groups = grid_rows // Q_ROWS
    nq, band = Q_ROWS * GRID_W, BAND_ROWS * GRID_W
    kc = kc_ref[...]
    vc = vc_ref[...]

    @pl.when(jnp.logical_and(pl.program_id(0) == 0, pl.program_id(1) == 0))
    def _():
        bt_ref[...] = jnp.full(bt_ref.shape, MASK_BIAS, F32)

    for p, pattern in enumerate(_band_patterns(grid_rows)):
        for a, row in enumerate(pattern):
            for i, off in enumerate(row):
                if off is not None:
                    bt_ref[p, a * GRID_W:(a + 1) * GRID_W, i * GRID_W:(i + 1) * GRID_W] = ct_ref[off]

    def body(g, carry):
        band_start = jnp.clip(g * Q_ROWS - WIN_H // 2, 0, grid_rows - BAND_ROWS)
        pattern = jnp.where(g == 0, 0, jnp.where(g == groups - 1, 2, 1))
        q0 = pl.multiple_of(g * nq, nq)
        k0 = pl.multiple_of(band_start * GRID_W, GRID_W)
        q = q_ref[pl.ds(q0, nq), :]
        kb = k_ref[pl.ds(k0, band), :]
        vb = v_ref[pl.ds(k0, band), :]
        s_loc = _qk(q, kb) + bt_ref[pattern]
        s_ctx = _qk(q, kc)
        o_ref[pl.ds(q0, nq), :] = _softmax_pv([(s_loc, vb), (s_ctx, vc)]).astype(o_ref.dtype)
        return carry

    lax.fori_loop(0, groups, body, 0, unroll=2)


def neighbourhood_attention(qkv, kvc, col_table, *, batch, n_lat, n_ctx, kc_row0, kc_col0, vc_col0):
    dh, h = NA_HEAD_DIM, NA_HEADS
    grid_rows = n_lat // GRID_W
    rb0, ck, cv = kc_row0 // n_ctx, kc_col0 // dh, vc_col0 // dh
    return pl.pallas_call(
        functools.partial(_natten_kernel, grid_rows=grid_rows),
        grid=(batch, h),
        in_specs=[
            pl.BlockSpec((n_lat, dh), lambda b, hh: (b, hh)),
            pl.BlockSpec((n_lat, dh), lambda b, hh: (b, h + hh)),
            pl.BlockSpec((n_lat, dh), lambda b, hh: (b, 2 * h + hh)),
            pl.BlockSpec((n_ctx, dh), lambda b, hh: (rb0 + b, ck + hh)),
            pl.BlockSpec((n_ctx, dh), lambda b, hh: (rb0 + b, cv + hh)),
            pl.BlockSpec((None, 2 * WIN_H - 1, GRID_W, GRID_W), lambda b, hh: (hh, 0, 0, 0)),
        ],
        out_specs=pl.BlockSpec((n_lat, dh), lambda b, hh: (b, hh)),
        out_shape=jax.ShapeDtypeStruct((batch * n_lat, NA_WIDTH), BF16),
        scratch_shapes=[pltpu.VMEM((N_PATTERNS, Q_ROWS * GRID_W, BAND_ROWS * GRID_W), F32)],
        compiler_params=_params("arbitrary", "arbitrary"),
        name="neighbourhood_attention",
    )(qkv, qkv, qkv, kvc, kvc, col_table)


def _ctx_attn_kernel(q_ref, k_ref, v_ref, o_ref):
    o_ref[...] = _softmax_pv([(_qk(q_ref[...], k_ref[...]), v_ref[...])]).astype(o_ref.dtype)


def context_attention(qkv, *, batch, n_ctx, row0):
    dh, h = NA_HEAD_DIM, NA_HEADS
    rb0 = row0 // n_ctx
    return pl.pallas_call(
        _ctx_attn_kernel,
        grid=(batch, h),
        in_specs=[
            pl.BlockSpec((n_ctx, dh), lambda b, hh: (rb0 + b, hh)),
            pl.BlockSpec((n_ctx, dh), lambda b, hh: (rb0 + b, h + hh)),
            pl.BlockSpec((n_ctx, dh), lambda b, hh: (rb0 + b, 2 * h + hh)),
        ],
        out_specs=pl.BlockSpec((n_ctx, dh), lambda b, hh: (b, hh)),
        out_shape=jax.ShapeDtypeStruct((batch * n_ctx, NA_WIDTH), BF16),
        compiler_params=_params("arbitrary", "arbitrary"),
        name="context_attention",
    )(qkv, qkv, qkv)


def _gmlp_kernel(u_ref, gv_ref, ng_ref, ws_ref, bst_ref, o_ref):
    v = jax.nn.gelu(gv_ref[...])
    vc = v - jnp.mean(v, axis=-1, keepdims=True)
    y = vc * lax.rsqrt(jnp.mean(vc * vc, axis=-1, keepdims=True) + EPS) * ng_ref[...]
    yb = y.astype(BF16)
    gd = GMLP_WIDTH // GMLP_GROUPS
    for g in range(GMLP_GROUPS):
        sl = slice(g * gd, (g + 1) * gd)
        mixed = jnp.dot(ws_ref[g].astype(BF16), yb[:, sl], preferred_element_type=F32) + bst_ref[:, g:g + 1]
        o_ref[:, sl] = (jax.nn.gelu(u_ref[:, sl]) * mixed).astype(o_ref.dtype)


def spatial_gating(p, norm_g, ws, bs, layer, *, nrows, u_col0, gv_col0):
    w = GMLP_WIDTH
    bst = jnp.transpose(bs[layer])
    return pl.pallas_call(
        _gmlp_kernel,
        grid=(nrows // CHUNK,),
        in_specs=[
            pl.BlockSpec((CHUNK, w), lambda i: (i, u_col0 // w)),
            pl.BlockSpec((CHUNK, w), lambda i: (i, gv_col0 // w)),
            pl.BlockSpec((None, 1, w), lambda i: (layer, 0, 0)),
            pl.BlockSpec((None, GMLP_GROUPS, CHUNK, CHUNK), lambda i: (layer, 0, 0, 0)),
            pl.BlockSpec((CHUNK, GMLP_GROUPS), lambda i: (0, 0)),
        ],
        out_specs=pl.BlockSpec((CHUNK, w), lambda i: (i, 0)),
        out_shape=jax.ShapeDtypeStruct((nrows, w), BF16),
        compiler_params=_params("arbitrary"),
        name="spatial_gating",
    )(p, p, norm_g.reshape(norm_g.shape[0], 1, w), ws, bst)


def _split_bf16(x):
    hi = x.astype(BF16)
    return hi, (x - hi.astype(F32)).astype(BF16)


def _norm_router_kernel(x_ref, g_ref, sh_ref, sc_ref, rw_ref, h_ref, lg_ref):
    h = _rms(x_ref[...], g_ref[...]) * (1.0 + sc_ref[0]) + sh_ref[0]
    h_ref[...] = h
    hh, hl = _split_bf16(h)
    wh, wl = _split_bf16(rw_ref[...])
    lg = (jnp.dot(hh, wh, preferred_element_type=F32) + jnp.dot(hl, wh, preferred_element_type=F32)
          + jnp.dot(hh, wl, preferred_element_type=F32))
    lg_ref[...] = jnp.transpose(lg)[:N_EXPERTS, :]


def norm_mod_router(x, g, layer, mod3, rows, k_shift, k_scale, router_w, nrows):
    d = x.shape[1]
    tr = 256
    mod_row = rows.mod_row(tr)
    rw = jnp.pad(router_w[layer], ((0, 0), (0, LANES - N_EXPERTS)))
    return pl.pallas_call(
        _norm_router_kernel,
        grid=(nrows // tr,),
        in_specs=[
            pl.BlockSpec((tr, d), lambda i: (i, 0)),
            pl.BlockSpec((None, 1, d), lambda i: (layer, 0, 0)),
            pl.BlockSpec((1, 1, d), lambda i: (mod_row(i), 0, k_shift)),
            pl.BlockSpec((1, 1, d), lambda i: (mod_row(i), 0, k_scale)),
            pl.BlockSpec((d, LANES), lambda i: (0, 0)),
        ],
        out_specs=[pl.BlockSpec((tr, d), lambda i: (i, 0)), pl.BlockSpec((N_EXPERTS, tr), lambda i: (0, i))],
        out_shape=[jax.ShapeDtypeStruct((nrows, d), F32), jax.ShapeDtypeStruct((N_EXPERTS, nrows), F32)],
        compiler_params=_params("arbitrary"),
        name="norm_mod_router",
    )(x, g.reshape(g.shape[0], 1, d), mod3, mod3, rw)


def _cumsum_lanes(x):
    r, n = x.shape
    blk = min(n, LANES)
    tri = (lax.broadcasted_iota(I32, (blk, blk), 0) <= lax.broadcasted_iota(I32, (blk, blk), 1)).astype(BF16)
    off = jnp.zeros((r, 1), F32)
    out = []
    for j in range(n // blk):
        cs = jnp.dot(x[:, j * blk:(j + 1) * blk].astype(BF16), tri, preferred_element_type=F32) + off
        out.append(cs)
        off = cs[:, blk - 1:blk]
    return jnp.concatenate(out, axis=1) if len(out) > 1 else out[0]


def _route_kernel(lg_ref, idx_ref, dest_ref, gate_ref, start_ref, cnt_ref, slot_s, aff_s, *, cap):
    e_n, n = lg_ref.shape
    lg = lg_ref[...]
    ex = jnp.exp(lg - jnp.max(lg, axis=0, keepdims=True))
    aff = ex / jnp.sum(ex, axis=0, keepdims=True)
    bits = lax.bitcast_convert_type(aff, I32)

    thr = jnp.zeros((e_n, 1), I32)
    for bit in range(30, -1, -1):
        cand = thr | (1 << bit)
        n_ge = jnp.sum((bits >= cand).astype(F32), axis=1, keepdims=True)
        thr = jnp.where(n_ge >= cap, cand, thr)
    above = bits > thr
    tied = (bits == thr).astype(F32)
    need = cap - jnp.sum(above.astype(F32), axis=1, keepdims=True)
    tied_before = _cumsum_lanes(tied) - tied
    sel = jnp.where(above, 1.0, jnp.where(tied_before < need, tied, 0.0))

    slot = _cumsum_lanes(sel) - sel
    per_tok = jnp.sum(sel, axis=0, keepdims=True)
    per_tok8 = jnp.broadcast_to(per_tok, (8, n))
    start = (_cumsum_lanes(per_tok8) - per_tok8)[0:1]
    start_ref[...] = start.astype(I32)
    cnt_ref[...] = per_tok.astype(I32)
    slot_s[...] = jnp.where(sel > 0.0, slot, -1.0)
    aff_s[...] = aff

    tok = lax.broadcasted_iota(I32, (1, n), 1).astype(F32)
    cc = min(cap, LANES)

    def body(e, chosen_by_earlier):
        srow = slot_s[pl.ds(e, 1), :]
        arow = aff_s[pl.ds(e, 1), :]
        drow = start + chosen_by_earlier
        for c0 in range(0, cap, cc):
            want = (lax.broadcasted_iota(I32, (cc, 1), 0) + c0).astype(F32)
            hit = srow == want
            pick = lambda row: jnp.sum(jnp.where(hit, row, 0.0), axis=1, keepdims=True)
            idx_ref[e, pl.ds(c0, cc), :] = pick(tok).astype(I32)
            dest_ref[e, pl.ds(c0, cc), :] = pick(drow).astype(I32)
            gate_ref[e, pl.ds(c0, cc), :] = pick(arow)
        return chosen_by_earlier + (srow >= 0.0).astype(F32)

    lax.fori_loop(0, e_n, body, jnp.zeros((1, n), F32))


def expert_choice_route(logits_t, *, batch, n, col0):
    e_n = logits_t.shape[0]
    cap = CAPACITY_FACTOR * n // e_n
    cb0 = col0 // n
    per_slot = lambda dt: jax.ShapeDtypeStruct((batch, e_n, cap, 1), dt)
    per_tok = jax.ShapeDtypeStruct((batch, 1, n), I32)
    slot_spec = pl.BlockSpec((None, e_n, cap, 1), lambda b: (b, 0, 0, 0))
    tok_spec = pl.BlockSpec((None, 1, n), lambda b: (b, 0, 0))
    return pl.pallas_call(
        functools.partial(_route_kernel, cap=cap),
        grid=(batch,),
        in_specs=[pl.BlockSpec((e_n, n), lambda b: (0, cb0 + b))],
        out_specs=[slot_spec, slot_spec, slot_spec, tok_spec, tok_spec],
        out_shape=[per_slot(I32), per_slot(I32), per_slot(F32), per_tok, per_tok],
        scratch_shapes=[pltpu.VMEM((e_n, n), F32), pltpu.VMEM((e_n, n), F32)],
        compiler_params=_params("arbitrary"),
        name="expert_choice_route",
    )(logits_t)


ROWS_PER_STEP = 256
ROW_COPY_UNROLL = 8


def _row_copies(n, make_copy):
    def start(i, carry):
        make_copy(i).start()
        return carry

    def wait(i, carry):
        make_copy(i).wait()
        return carry

    lax.fori_loop(0, n, start, 0, unroll=ROW_COPY_UNROLL)
    lax.fori_loop(0, n, wait, 0, unroll=ROW_COPY_UNROLL)


def _gather_kernel(idx_ref, src_ref, o_ref, stage_ref, sem):
    rows = o_ref.shape[0]
    base = pl.program_id(0) * rows
    _row_copies(rows, lambda i: pltpu.make_async_copy(
        src_ref.at[pl.ds(idx_ref[base + i], 1), :], stage_ref.at[pl.ds(i, 1), :], sem))
    o_ref[...] = stage_ref[...].astype(o_ref.dtype)


def gather_rows(src, idx, out_dtype):
    n, d = idx.shape[0], src.shape[1]
    rows = min(n, ROWS_PER_STEP)
    return pl.pallas_call(
        _gather_kernel,
        grid_spec=pltpu.PrefetchScalarGridSpec(
            num_scalar_prefetch=1,
            grid=(n // rows,),
            in_specs=[pl.BlockSpec(memory_space=pl.ANY)],
            out_specs=pl.BlockSpec((rows, d), lambda i, idx_ref: (i, 0)),
            scratch_shapes=[pltpu.VMEM((rows, d), src.dtype), pltpu.SemaphoreType.DMA(())],
        ),
        out_shape=jax.ShapeDtypeStruct((n, d), out_dtype),
        compiler_params=_params("arbitrary"),
        name="gather_rows",
    )(idx, src)


def _scatter_kernel(idx_ref, x_ref, o_ref, sem):
    rows = x_ref.shape[0]
    base = pl.program_id(0) * rows
    _row_copies(rows, lambda i: pltpu.make_async_copy(
        x_ref.at[pl.ds(i, 1), :], o_ref.at[pl.ds(idx_ref[base + i], 1), :], sem))


def scatter_rows(x, idx):
    n, d = x.shape
    rows = min(n, ROWS_PER_STEP)
    return pl.pallas_call(
        _scatter_kernel,
        grid_spec=pltpu.PrefetchScalarGridSpec(
            num_scalar_prefetch=1,
            grid=(n // rows,),
            in_specs=[pl.BlockSpec((rows, d), lambda i, idx_ref: (i, 0))],
            out_specs=pl.BlockSpec(memory_space=pl.ANY),
            scratch_shapes=[pltpu.SemaphoreType.DMA(())],
        ),
        out_shape=jax.ShapeDtypeStruct((n, d), x.dtype),
        compiler_params=_params("arbitrary"),
        name="scatter_rows",
    )(idx, x)


def _ffn_kernel(xg_ref, w1_ref, w3_ref, w2_ref, g_ref, y_ref, hid_s, *, f1, tf):
    s = pl.program_id(1)

    @pl.when(s < f1)
    def _():
        xs = xg_ref[...]
        h1 = jnp.dot(xs, w1_ref[...].astype(BF16), preferred_element_type=F32)
        h3 = jnp.dot(xs, w3_ref[...].astype(BF16), preferred_element_type=F32)
        hid_s[s] = (jax.nn.silu(h1) * h3).astype(BF16)

    @pl.when(s >= f1)
    def _():
        acc = jnp.dot(hid_s[0], w2_ref[0:tf, :].astype(BF16), preferred_element_type=F32)
        for f in range(1, f1):
            acc += jnp.dot(hid_s[f], w2_ref[f * tf:(f + 1) * tf, :].astype(BF16), preferred_element_type=F32)
        y_ref[...] = acc * g_ref[...]


def expert_ffn(xg, w1, w3, w2, gate, layer):
    e_n, m, d = xg.shape
    ff = w1.shape[3]
    tf, tn = min(ff, 256), min(d, 512)
    f1, f2 = ff // tf, d // tn
    out_step = lambda s: jnp.maximum(s - f1, 0)
    return pl.pallas_call(
        functools.partial(_ffn_kernel, f1=f1, tf=tf),
        grid=(e_n, f1 + f2),
        in_specs=[
            pl.BlockSpec((None, m, d), lambda e, s: (e, 0, 0)),
            pl.BlockSpec((None, None, d, tf), lambda e, s: (layer, e, 0, jnp.minimum(s, f1 - 1))),
            pl.BlockSpec((None, None, d, tf), lambda e, s: (layer, e, 0, jnp.minimum(s, f1 - 1))),
            pl.BlockSpec((None, None, ff, tn), lambda e, s: (layer, e, 0, out_step(s))),
            pl.BlockSpec((None, m, 1), lambda e, s: (e, 0, 0)),
        ],
        out_specs=pl.BlockSpec((None, m, tn), lambda e, s: (e, 0, out_step(s))),
        out_shape=jax.ShapeDtypeStruct((e_n, m, d), F32),
        scratch_shapes=[pltpu.VMEM((f1, m, tf), BF16)],
        compiler_params=_params("arbitrary", "arbitrary"),
        name="expert_ffn",
    )(xg, w1, w3, w2, gate)


ITEM_ADD, ITEM_FIRST, ITEM_LAST = 1, 2, 4


def _combine_kernel(tile_ref, chunk_ref, flag_ref, yg_ref, st_ref, ct_ref, x_ref, gt_ref, o_ref, acc_s, *, ch):
    del tile_ref
    item = pl.program_id(0) * pl.num_programs(1) + pl.program_id(1)
    flags = flag_ref[item]

    @pl.when((flags & ITEM_FIRST) != 0)
    def _():
        acc_s[...] = jnp.zeros_like(acc_s)

    @pl.when((flags & ITEM_ADD) != 0)
    def _():
        pair = chunk_ref[item] * ch + lax.broadcasted_iota(I32, (ch, 1), 0)
        rel = pair - st_ref[...]
        own_t = jnp.where(jnp.logical_and(rel >= 0, rel < ct_ref[...]), 1.0, 0.0).astype(BF16)
        acc_s[...] += lax.dot_general(own_t, yg_ref[...].astype(BF16), (((0,), (0,)), ((), ())),
                                      preferred_element_type=F32)

    @pl.when((flags & ITEM_LAST) != 0)
    def _():
        o_ref[...] = x_ref[...] + gt_ref[0] * acc_s[...]


def _combine_work_items(start, *, batch, n, tt, pairs, ch):
    tiles, n_chunks = n // tt, pairs // ch
    n_items = tiles + n_chunks
    first = start.reshape(batch, n)[:, ::tt]
    last = jnp.concatenate([first[:, 1:], jnp.full((batch, 1), pairs, I32)], axis=1)
    c_first = jnp.minimum(first // ch, n_chunks - 1)
    c_num = jnp.where(last > first, (last - 1) // ch - first // ch + 1, 0)
    t_items = jnp.maximum(c_num, 1)
    ends = jnp.cumsum(t_items, axis=1)
    k = jnp.arange(n_items, dtype=I32)[None, :]
    tile = jnp.minimum(jnp.sum((ends[:, None, :] <= k[:, :, None]).astype(I32), axis=-1), tiles - 1)
    at = lambda v: jnp.take_along_axis(v, tile, axis=1)
    j = k - at(ends - t_items)
    real = k < ends[:, -1:]
    chunk = at(c_first) + jnp.minimum(j, jnp.maximum(at(c_num) - 1, 0))
    flags = (ITEM_ADD * (real & (j < at(c_num))) + ITEM_FIRST * (real & (j == 0))
             + ITEM_LAST * (real & (j == at(t_items) - 1)))
    flat = lambda v: v.reshape(-1).astype(I32)
    return n_items, flat(tile), flat(chunk), flat(flags)


def combine_residual(yg, start, cnt, x, mod3, rows, k_gate, *, batch, n, row0, pairs, pair_row0):
    d = x.shape[1]
    tt = min(n, 256)
    ch = min(pairs, 256)
    tiles, n_chunks = n // tt, pairs // ch
    n_items, item_tile, item_chunk, item_flags = _combine_work_items(start, batch=batch, n=n, tt=tt, pairs=pairs,
                                                                     ch=ch)
    mod_row = rows.mod_row(tt)
    rt0, chunk0 = row0 // tt, pair_row0 // ch

    def tile_of(b, k, tile_ref):
        return tile_ref[b * n_items + k]

    yg_map = lambda b, k, tile_ref, chunk_ref, flag_ref: (chunk0 + b * n_chunks + chunk_ref[b * n_items + k], 0)
    row_map = lambda b, k, tile_ref, chunk_ref, flag_ref: (rt0 + b * tiles + tile_of(b, k, tile_ref), 0)
    tok_map = lambda b, k, tile_ref, chunk_ref, flag_ref: (b, 0, tile_of(b, k, tile_ref))
    gate_map = lambda b, k, tile_ref, chunk_ref, flag_ref: (
        mod_row(rt0 + b * tiles + tile_of(b, k, tile_ref)), 0, k_gate)
    return pl.pallas_call(
        functools.partial(_combine_kernel, ch=ch),
        grid_spec=pltpu.PrefetchScalarGridSpec(
            num_scalar_prefetch=3,
            grid=(batch, n_items),
            in_specs=[
                pl.BlockSpec((ch, d), yg_map),
                pl.BlockSpec((None, 1, tt), tok_map),
                pl.BlockSpec((None, 1, tt), tok_map),
                pl.BlockSpec((tt, d), row_map),
                pl.BlockSpec((1, 1, d), gate_map),
            ],
            out_specs=pl.BlockSpec((tt, d), row_map),
            scratch_shapes=[pltpu.VMEM((tt, d), F32)],
        ),
        out_shape=jax.ShapeDtypeStruct(x.shape, F32),
        input_output_aliases={6: 0},
        compiler_params=_params("arbitrary", "arbitrary"),
        name="combine_residual",
    )(item_tile, item_chunk, item_flags, yg, start, cnt, x, mod3)


def expert_choice_ffn_residual(x, h, logits_t, mod3, rows, k_gate, w1, w3, w2, layer, *, batch, sets):
    e_n = logits_t.shape[0]
    d = x.shape[1]
    boff = jnp.arange(batch, dtype=I32)[:, None, None]
    routed, src_rows, pair_rows, gate_cols, pair_row0 = [], [], [], [], 0
    for n, row0 in sets:
        idx, dest, gate, start, cnt = expert_choice_route(logits_t, batch=batch, n=n, col0=row0)
        cap = idx.shape[2]
        pairs = e_n * cap
        src_rows.append(jnp.transpose(idx[..., 0] + row0 + boff * n, (1, 0, 2)).reshape(e_n, batch * cap))
        pair_rows.append(jnp.transpose(dest[..., 0] + pair_row0 + boff * pairs, (1, 0, 2)).reshape(e_n, batch * cap))
        gate_cols.append(jnp.transpose(gate, (1, 0, 2, 3)).reshape(e_n, batch * cap, 1))
        routed.append((start, cnt, n, row0, pairs, pair_row0))
        pair_row0 += batch * pairs
    src_rows = jnp.concatenate(src_rows, axis=1)
    m = src_rows.shape[1]
    xg = gather_rows(h, src_rows.reshape(-1), BF16).reshape(e_n, m, d)
    y = expert_ffn(xg, w1, w3, w2, jnp.concatenate(gate_cols, axis=1), layer).reshape(e_n * m, d)
    yg = scatter_rows(y, jnp.concatenate(pair_rows, axis=1).reshape(-1))
    for start, cnt, n, row0, pairs, pair_row0 in routed:
        x = combine_residual(yg, start, cnt, x, mod3, rows, k_gate, batch=batch, n=n, row0=row0, pairs=pairs,
                             pair_row0=pair_row0)
    return x


def kernel(x, c, ctx, c_ctx, ada_w, ada_b, norm1_g, norm2_g, w_in, na_rpb, gmlp_norm_g, gmlp_ws, gmlp_bs,
           w_branch_a, w_branch_b, w_out, router_w, exp_w1, exp_w3, exp_w2, final_norm_g):
    batch, n_lat, d = x.shape
    n_ctx = ctx.shape[1]
    depth = ada_w.shape[0]
    rows = Rows(batch, n_lat, n_ctx)
    lat_rows, all_rows = rows.lat_rows, rows.all_rows
    off_k, off_v, off_u = NA_WIDTH, 2 * NA_WIDTH, 3 * NA_WIDTH
    off_gv = off_u + GMLP_WIDTH
    off_ga = off_gv + GMLP_WIDTH
    off_gb = off_ga + d
    in_cols = off_gb + d

    x_head, x_tail = x.reshape(lat_rows, d), ctx.reshape(batch * n_ctx, d)
    cv = jnp.zeros((MOD_ROWS, d), F32).at[:batch].set(c).at[batch].set(c_ctx)

    for layer in range(depth):
        last = layer == depth - 1
        nrows = lat_rows if last else all_rows
        mod3 = ada_mod(cv, ada_w, ada_b, layer).reshape(MOD_ROWS, 1, N_MOD * d)

        h = norm_mod(x_head, x_tail, norm1_g, layer, mod3, rows, 0, 1, 0, nrows, BF16)
        qkv = matmul(h, w_in, layer, row0=0, nrows=nrows, col0=0, ncols=off_u, out_dtype=BF16,
                     scaled_cols=NA_WIDTH, scale=NA_HEAD_DIM ** -0.5)
        rest = matmul(h, w_in, layer, row0=0, nrows=nrows, col0=off_u, ncols=in_cols - off_u, out_dtype=F32)
        bias_table = build_col_table(na_rpb[layer])
        if last:
            h_ctx = norm_mod(x_head, x_tail, norm1_g, layer, mod3, rows, 0, 1, lat_rows, batch * n_ctx, BF16)
            kvc = matmul(h_ctx, w_in, layer, row0=0, nrows=batch * n_ctx, col0=off_k, ncols=2 * NA_WIDTH,
                         out_dtype=BF16)
            o_a = neighbourhood_attention(qkv, kvc, bias_table, batch=batch, n_lat=n_lat, n_ctx=n_ctx,
                                          kc_row0=0, kc_col0=0, vc_col0=NA_WIDTH)
        else:
            o_lat = neighbourhood_attention(qkv, qkv, bias_table, batch=batch, n_lat=n_lat, n_ctx=n_ctx,
                                            kc_row0=lat_rows, kc_col0=off_k, vc_col0=off_v)
            o_ctx = context_attention(qkv, batch=batch, n_ctx=n_ctx, row0=lat_rows)
            o_a = jnp.concatenate([o_lat, o_ctx], axis=0)
        o_b = spatial_gating(rest, gmlp_norm_g, gmlp_ws, gmlp_bs, layer, nrows=nrows, u_col0=0,
                             gv_col0=off_gv - off_u)
        y = gated_merge(o_a, o_b, w_branch_a, w_branch_b, rest, layer, nrows=nrows,
                        ga_col0=off_ga - off_u, gb_col0=off_gb - off_u)
        stream = outproj_residual(y, w_out, x_head, x_tail, mod3, layer, rows, 2, nrows=nrows)

        h2, logits_t = norm_mod_router(stream, norm2_g, layer, mod3, rows, 3, 4, router_w, nrows)
        sets = [(n_lat, 0)] if last else [(n_lat, 0), (n_ctx, lat_rows)]
        stream = expert_choice_ffn_residual(stream, h2, logits_t, mod3, rows, 5, exp_w1, exp_w3, exp_w2, layer,
                                            batch=batch, sets=sets)
        x_head = x_tail = stream

    return final_norm(stream, final_norm_g, lat_rows).reshape(batch, n_lat, d)
```

```python
import functools

import jax
import jax.numpy as jnp
from jax import lax
from jax.experimental import pallas as pl
from jax.experimental.pallas import tpu as pltpu

F32, BF16, I32 = jnp.float32, jnp.bfloat16, jnp.int32

GRID_W = 64
NA_HEADS = 16
NA_HEAD_DIM = 128
NA_WIDTH = NA_HEADS * NA_HEAD_DIM
WIN_H = 8
WIN_W = 16
GMLP_GROUPS = 16
GMLP_WIDTH = 2048
CHUNK = 128
N_EXPERTS = 16
CAPACITY_FACTOR = 2
N_MOD = 6
EPS = 1e-6
MASK_BIAS = -1e30

V7X_VMEM_LIMIT_BYTES = 56 * 1024 * 1024
LANES = 128
MOD_ROWS = 8


def _params(*sem):
    return pltpu.CompilerParams(dimension_semantics=sem, vmem_limit_bytes=V7X_VMEM_LIMIT_BYTES)


def _cast_weight(w_ref, wb_ref):
    rows = w_ref.shape[0]
    step = min(rows, 512)

    def body(i, carry):
        r = pl.multiple_of(i * step, step)
        wb_ref[pl.ds(r, step), :] = w_ref[pl.ds(r, step), :].astype(BF16)
        return carry

    lax.fori_loop(0, rows // step, body, 0)


def _ada_kernel(c_ref, w_ref, b_ref, o_ref):
    c = c_ref[...]
    s = c * jax.nn.sigmoid(c)
    o_ref[...] = jnp.dot(s.astype(BF16), w_ref[...].astype(BF16), preferred_element_type=F32) + b_ref[...]


def ada_mod(cv, ada_w, ada_b, layer):
    _, d, m = ada_w.shape
    tn = min(m, 512)
    return pl.pallas_call(
        _ada_kernel,
        grid=(m // tn,),
        in_specs=[
            pl.BlockSpec((MOD_ROWS, d), lambda j: (0, 0)),
            pl.BlockSpec((None, d, tn), lambda j: (layer, 0, j)),
            pl.BlockSpec((None, 1, tn), lambda j: (layer, 0, j)),
        ],
        out_specs=pl.BlockSpec((MOD_ROWS, tn), lambda j: (0, j)),
        out_shape=jax.ShapeDtypeStruct((MOD_ROWS, m), F32),
        compiler_params=_params("arbitrary"),
        name="ada_mod",
    )(cv, ada_w, ada_b.reshape(ada_b.shape[0], 1, m))


class Rows:
    def __init__(self, batch, n_lat, n_ctx):
        self.batch, self.n_lat, self.n_ctx = batch, n_lat, n_ctx
        self.lat_rows = batch * n_lat
        self.all_rows = self.lat_rows + batch * n_ctx

    def mod_row(self, tile_rows):
        lat_tiles = self.lat_rows // tile_rows
        per_batch = self.n_lat // tile_rows
        batch = self.batch

        def f(i):
            return jnp.where(i < lat_tiles, i // per_batch, batch)

        return f


def _mod_spec(mod_row, k, d):
    return pl.BlockSpec((1, 1, d), lambda i: (mod_row(i), 0, k))


def _rms(x, g):
    return x * lax.rsqrt(jnp.mean(x * x, axis=-1, keepdims=True) + EPS) * g


def _two_source_specs(block, head_tiles, tile_of):
    def head(*ids):
        t, c = tile_of(*ids)
        return (jnp.minimum(t, head_tiles - 1), c)

    def tail(*ids):
        t, c = tile_of(*ids)
        return (jnp.maximum(t - head_tiles, 0), c)

    return pl.BlockSpec(block, head), pl.BlockSpec(block, tail)


def _norm_mod_kernel(xh_ref, xt_ref, g_ref, sh_ref, sc_ref, o_ref, *, head_tiles, t0):
    def emit(x_ref):
        y = _rms(x_ref[...], g_ref[...])
        o_ref[...] = (y * (1.0 + sc_ref[0]) + sh_ref[0]).astype(o_ref.dtype)

    in_head = pl.program_id(0) + t0 < head_tiles
    pl.when(in_head)(lambda: emit(xh_ref))
    pl.when(jnp.logical_not(in_head))(lambda: emit(xt_ref))


def norm_mod(x_head, x_tail, g, layer, mod3, rows, k_shift, k_scale, row0, nrows, out_dtype):
    d = x_head.shape[1]
    tr = 256
    mod_row = rows.mod_row(tr)
    t0 = row0 // tr
    head_tiles = x_head.shape[0] // tr
    head_spec, tail_spec = _two_source_specs((tr, d), head_tiles, lambda i: (i + t0, 0))
    return pl.pallas_call(
        functools.partial(_norm_mod_kernel, head_tiles=head_tiles, t0=t0),
        grid=(nrows // tr,),
        in_specs=[
            head_spec,
            tail_spec,
            pl.BlockSpec((None, 1, d), lambda i: (layer, 0, 0)),
            pl.BlockSpec((1, 1, d), lambda i: (mod_row(i + t0), 0, k_shift)),
            pl.BlockSpec((1, 1, d), lambda i: (mod_row(i + t0), 0, k_scale)),
        ],
        out_specs=pl.BlockSpec((tr, d), lambda i: (i, 0)),
        out_shape=jax.ShapeDtypeStruct((nrows, d), out_dtype),
        compiler_params=_params("arbitrary"),
        name="norm_mod",
    )(x_head, x_tail, g.reshape(g.shape[0], 1, d), mod3, mod3)


def _final_norm_kernel(x_ref, g_ref, o_ref):
    o_ref[...] = _rms(x_ref[...], g_ref[...])


def final_norm(x, g, nrows):
    d = x.shape[1]
    tr = 256
    return pl.pallas_call(
        _final_norm_kernel,
        grid=(nrows // tr,),
        in_specs=[pl.BlockSpec((tr, d), lambda i: (i, 0)), pl.BlockSpec((1, d), lambda i: (0, 0))],
        out_specs=pl.BlockSpec((tr, d), lambda i: (i, 0)),
        out_shape=jax.ShapeDtypeStruct((nrows, d), F32),
        compiler_params=_params("arbitrary"),
        name="final_norm",
    )(x, g.reshape(1, d))


WEIGHT_TILE_BUFFERS = pl.Buffered(1)
MATMUL_ROW_TILES = 8

def _mm_kernel(a_ref, w_ref, o_ref, wb_ref, *, scaled_tiles, scale):
    @pl.when(pl.program_id(1) == 0)
    def _():
        _cast_weight(w_ref, wb_ref)

    acc = jnp.dot(a_ref[...], wb_ref[...], preferred_element_type=F32)
    if scaled_tiles:
        acc = acc * jnp.where(pl.program_id(0) < scaled_tiles, scale, 1.0)
    o_ref[...] = acc.astype(o_ref.dtype)


def matmul(a, w, layer, *, row0, nrows, col0, ncols, out_dtype, scaled_cols=0, scale=1.0, tm=512, tn=1024):
    k = a.shape[1]
    if nrows % (MATMUL_ROW_TILES * 16) == 0 and nrows // MATMUL_ROW_TILES > tm:
        tm = nrows // MATMUL_ROW_TILES
    tm, tn = min(tm, nrows), min(tn, ncols)
    r0, c0 = row0 // tm, col0 // tn
    assert scaled_cols % tn == 0 and row0 % tm == 0
    return pl.pallas_call(
        functools.partial(_mm_kernel, scaled_tiles=scaled_cols // tn, scale=scale),
        grid=(ncols // tn, nrows // tm),
        in_specs=[
            pl.BlockSpec((tm, k), lambda j, i: (i + r0, 0)),
            pl.BlockSpec((None, k, tn), lambda j, i: (layer, 0, j + c0), pipeline_mode=WEIGHT_TILE_BUFFERS),
        ],
        out_specs=pl.BlockSpec((tm, tn), lambda j, i: (i, j)),
        out_shape=jax.ShapeDtypeStruct((nrows, ncols), out_dtype),
        scratch_shapes=[pltpu.VMEM((k, tn), BF16)],
        compiler_params=_params("arbitrary", "arbitrary"),
        name="matmul",
    )(a, w)


def _merge_kernel(oa_ref, ob_ref, wa_ref, wb_ref, ga_ref, gb_ref, y_ref, wab_ref, wbb_ref):
    @pl.when(pl.program_id(1) == 0)
    def _():
        _cast_weight(wa_ref, wab_ref)
        _cast_weight(wb_ref, wbb_ref)

    ya = jnp.dot(oa_ref[...], wab_ref[...], preferred_element_type=F32)
    yb = jnp.dot(ob_ref[...], wbb_ref[...], preferred_element_type=F32)
    y = jax.nn.sigmoid(ga_ref[...].astype(F32)) * ya + jax.nn.sigmoid(gb_ref[...].astype(F32)) * yb
    y_ref[...] = y.astype(y_ref.dtype)


def gated_merge(o_a, o_b, w_a, w_b, gates, layer, *, nrows, ga_col0, gb_col0, tm=512, tn=1024):
    ka, kb = o_a.shape[1], o_b.shape[1]
    d = w_a.shape[2]
    tm, tn = min(tm, nrows), min(tn, d)
    ca, cb = ga_col0 // tn, gb_col0 // tn
    return pl.pallas_call(
        _merge_kernel,
        grid=(d // tn, nrows // tm),
        in_specs=[
            pl.BlockSpec((tm, ka), lambda j, i: (i, 0)),
            pl.BlockSpec((tm, kb), lambda j, i: (i, 0)),
            pl.BlockSpec((None, ka, tn), lambda j, i: (layer, 0, j), pipeline_mode=WEIGHT_TILE_BUFFERS),
            pl.BlockSpec((None, kb, tn), lambda j, i: (layer, 0, j), pipeline_mode=WEIGHT_TILE_BUFFERS),
            pl.BlockSpec((tm, tn), lambda j, i: (i, j + ca)),
            pl.BlockSpec((tm, tn), lambda j, i: (i, j + cb)),
        ],
        out_specs=pl.BlockSpec((tm, tn), lambda j, i: (i, j)),
        out_shape=jax.ShapeDtypeStruct((nrows, d), BF16),
        scratch_shapes=[pltpu.VMEM((ka, tn), BF16), pltpu.VMEM((kb, tn), BF16)],
        compiler_params=_params("arbitrary", "arbitrary"),
        name="gated_merge",
    )(o_a, o_b, w_a, w_b, gates, gates)


def _outproj_kernel(y_ref, w_ref, xh_ref, xt_ref, gt_ref, o_ref, wb_ref, *, head_tiles):
    @pl.when(pl.program_id(1) == 0)
    def _():
        _cast_weight(w_ref, wb_ref)

    upd = gt_ref[0] * jnp.dot(y_ref[...], wb_ref[...], preferred_element_type=F32)

    def emit(x_ref):
        o_ref[...] = x_ref[...] + upd

    in_head = pl.program_id(1) < head_tiles
    pl.when(in_head)(lambda: emit(xh_ref))
    pl.when(jnp.logical_not(in_head))(lambda: emit(xt_ref))


def outproj_residual(y, w_out, x_head, x_tail, mod3, layer, rows, k_gate, *, nrows, tm=512, tn=1024):
    k = y.shape[1]
    d = w_out.shape[2]
    tm, tn = min(tm, nrows), min(tn, d)
    mod_row = rows.mod_row(tm)
    gate_col0 = k_gate * (d // tn)
    head_tiles = x_head.shape[0] // tm
    head_spec, tail_spec = _two_source_specs((tm, tn), head_tiles, lambda j, i: (i, j))
    return pl.pallas_call(
        functools.partial(_outproj_kernel, head_tiles=head_tiles),
        grid=(d // tn, nrows // tm),
        in_specs=[
            pl.BlockSpec((tm, k), lambda j, i: (i, 0)),
            pl.BlockSpec((None, k, tn), lambda j, i: (layer, 0, j), pipeline_mode=WEIGHT_TILE_BUFFERS),
            head_spec,
            tail_spec,
            pl.BlockSpec((1, 1, tn), lambda j, i: (mod_row(i), 0, j + gate_col0)),
        ],
        out_specs=pl.BlockSpec((tm, tn), lambda j, i: (i, j)),
        out_shape=jax.ShapeDtypeStruct((nrows, d), F32),
        scratch_shapes=[pltpu.VMEM((k, tn), BF16)],
        compiler_params=_params("arbitrary", "arbitrary"),
        name="outproj_residual",
    )(y, w_out, x_head, x_tail, mod3)


Q_ROWS = 4
BAND_ROWS = WIN_H + Q_ROWS
N_PATTERNS = 3


def build_col_table(rpb):
    cols = jnp.arange(GRID_W)
    col_start = jnp.clip(cols - WIN_W // 2, 0, GRID_W - WIN_W)
    col_ok = (cols[None, :] >= col_start[:, None]) & (cols[None, :] < col_start[:, None] + WIN_W)
    col_off = jnp.clip(cols[None, :] - cols[:, None] + (WIN_W - 1), 0, 2 * WIN_W - 2)
    return jnp.where(col_ok[None, None], rpb[:, :, col_off], MASK_BIAS).astype(F32)


def _band_patterns(grid_rows):
    half = WIN_H // 2
    groups = grid_rows // Q_ROWS
    assert grid_rows % Q_ROWS == 0 and grid_rows >= BAND_ROWS
    patterns = []
    for g in (0, min(1, groups - 1), groups - 1):
        band_start = min(max(g * Q_ROWS - half, 0), grid_rows - BAND_ROWS)
        pattern = []
        for a in range(Q_ROWS):
            q_row = g * Q_ROWS + a
            win_start = min(max(q_row - half, 0), grid_rows - WIN_H)
            pattern.append([band_start + i - q_row + WIN_H - 1 if win_start <= band_start + i < win_start + WIN_H
                            else None for i in range(BAND_ROWS)])
        patterns.append(pattern)
    return patterns


def _softmax_pv(parts):
    m = functools.reduce(jnp.maximum, [jnp.max(s, axis=-1, keepdims=True) for s, _ in parts])
    es = [jnp.exp(s - m) for s, _ in parts]
    denom = functools.reduce(jnp.add, [jnp.sum(e, axis=-1, keepdims=True) for e in es])
    acc = functools.reduce(jnp.add, [jnp.dot(e.astype(BF16), v, preferred_element_type=F32)
                                     for e, (_, v) in zip(es, parts)])
    return acc / denom


def _qk(q, k):
    return lax.dot_general(q, k, (((1,), (1,)), ((), ())), preferred_element_type=F32)


def _natten_kernel(q_ref, k_ref, v_ref, kc_ref, vc_ref, ct_ref, o_ref, bt_ref, *, grid_rows):
    groups = grid_rows // Q_ROWS
    nq, band = Q_ROWS * GRID_W, BAND_ROWS * GRID_W
    kc = kc_ref[...]
    vc = vc_ref[...]

    @pl.when(jnp.logical_and(pl.program_id(0) == 0, pl.program_id(1) == 0))
    def _():
        bt_ref[...] = jnp.full(bt_ref.shape, MASK_BIAS, F32)

    for p, pattern in enumerate(_band_patterns(grid_rows)):
        for a, row in enumerate(pattern):
            for i, off in enumerate(row):
                if off is not None:
                    bt_ref[p, a * GRID_W:(a + 1) * GRID_W, i * GRID_W:(i + 1) * GRID_W] = ct_ref[off]

    def body(g, carry):
        band_start = jnp.clip(g * Q_ROWS - WIN_H // 2, 0, grid_rows - BAND_ROWS)
        pattern = jnp.where(g == 0, 0, jnp.where(g == groups - 1, 2, 1))
        q0 = pl.multiple_of(g * nq, nq)
        k0 = pl.multiple_of(band_start * GRID_W, GRID_W)
        q = q_ref[pl.ds(q0, nq), :]
        kb = k_ref[pl.ds(k0, band), :]
        vb = v_ref[pl.ds(k0, band), :]
        s_loc = _qk(q, kb) + bt_ref[pattern]
        s_ctx = _qk(q, kc)
        o_ref[pl.ds(q0, nq), :] = _softmax_pv([(s_loc, vb), (s_ctx, vc)]).astype(o_ref.dtype)
        return carry

    lax.fori_loop(0, groups, body, 0, unroll=2)


def neighbourhood_attention(qkv, kvc, col_table, *, batch, n_lat, n_ctx, kc_row0, kc_col0, vc_col0):
    dh, h = NA_HEAD_DIM, NA_HEADS
    grid_rows = n_lat // GRID_W
    rb0, ck, cv = kc_row0 // n_ctx, kc_col0 // dh, vc_col0 // dh
    return pl.pallas_call(
        functools.partial(_natten_kernel, grid_rows=grid_rows),
        grid=(batch, h),
        in_specs=[
            pl.BlockSpec((n_lat, dh), lambda b, hh: (b, hh)),
            pl.BlockSpec((n_lat, dh), lambda b, hh: (b, h + hh)),
            pl.BlockSpec((n_lat, dh), lambda b, hh: (b, 2 * h + hh)),
            pl.BlockSpec((n_ctx, dh), lambda b, hh: (rb0 + b, ck + hh)),
            pl.BlockSpec((n_ctx, dh), lambda b, hh: (rb0 + b, cv + hh)),
            pl.BlockSpec((None, 2 * WIN_H - 1, GRID_W, GRID_W), lambda b, hh: (hh, 0, 0, 0)),
        ],
        out_specs=pl.BlockSpec((n_lat, dh), lambda b, hh: (b, hh)),
        out_shape=jax.ShapeDtypeStruct((batch * n_lat, NA_WIDTH), BF16),
        scratch_shapes=[pltpu.VMEM((N_PATTERNS, Q_ROWS * GRID_W, BAND_ROWS * GRID_W), F32)],
        compiler_params=_params("arbitrary", "arbitrary"),
        name="neighbourhood_attention",
    )(qkv, qkv, qkv, kvc, kvc, col_table)


def _ctx_attn_kernel(q_ref, k_ref, v_ref, o_ref):
    o_ref[...] = _softmax_pv([(_qk(q_ref[...], k_ref[...]), v_ref[...])]).astype(o_ref.dtype)


def context_attention(qkv, *, batch, n_ctx, row0):
    dh, h = NA_HEAD_DIM, NA_HEADS
    rb0 = row0 // n_ctx
    return pl.pallas_call(
        _ctx_attn_kernel,
        grid=(batch, h),
        in_specs=[
            pl.BlockSpec((n_ctx, dh), lambda b, hh: (rb0 + b, hh)),
            pl.BlockSpec((n_ctx, dh), lambda b, hh: (rb0 + b, h + hh)),
            pl.BlockSpec((n_ctx, dh), lambda b, hh: (rb0 + b, 2 * h + hh)),
        ],
        out_specs=pl.BlockSpec((n_ctx, dh), lambda b, hh: (b, hh)),
        out_shape=jax.ShapeDtypeStruct((batch * n_ctx, NA_WIDTH), BF16),
        compiler_params=_params("arbitrary", "arbitrary"),
        name="context_attention",
    )(qkv, qkv, qkv)


def _gmlp_kernel(u_ref, gv_ref, ng_ref, ws_ref, bst_ref, o_ref):
    gd = GMLP_WIDTH // GMLP_GROUPS
    for c in range(u_ref.shape[0] // CHUNK):
        rows = slice(c * CHUNK, (c + 1) * CHUNK)
        v = jax.nn.gelu(gv_ref[rows, :].astype(F32))
        vc = v - jnp.mean(v, axis=-1, keepdims=True)
        y = vc * lax.rsqrt(jnp.mean(vc * vc, axis=-1, keepdims=True) + EPS) * ng_ref[...]
        yb = y.astype(BF16)
        for g in range(GMLP_GROUPS):
            sl = slice(g * gd, (g + 1) * gd)
            mixed = jnp.dot(ws_ref[g].astype(BF16), yb[:, sl], preferred_element_type=F32) + bst_ref[:, g:g + 1]
            o_ref[rows, sl] = (jax.nn.gelu(u_ref[rows, sl].astype(F32)) * mixed).astype(o_ref.dtype)


def spatial_gating(p, norm_g, ws, bs, layer, *, nrows, u_col0, gv_col0):
    w = GMLP_WIDTH
    bst = jnp.transpose(bs[layer])
    tr = 2 * CHUNK
    return pl.pallas_call(
        _gmlp_kernel,
        grid=(nrows // tr,),
        in_specs=[
            pl.BlockSpec((tr, w), lambda i: (i, u_col0 // w)),
            pl.BlockSpec((tr, w), lambda i: (i, gv_col0 // w)),
            pl.BlockSpec((None, 1, w), lambda i: (layer, 0, 0)),
            pl.BlockSpec((None, GMLP_GROUPS, CHUNK, CHUNK), lambda i: (layer, 0, 0, 0)),
            pl.BlockSpec((CHUNK, GMLP_GROUPS), lambda i: (0, 0)),
        ],
        out_specs=pl.BlockSpec((tr, w), lambda i: (i, 0)),
        out_shape=jax.ShapeDtypeStruct((nrows, w), BF16),
        compiler_params=_params("arbitrary"),
        name="spatial_gating",
    )(p, p, norm_g.reshape(norm_g.shape[0], 1, w), ws, bst)


def _split_bf16(x):
    hi = x.astype(BF16)
    return hi, (x - hi.astype(F32)).astype(BF16)


def _norm_router_kernel(x_ref, g_ref, sh_ref, sc_ref, rw_ref, h_ref, lg_ref):
    h = _rms(x_ref[...], g_ref[...]) * (1.0 + sc_ref[0]) + sh_ref[0]
    h_ref[...] = h
    hh, hl = _split_bf16(h)
    wh, wl = _split_bf16(rw_ref[...])
    lg = (jnp.dot(hh, wh, preferred_element_type=F32) + jnp.dot(hl, wh, preferred_element_type=F32)
          + jnp.dot(hh, wl, preferred_element_type=F32))
    lg_ref[...] = jnp.transpose(lg)[:N_EXPERTS, :]


def norm_mod_router(x, g, layer, mod3, rows, k_shift, k_scale, router_w, nrows):
    d = x.shape[1]
    tr = 256
    mod_row = rows.mod_row(tr)
    rw = jnp.pad(router_w[layer], ((0, 0), (0, LANES - N_EXPERTS)))
    return pl.pallas_call(
        _norm_router_kernel,
        grid=(nrows // tr,),
        in_specs=[
            pl.BlockSpec((tr, d), lambda i: (i, 0)),
            pl.BlockSpec((None, 1, d), lambda i: (layer, 0, 0)),
            pl.BlockSpec((1, 1, d), lambda i: (mod_row(i), 0, k_shift)),
            pl.BlockSpec((1, 1, d), lambda i: (mod_row(i), 0, k_scale)),
            pl.BlockSpec((d, LANES), lambda i: (0, 0)),
        ],
        out_specs=[pl.BlockSpec((tr, d), lambda i: (i, 0)), pl.BlockSpec((N_EXPERTS, tr), lambda i: (0, i))],
        out_shape=[jax.ShapeDtypeStruct((nrows, d), F32), jax.ShapeDtypeStruct((N_EXPERTS, nrows), F32)],
        compiler_params=_params("arbitrary"),
        name="norm_mod_router",
    )(x, g.reshape(g.shape[0], 1, d), mod3, mod3, rw)


def _cumsum_lanes(x):
    r, n = x.shape
    blk = min(n, LANES)
    tri = (lax.broadcasted_iota(I32, (blk, blk), 0) <= lax.broadcasted_iota(I32, (blk, blk), 1)).astype(BF16)
    off = jnp.zeros((r, 1), F32)
    out = []
    for j in range(n // blk):
        cs = jnp.dot(x[:, j * blk:(j + 1) * blk].astype(BF16), tri, preferred_element_type=F32) + off
        out.append(cs)
        off = cs[:, blk - 1:blk]
    return jnp.concatenate(out, axis=1) if len(out) > 1 else out[0]


def _route_kernel(lg_ref, idx_ref, dest_ref, gate_ref, start_ref, cnt_ref, slot_s, aff_s, *, cap):
    e_n, n = lg_ref.shape
    lg = lg_ref[...]
    ex = jnp.exp(lg - jnp.max(lg, axis=0, keepdims=True))
    aff = ex / jnp.sum(ex, axis=0, keepdims=True)
    bits = lax.bitcast_convert_type(aff, I32)

    thr = jnp.zeros((e_n, 1), I32)
    for bit in range(30, -1, -1):
        cand = thr | (1 << bit)
        n_ge = jnp.sum((bits >= cand).astype(F32), axis=1, keepdims=True)
        thr = jnp.where(n_ge >= cap, cand, thr)
    above = bits > thr
    tied = (bits == thr).astype(F32)
    need = cap - jnp.sum(above.astype(F32), axis=1, keepdims=True)
    tied_before = _cumsum_lanes(tied) - tied
    sel = jnp.where(above, 1.0, jnp.where(tied_before < need, tied, 0.0))

    slot = _cumsum_lanes(sel) - sel
    per_tok = jnp.sum(sel, axis=0, keepdims=True)
    per_tok8 = jnp.broadcast_to(per_tok, (8, n))
    start = (_cumsum_lanes(per_tok8) - per_tok8)[0:1]
    start_ref[...] = start.astype(I32)
    cnt_ref[...] = per_tok.astype(I32)
    slot_s[...] = jnp.where(sel > 0.0, slot, -1.0)
    aff_s[...] = aff

    tok = lax.broadcasted_iota(I32, (1, n), 1).astype(F32)
    cc = min(cap, LANES)

    def body(e, chosen_by_earlier):
        srow = slot_s[pl.ds(e, 1), :]
        arow = aff_s[pl.ds(e, 1), :]
        drow = start + chosen_by_earlier
        for c0 in range(0, cap, cc):
            want = (lax.broadcasted_iota(I32, (cc, 1), 0) + c0).astype(F32)
            hit = srow == want
            pick = lambda row: jnp.sum(jnp.where(hit, row, 0.0), axis=1, keepdims=True)
            idx_ref[e, pl.ds(c0, cc), :] = pick(tok).astype(I32)
            dest_ref[e, pl.ds(c0, cc), :] = pick(drow).astype(I32)
            gate_ref[e, pl.ds(c0, cc), :] = pick(arow)
        return chosen_by_earlier + (srow >= 0.0).astype(F32)

    lax.fori_loop(0, e_n, body, jnp.zeros((1, n), F32))


def expert_choice_route(logits_t, *, batch, n, col0):
    e_n = logits_t.shape[0]
    cap = CAPACITY_FACTOR * n // e_n
    cb0 = col0 // n
    per_slot = lambda dt: jax.ShapeDtypeStruct((batch, e_n, cap, 1), dt)
    per_tok = jax.ShapeDtypeStruct((batch, 1, n), I32)
    slot_spec = pl.BlockSpec((None, e_n, cap, 1), lambda b: (b, 0, 0, 0))
    tok_spec = pl.BlockSpec((None, 1, n), lambda b: (b, 0, 0))
    return pl.pallas_call(
        functools.partial(_route_kernel, cap=cap),
        grid=(batch,),
        in_specs=[pl.BlockSpec((e_n, n), lambda b: (0, cb0 + b))],
        out_specs=[slot_spec, slot_spec, slot_spec, tok_spec, tok_spec],
        out_shape=[per_slot(I32), per_slot(I32), per_slot(F32), per_tok, per_tok],
        scratch_shapes=[pltpu.VMEM((e_n, n), F32), pltpu.VMEM((e_n, n), F32)],
        compiler_params=_params("arbitrary"),
        name="expert_choice_route",
    )(logits_t)


ROWS_PER_STEP = 256
ROW_COPY_UNROLL = 8


def _row_copies(n, make_copy):
    def start(i, carry):
        make_copy(i).start()
        return carry

    def wait(i, carry):
        make_copy(i).wait()
        return carry

    lax.fori_loop(0, n, start, 0, unroll=ROW_COPY_UNROLL)
    lax.fori_loop(0, n, wait, 0, unroll=ROW_COPY_UNROLL)


def _gather_kernel(idx_ref, src_ref, o_ref, stage_ref, sem):
    rows = o_ref.shape[0]
    base = pl.program_id(0) * rows
    _row_copies(rows, lambda i: pltpu.make_async_copy(
        src_ref.at[pl.ds(idx_ref[base + i], 1), :], stage_ref.at[pl.ds(i, 1), :], sem))
    o_ref[...] = stage_ref[...].astype(o_ref.dtype)


def gather_rows(src, idx, out_dtype):
    n, d = idx.shape[0], src.shape[1]
    rows = min(n, ROWS_PER_STEP)
    return pl.pallas_call(
        _gather_kernel,
        grid_spec=pltpu.PrefetchScalarGridSpec(
            num_scalar_prefetch=1,
            grid=(n // rows,),
            in_specs=[pl.BlockSpec(memory_space=pl.ANY)],
            out_specs=pl.BlockSpec((rows, d), lambda i, idx_ref: (i, 0)),
            scratch_shapes=[pltpu.VMEM((rows, d), src.dtype), pltpu.SemaphoreType.DMA(())],
        ),
        out_shape=jax.ShapeDtypeStruct((n, d), out_dtype),
        compiler_params=_params("arbitrary"),
        name="gather_rows",
    )(idx, src)


def _scatter_kernel(idx_ref, x_ref, o_ref, stage_ref, sem):
    rows = x_ref.shape[0]
    base = pl.program_id(0) * rows
    stage_ref[...] = x_ref[...].astype(stage_ref.dtype)
    _row_copies(rows, lambda i: pltpu.make_async_copy(
        stage_ref.at[pl.ds(i, 1), :], o_ref.at[pl.ds(idx_ref[base + i], 1), :], sem))


def scatter_rows(x, idx, out_dtype):
    n, d = x.shape
    rows = min(n, ROWS_PER_STEP)
    return pl.pallas_call(
        _scatter_kernel,
        grid_spec=pltpu.PrefetchScalarGridSpec(
            num_scalar_prefetch=1,
            grid=(n // rows,),
            in_specs=[pl.BlockSpec((rows, d), lambda i, idx_ref: (i, 0))],
            out_specs=pl.BlockSpec(memory_space=pl.ANY),
            scratch_shapes=[pltpu.VMEM((rows, d), out_dtype), pltpu.SemaphoreType.DMA(())],
        ),
        out_shape=jax.ShapeDtypeStruct((n, d), out_dtype),
        compiler_params=_params("arbitrary"),
        name="scatter_rows",
    )(idx, x)


def _ffn_kernel(xg_ref, w1_ref, w3_ref, w2_ref, g_ref, y_ref, hid_s, *, f1, tf):
    s = pl.program_id(1)

    @pl.when(s < f1)
    def _():
        xs = xg_ref[...]
        h1 = jnp.dot(xs, w1_ref[...].astype(BF16), preferred_element_type=F32)
        h3 = jnp.dot(xs, w3_ref[...].astype(BF16), preferred_element_type=F32)
        hid_s[s] = (jax.nn.silu(h1) * h3).astype(BF16)

    @pl.when(s >= f1)
    def _():
        acc = jnp.dot(hid_s[0], w2_ref[0:tf, :].astype(BF16), preferred_element_type=F32)
        for f in range(1, f1):
            acc += jnp.dot(hid_s[f], w2_ref[f * tf:(f + 1) * tf, :].astype(BF16), preferred_element_type=F32)
        y_ref[...] = (acc * g_ref[...]).astype(y_ref.dtype)


def expert_ffn(xg, w1, w3, w2, gate, layer):
    e_n, m, d = xg.shape
    ff = w1.shape[3]
    tf, tn = min(ff, 256), min(d, 512)
    f1, f2 = ff // tf, d // tn
    out_step = lambda s: jnp.maximum(s - f1, 0)
    return pl.pallas_call(
        functools.partial(_ffn_kernel, f1=f1, tf=tf),
        grid=(e_n, f1 + f2),
        in_specs=[
            pl.BlockSpec((None, m, d), lambda e, s: (e, 0, 0)),
            pl.BlockSpec((None, None, d, tf), lambda e, s: (layer, e, 0, jnp.minimum(s, f1 - 1))),
            pl.BlockSpec((None, None, d, tf), lambda e, s: (layer, e, 0, jnp.minimum(s, f1 - 1))),
            pl.BlockSpec((None, None, ff, tn), lambda e, s: (layer, e, 0, out_step(s))),
            pl.BlockSpec((None, m, 1), lambda e, s: (e, 0, 0)),
        ],
        out_specs=pl.BlockSpec((None, m, tn), lambda e, s: (e, 0, out_step(s))),
        out_shape=jax.ShapeDtypeStruct((e_n, m, d), BF16),
        scratch_shapes=[pltpu.VMEM((f1, m, tf), BF16)],
        compiler_params=_params("arbitrary", "arbitrary"),
        name="expert_ffn",
    )(xg, w1, w3, w2, gate)


ITEM_ADD, ITEM_FIRST, ITEM_LAST = 1, 2, 4


def _combine_kernel(tile_ref, chunk_ref, flag_ref, yg_ref, st_ref, ct_ref, x_ref, gt_ref, o_ref, acc_s, *, ch):
    del tile_ref
    item = pl.program_id(0) * pl.num_programs(1) + pl.program_id(1)
    flags = flag_ref[item]

    @pl.when((flags & ITEM_FIRST) != 0)
    def _():
        acc_s[...] = jnp.zeros_like(acc_s)

    @pl.when((flags & ITEM_ADD) != 0)
    def _():
        pair = chunk_ref[item] * ch + lax.broadcasted_iota(I32, (ch, 1), 0)
        rel = pair - st_ref[...]
        own_t = jnp.where(jnp.logical_and(rel >= 0, rel < ct_ref[...]), 1.0, 0.0).astype(BF16)
        acc_s[...] += lax.dot_general(own_t, yg_ref[...].astype(BF16), (((0,), (0,)), ((), ())),
                                      preferred_element_type=F32)

    @pl.when((flags & ITEM_LAST) != 0)
    def _():
        o_ref[...] = x_ref[...] + gt_ref[0] * acc_s[...]


def _combine_work_items(start, *, batch, n, tt, pairs, ch):
    tiles, n_chunks = n // tt, pairs // ch
    n_items = tiles + n_chunks
    first = start.reshape(batch, n)[:, ::tt]
    last = jnp.concatenate([first[:, 1:], jnp.full((batch, 1), pairs, I32)], axis=1)
    c_first = jnp.minimum(first // ch, n_chunks - 1)
    c_num = jnp.where(last > first, (last - 1) // ch - first // ch + 1, 0)
    t_items = jnp.maximum(c_num, 1)
    ends = jnp.cumsum(t_items, axis=1)
    k = jnp.arange(n_items, dtype=I32)[None, :]
    tile = jnp.minimum(jnp.sum((ends[:, None, :] <= k[:, :, None]).astype(I32), axis=-1), tiles - 1)
    at = lambda v: jnp.take_along_axis(v, tile, axis=1)
    j = k - at(ends - t_items)
    real = k < ends[:, -1:]
    chunk = at(c_first) + jnp.minimum(j, jnp.maximum(at(c_num) - 1, 0))
    flags = (ITEM_ADD * (real & (j < at(c_num))) + ITEM_FIRST * (real & (j == 0))
             + ITEM_LAST * (real & (j == at(t_items) - 1)))
    flat = lambda v: v.reshape(-1).astype(I32)
    return n_items, flat(tile), flat(chunk), flat(flags)


def combine_residual(yg, start, cnt, x, mod3, rows, k_gate, *, batch, n, row0, pairs, pair_row0):
    d = x.shape[1]
    tt = min(n, 256)
    ch = min(pairs, 256)
    tiles, n_chunks = n // tt, pairs // ch
    n_items, item_tile, item_chunk, item_flags = _combine_work_items(start, batch=batch, n=n, tt=tt, pairs=pairs,
                                                                     ch=ch)
    mod_row = rows.mod_row(tt)
    rt0, chunk0 = row0 // tt, pair_row0 // ch

    def tile_of(b, k, tile_ref):
        return tile_ref[b * n_items + k]

    yg_map = lambda b, k, tile_ref, chunk_ref, flag_ref: (chunk0 + b * n_chunks + chunk_ref[b * n_items + k], 0)
    row_map = lambda b, k, tile_ref, chunk_ref, flag_ref: (rt0 + b * tiles + tile_of(b, k, tile_ref), 0)
    tok_map = lambda b, k, tile_ref, chunk_ref, flag_ref: (b, 0, tile_of(b, k, tile_ref))
    gate_map = lambda b, k, tile_ref, chunk_ref, flag_ref: (
        mod_row(rt0 + b * tiles + tile_of(b, k, tile_ref)), 0, k_gate)
    return pl.pallas_call(
        functools.partial(_combine_kernel, ch=ch),
        grid_spec=pltpu.PrefetchScalarGridSpec(
            num_scalar_prefetch=3,
            grid=(batch, n_items),
            in_specs=[
                pl.BlockSpec((ch, d), yg_map),
                pl.BlockSpec((None, 1, tt), tok_map),
                pl.BlockSpec((None, 1, tt), tok_map),
                pl.BlockSpec((tt, d), row_map),
                pl.BlockSpec((1, 1, d), gate_map),
            ],
            out_specs=pl.BlockSpec((tt, d), row_map),
            scratch_shapes=[pltpu.VMEM((tt, d), F32)],
        ),
        out_shape=jax.ShapeDtypeStruct(x.shape, F32),
        input_output_aliases={6: 0},
        compiler_params=_params("arbitrary", "arbitrary"),
        name="combine_residual",
    )(item_tile, item_chunk, item_flags, yg, start, cnt, x, mod3)


def expert_choice_ffn_residual(x, h, logits_t, mod3, rows, k_gate, w1, w3, w2, layer, *, batch, sets):
    e_n = logits_t.shape[0]
    d = x.shape[1]
    boff = jnp.arange(batch, dtype=I32)[:, None, None]
    routed, src_rows, pair_rows, gate_cols, pair_row0 = [], [], [], [], 0
    for n, row0 in sets:
        idx, dest, gate, start, cnt = expert_choice_route(logits_t, batch=batch, n=n, col0=row0)
        cap = idx.shape[2]
        pairs = e_n * cap
        src_rows.append(jnp.transpose(idx[..., 0] + row0 + boff * n, (1, 0, 2)).reshape(e_n, batch * cap))
        pair_rows.append(jnp.transpose(dest[..., 0] + pair_row0 + boff * pairs, (1, 0, 2)).reshape(e_n, batch * cap))
        gate_cols.append(jnp.transpose(gate, (1, 0, 2, 3)).reshape(e_n, batch * cap, 1))
        routed.append((start, cnt, n, row0, pairs, pair_row0))
        pair_row0 += batch * pairs
    src_rows = jnp.concatenate(src_rows, axis=1)
    m = src_rows.shape[1]
    xg = gather_rows(h, src_rows.reshape(-1), BF16).reshape(e_n, m, d)
    y = expert_ffn(xg, w1, w3, w2, jnp.concatenate(gate_cols, axis=1), layer).reshape(e_n * m, d)
    yg = scatter_rows(y, jnp.concatenate(pair_rows, axis=1).reshape(-1), F32)
    for start, cnt, n, row0, pairs, pair_row0 in routed:
        x = combine_residual(yg, start, cnt, x, mod3, rows, k_gate, batch=batch, n=n, row0=row0, pairs=pairs,
                             pair_row0=pair_row0)
    return x


def kernel(x, c, ctx, c_ctx, ada_w, ada_b, norm1_g, norm2_g, w_in, na_rpb, gmlp_norm_g, gmlp_ws, gmlp_bs,
           w_branch_a, w_branch_b, w_out, router_w, exp_w1, exp_w3, exp_w2, final_norm_g):
    batch, n_lat, d = x.shape
    n_ctx = ctx.shape[1]
    depth = ada_w.shape[0]
    rows = Rows(batch, n_lat, n_ctx)
    lat_rows, all_rows = rows.lat_rows, rows.all_rows
    off_k, off_v, off_u = NA_WIDTH, 2 * NA_WIDTH, 3 * NA_WIDTH
    off_gv = off_u + GMLP_WIDTH
    off_ga = off_gv + GMLP_WIDTH
    off_gb = off_ga + d
    in_cols = off_gb + d

    x_head, x_tail = x.reshape(lat_rows, d), ctx.reshape(batch * n_ctx, d)
    cv = jnp.zeros((MOD_ROWS, d), F32).at[:batch].set(c).at[batch].set(c_ctx)

    for layer in range(depth):
        last = layer == depth - 1
        nrows = lat_rows if last else all_rows
        mod3 = ada_mod(cv, ada_w, ada_b, layer).reshape(MOD_ROWS, 1, N_MOD * d)

        h = norm_mod(x_head, x_tail, norm1_g, layer, mod3, rows, 0, 1, 0, nrows, BF16)
        qkv = rest = matmul(h, w_in, layer, row0=0, nrows=nrows, col0=0, ncols=in_cols, out_dtype=BF16,
                            scaled_cols=NA_WIDTH, scale=NA_HEAD_DIM ** -0.5)
        bias_table = build_col_table(na_rpb[layer])
        if last:
            h_ctx = norm_mod(x_head, x_tail, norm1_g, layer, mod3, rows, 0, 1, lat_rows, batch * n_ctx, BF16)
            kvc = matmul(h_ctx, w_in, layer, row0=0, nrows=batch * n_ctx, col0=off_k, ncols=2 * NA_WIDTH,
                         out_dtype=BF16)
            o_a = neighbourhood_attention(qkv, kvc, bias_table, batch=batch, n_lat=n_lat, n_ctx=n_ctx,
                                          kc_row0=0, kc_col0=0, vc_col0=NA_WIDTH)
        else:
            o_lat = neighbourhood_attention(qkv, qkv, bias_table, batch=batch, n_lat=n_lat, n_ctx=n_ctx,
                                            kc_row0=lat_rows, kc_col0=off_k, vc_col0=off_v)
            o_ctx = context_attention(qkv, batch=batch, n_ctx=n_ctx, row0=lat_rows)
            o_a = jnp.concatenate([o_lat, o_ctx], axis=0)
        o_b = spatial_gating(rest, gmlp_norm_g, gmlp_ws, gmlp_bs, layer, nrows=nrows, u_col0=off_u, gv_col0=off_gv)
        y = gated_merge(o_a, o_b, w_branch_a, w_branch_b, rest, layer, nrows=nrows, ga_col0=off_ga, gb_col0=off_gb)
        stream = outproj_residual(y, w_out, x_head, x_tail, mod3, layer, rows, 2, nrows=nrows)

        h2, logits_t = norm_mod_router(stream, norm2_g, layer, mod3, rows, 3, 4, router_w, nrows)
        sets = [(n_lat, 0)] if last else [(n_lat, 0), (n_ctx, lat_rows)]
        stream = expert_choice_ffn_residual(stream, h2, logits_t, mod3, rows, 5, exp_w1, exp_w3, exp_w2, layer,
                                            batch=batch, sets=sets)
        x_head = x_tail = stream

    return final_norm(stream, final_norm_g, lat_rows).reshape(batch, n_lat, d)
```

```python
import functools

import jax
import jax.numpy as jnp
from jax import lax
from jax.experimental import pallas as pl
from jax.experimental.pallas import tpu as pltpu

F32, BF16, I32 = jnp.float32, jnp.bfloat16, jnp.int32

GRID_W = 64
NA_HEADS = 16
NA_HEAD_DIM = 128
NA_WIDTH = NA_HEADS * NA_HEAD_DIM
WIN_H = 8
WIN_W = 16
GMLP_GROUPS = 16
GMLP_WIDTH = 2048
CHUNK = 128
N_EXPERTS = 16
CAPACITY_FACTOR = 2
N_MOD = 6
EPS = 1e-6
MASK_BIAS = -1e30

V7X_VMEM_LIMIT_BYTES = 56 * 1024 * 1024
LANES = 128
MOD_ROWS = 8


def _params(*sem):
    return pltpu.CompilerParams(dimension_semantics=sem, vmem_limit_bytes=V7X_VMEM_LIMIT_BYTES)


def _cast_weight(w_ref, wb_ref):
    rows = w_ref.shape[0]
    step = min(rows, 512)

    def body(i, carry):
        r = pl.multiple_of(i * step, step)
        wb_ref[pl.ds(r, step), :] = w_ref[pl.ds(r, step), :].astype(BF16)
        return carry

    lax.fori_loop(0, rows // step, body, 0)


def _ada_kernel(c_ref, w_ref, b_ref, o_ref):
    c = c_ref[...]
    s = c * jax.nn.sigmoid(c)
    o_ref[...] = jnp.dot(s.astype(BF16), w_ref[...].astype(BF16), preferred_element_type=F32) + b_ref[...]


def ada_mod(cv, ada_w, ada_b, layer):
    _, d, m = ada_w.shape
    tn = min(m, 512)
    return pl.pallas_call(
        _ada_kernel,
        grid=(m // tn,),
        in_specs=[
            pl.BlockSpec((MOD_ROWS, d), lambda j: (0, 0)),
            pl.BlockSpec((None, d, tn), lambda j: (layer, 0, j)),
            pl.BlockSpec((None, 1, tn), lambda j: (layer, 0, j)),
        ],
        out_specs=pl.BlockSpec((MOD_ROWS, tn), lambda j: (0, j)),
        out_shape=jax.ShapeDtypeStruct((MOD_ROWS, m), F32),
        compiler_params=_params("arbitrary"),
        name="ada_mod",
    )(cv, ada_w, ada_b.reshape(ada_b.shape[0], 1, m))


class Rows:
    def __init__(self, batch, n_lat, n_ctx):
        self.batch, self.n_lat, self.n_ctx = batch, n_lat, n_ctx
        self.lat_rows = batch * n_lat
        self.all_rows = self.lat_rows + batch * n_ctx

    def mod_row(self, tile_rows):
        lat_tiles = self.lat_rows // tile_rows
        per_batch = self.n_lat // tile_rows
        batch = self.batch

        def f(i):
            return jnp.where(i < lat_tiles, i // per_batch, batch)

        return f


def _mod_spec(mod_row, k, d):
    return pl.BlockSpec((1, 1, d), lambda i: (mod_row(i), 0, k))


def _rms(x, g):
    return x * lax.rsqrt(jnp.mean(x * x, axis=-1, keepdims=True) + EPS) * g


def _two_source_specs(block, head_tiles, tile_of):
    def head(*ids):
        t, c = tile_of(*ids)
        return (jnp.minimum(t, head_tiles - 1), c)

    def tail(*ids):
        t, c = tile_of(*ids)
        return (jnp.maximum(t - head_tiles, 0), c)

    return pl.BlockSpec(block, head), pl.BlockSpec(block, tail)


def _norm_mod_kernel(xh_ref, xt_ref, g_ref, sh_ref, sc_ref, o_ref, *, head_tiles, t0):
    def emit(x_ref):
        y = _rms(x_ref[...], g_ref[...])
        o_ref[...] = (y * (1.0 + sc_ref[0]) + sh_ref[0]).astype(o_ref.dtype)

    in_head = pl.program_id(0) + t0 < head_tiles
    pl.when(in_head)(lambda: emit(xh_ref))
    pl.when(jnp.logical_not(in_head))(lambda: emit(xt_ref))


def norm_mod(x_head, x_tail, g, layer, mod3, rows, k_shift, k_scale, row0, nrows, out_dtype):
    d = x_head.shape[1]
    tr = 256
    mod_row = rows.mod_row(tr)
    t0 = row0 // tr
    head_tiles = x_head.shape[0] // tr
    head_spec, tail_spec = _two_source_specs((tr, d), head_tiles, lambda i: (i + t0, 0))
    return pl.pallas_call(
        functools.partial(_norm_mod_kernel, head_tiles=head_tiles, t0=t0),
        grid=(nrows // tr,),
        in_specs=[
            head_spec,
            tail_spec,
            pl.BlockSpec((None, 1, d), lambda i: (layer, 0, 0)),
            pl.BlockSpec((1, 1, d), lambda i: (mod_row(i + t0), 0, k_shift)),
            pl.BlockSpec((1, 1, d), lambda i: (mod_row(i + t0), 0, k_scale)),
        ],
        out_specs=pl.BlockSpec((tr, d), lambda i: (i, 0)),
        out_shape=jax.ShapeDtypeStruct((nrows, d), out_dtype),
        compiler_params=_params("arbitrary"),
        name="norm_mod",
    )(x_head, x_tail, g.reshape(g.shape[0], 1, d), mod3, mod3)


WEIGHT_TILE_BUFFERS = pl.Buffered(1)
MATMUL_ROW_TILES = 8

def _mm_kernel(a_ref, w_ref, o_ref, wb_ref, *, scaled_tiles, scale):
    @pl.when(pl.program_id(1) == 0)
    def _():
        _cast_weight(w_ref, wb_ref)

    acc = jnp.dot(a_ref[...], wb_ref[...], preferred_element_type=F32)
    if scaled_tiles:
        acc = acc * jnp.where(pl.program_id(0) < scaled_tiles, scale, 1.0)
    o_ref[...] = acc.astype(o_ref.dtype)


def matmul(a, w, layer, *, row0, nrows, col0, ncols, out_dtype, scaled_cols=0, scale=1.0, tm=512, tn=1024):
    k = a.shape[1]
    if nrows % (MATMUL_ROW_TILES * 16) == 0 and nrows // MATMUL_ROW_TILES > tm:
        tm = nrows // MATMUL_ROW_TILES
    tm, tn = min(tm, nrows), min(tn, ncols)
    r0, c0 = row0 // tm, col0 // tn
    assert scaled_cols % tn == 0 and row0 % tm == 0
    return pl.pallas_call(
        functools.partial(_mm_kernel, scaled_tiles=scaled_cols // tn, scale=scale),
        grid=(ncols // tn, nrows // tm),
        in_specs=[
            pl.BlockSpec((tm, k), lambda j, i: (i + r0, 0)),
            pl.BlockSpec((None, k, tn), lambda j, i: (layer, 0, j + c0), pipeline_mode=WEIGHT_TILE_BUFFERS),
        ],
        out_specs=pl.BlockSpec((tm, tn), lambda j, i: (i, j)),
        out_shape=jax.ShapeDtypeStruct((nrows, ncols), out_dtype),
        scratch_shapes=[pltpu.VMEM((k, tn), BF16)],
        compiler_params=_params("arbitrary", "arbitrary"),
        name="matmul",
    )(a, w)


def _merge_kernel(oa_ref, ob_ref, wa_ref, wb_ref, ga_ref, gb_ref, y_ref, wab_ref, wbb_ref):
    @pl.when(pl.program_id(1) == 0)
    def _():
        _cast_weight(wa_ref, wab_ref)
        _cast_weight(wb_ref, wbb_ref)

    ya = jnp.dot(oa_ref[...], wab_ref[...], preferred_element_type=F32)
    yb = jnp.dot(ob_ref[...], wbb_ref[...], preferred_element_type=F32)
    y = jax.nn.sigmoid(ga_ref[...].astype(F32)) * ya + jax.nn.sigmoid(gb_ref[...].astype(F32)) * yb
    y_ref[...] = y.astype(y_ref.dtype)


def gated_merge(o_a, o_b, w_a, w_b, gates, layer, *, nrows, ga_col0, gb_col0, tm=512, tn=1024):
    ka, kb = o_a.shape[1], o_b.shape[1]
    d = w_a.shape[2]
    tm, tn = min(tm, nrows), min(tn, d)
    ca, cb = ga_col0 // tn, gb_col0 // tn
    return pl.pallas_call(
        _merge_kernel,
        grid=(d // tn, nrows // tm),
        in_specs=[
            pl.BlockSpec((tm, ka), lambda j, i: (i, 0)),
            pl.BlockSpec((tm, kb), lambda j, i: (i, 0)),
            pl.BlockSpec((None, ka, tn), lambda j, i: (layer, 0, j), pipeline_mode=WEIGHT_TILE_BUFFERS),
            pl.BlockSpec((None, kb, tn), lambda j, i: (layer, 0, j), pipeline_mode=WEIGHT_TILE_BUFFERS),
            pl.BlockSpec((tm, tn), lambda j, i: (i, j + ca)),
            pl.BlockSpec((tm, tn), lambda j, i: (i, j + cb)),
        ],
        out_specs=pl.BlockSpec((tm, tn), lambda j, i: (i, j)),
        out_shape=jax.ShapeDtypeStruct((nrows, d), BF16),
        scratch_shapes=[pltpu.VMEM((ka, tn), BF16), pltpu.VMEM((kb, tn), BF16)],
        compiler_params=_params("arbitrary", "arbitrary"),
        name="gated_merge",
    )(o_a, o_b, w_a, w_b, gates, gates)


def _outproj_kernel(y_ref, w_ref, xh_ref, xt_ref, gt_ref, o_ref, wb_ref, *, head_tiles):
    @pl.when(pl.program_id(1) == 0)
    def _():
        _cast_weight(w_ref, wb_ref)

    upd = gt_ref[0] * jnp.dot(y_ref[...], wb_ref[...], preferred_element_type=F32)

    def emit(x_ref):
        o_ref[...] = x_ref[...] + upd

    in_head = pl.program_id(1) < head_tiles
    pl.when(in_head)(lambda: emit(xh_ref))
    pl.when(jnp.logical_not(in_head))(lambda: emit(xt_ref))


def outproj_residual(y, w_out, x_head, x_tail, mod3, layer, rows, k_gate, *, nrows, tm=512, tn=1024):
    k = y.shape[1]
    d = w_out.shape[2]
    tm, tn = min(tm, nrows), min(tn, d)
    mod_row = rows.mod_row(tm)
    gate_col0 = k_gate * (d // tn)
    head_tiles = x_head.shape[0] // tm
    head_spec, tail_spec = _two_source_specs((tm, tn), head_tiles, lambda j, i: (i, j))
    return pl.pallas_call(
        functools.partial(_outproj_kernel, head_tiles=head_tiles),
        grid=(d // tn, nrows // tm),
        in_specs=[
            pl.BlockSpec((tm, k), lambda j, i: (i, 0)),
            pl.BlockSpec((None, k, tn), lambda j, i: (layer, 0, j), pipeline_mode=WEIGHT_TILE_BUFFERS),
            head_spec,
            tail_spec,
            pl.BlockSpec((1, 1, tn), lambda j, i: (mod_row(i), 0, j + gate_col0)),
        ],
        out_specs=pl.BlockSpec((tm, tn), lambda j, i: (i, j)),
        out_shape=jax.ShapeDtypeStruct((nrows, d), F32),
        scratch_shapes=[pltpu.VMEM((k, tn), BF16)],
        compiler_params=_params("arbitrary", "arbitrary"),
        name="outproj_residual",
    )(y, w_out, x_head, x_tail, mod3)


Q_ROWS = 4
BAND_ROWS = WIN_H + Q_ROWS
N_PATTERNS = 3


def build_col_table(rpb):
    cols = jnp.arange(GRID_W)
    col_start = jnp.clip(cols - WIN_W // 2, 0, GRID_W - WIN_W)
    col_ok = (cols[None, :] >= col_start[:, None]) & (cols[None, :] < col_start[:, None] + WIN_W)
    col_off = jnp.clip(cols[None, :] - cols[:, None] + (WIN_W - 1), 0, 2 * WIN_W - 2)
    return jnp.where(col_ok[None, None], rpb[:, :, col_off], MASK_BIAS).astype(F32)


def _band_patterns(grid_rows):
    half = WIN_H // 2
    groups = grid_rows // Q_ROWS
    assert grid_rows % Q_ROWS == 0 and grid_rows >= BAND_ROWS
    patterns = []
    for g in (0, min(1, groups - 1), groups - 1):
        band_start = min(max(g * Q_ROWS - half, 0), grid_rows - BAND_ROWS)
        pattern = []
        for a in range(Q_ROWS):
            q_row = g * Q_ROWS + a
            win_start = min(max(q_row - half, 0), grid_rows - WIN_H)
            pattern.append([band_start + i - q_row + WIN_H - 1 if win_start <= band_start + i < win_start + WIN_H
                            else None for i in range(BAND_ROWS)])
        patterns.append(pattern)
    return patterns


def _softmax_pv(parts):
    m = functools.reduce(jnp.maximum, [jnp.max(s, axis=-1, keepdims=True) for s, _ in parts])
    es = [jnp.exp(s - m) for s, _ in parts]
    denom = functools.reduce(jnp.add, [jnp.sum(e, axis=-1, keepdims=True) for e in es])
    acc = functools.reduce(jnp.add, [jnp.dot(e.astype(BF16), v, preferred_element_type=F32)
                                     for e, (_, v) in zip(es, parts)])
    return acc / denom


def _qk(q, k):
    return lax.dot_general(q, k, (((1,), (1,)), ((), ())), preferred_element_type=F32)


def _natten_kernel(q_ref, k_ref, v_ref, kc_ref, vc_ref, ct_ref, o_ref, bt_ref, *, grid_rows):
    groups = grid_rows // Q_ROWS
    nq, band = Q_ROWS * GRID_W, BAND_ROWS * GRID_W
    kc = kc_ref[...]
    vc = vc_ref[...]

    @pl.when(jnp.logical_and(pl.program_id(0) == 0, pl.program_id(1) == 0))
    def _():
        bt_ref[...] = jnp.full(bt_ref.shape, MASK_BIAS, F32)

    for p, pattern in enumerate(_band_patterns(grid_rows)):
        for a, row in enumerate(pattern):
            for i, off in enumerate(row):
                if off is not None:
                    bt_ref[p, a * GRID_W:(a + 1) * GRID_W, i * GRID_W:(i + 1) * GRID_W] = ct_ref[off]

    def body(g, carry):
        band_start = jnp.clip(g * Q_ROWS - WIN_H // 2, 0, grid_rows - BAND_ROWS)
        pattern = jnp.where(g == 0, 0, jnp.where(g == groups - 1, 2, 1))
        q0 = pl.multiple_of(g * nq, nq)
        k0 = pl.multiple_of(band_start * GRID_W, GRID_W)
        q = q_ref[pl.ds(q0, nq), :]
        kb = k_ref[pl.ds(k0, band), :]
        vb = v_ref[pl.ds(k0, band), :]
        s_loc = _qk(q, kb) + bt_ref[pattern]
        s_ctx = _qk(q, kc)
        o_ref[pl.ds(q0, nq), :] = _softmax_pv([(s_loc, vb), (s_ctx, vc)]).astype(o_ref.dtype)
        return carry

    lax.fori_loop(0, groups, body, 0, unroll=2)


def neighbourhood_attention(qkv, kvc, col_table, *, batch, n_lat, n_ctx, kc_row0, kc_col0, vc_col0):
    dh, h = NA_HEAD_DIM, NA_HEADS
    grid_rows = n_lat // GRID_W
    rb0, ck, cv = kc_row0 // n_ctx, kc_col0 // dh, vc_col0 // dh
    return pl.pallas_call(
        functools.partial(_natten_kernel, grid_rows=grid_rows),
        grid=(batch, h),
        in_specs=[
            pl.BlockSpec((n_lat, dh), lambda b, hh: (b, hh)),
            pl.BlockSpec((n_lat, dh), lambda b, hh: (b, h + hh)),
            pl.BlockSpec((n_lat, dh), lambda b, hh: (b, 2 * h + hh)),
            pl.BlockSpec((n_ctx, dh), lambda b, hh: (rb0 + b, ck + hh)),
            pl.BlockSpec((n_ctx, dh), lambda b, hh: (rb0 + b, cv + hh)),
            pl.BlockSpec((None, 2 * WIN_H - 1, GRID_W, GRID_W), lambda b, hh: (hh, 0, 0, 0)),
        ],
        out_specs=pl.BlockSpec((n_lat, dh), lambda b, hh: (b, hh)),
        out_shape=jax.ShapeDtypeStruct((batch * n_lat, NA_WIDTH), BF16),
        scratch_shapes=[pltpu.VMEM((N_PATTERNS, Q_ROWS * GRID_W, BAND_ROWS * GRID_W), F32)],
        compiler_params=_params("arbitrary", "arbitrary"),
        name="neighbourhood_attention",
    )(qkv, qkv, qkv, kvc, kvc, col_table)


def _ctx_attn_kernel(q_ref, k_ref, v_ref, o_ref):
    o_ref[...] = _softmax_pv([(_qk(q_ref[...], k_ref[...]), v_ref[...])]).astype(o_ref.dtype)


def context_attention(qkv, *, batch, n_ctx, row0):
    dh, h = NA_HEAD_DIM, NA_HEADS
    rb0 = row0 // n_ctx
    return pl.pallas_call(
        _ctx_attn_kernel,
        grid=(batch, h),
        in_specs=[
            pl.BlockSpec((n_ctx, dh), lambda b, hh: (rb0 + b, hh)),
            pl.BlockSpec((n_ctx, dh), lambda b, hh: (rb0 + b, h + hh)),
            pl.BlockSpec((n_ctx, dh), lambda b, hh: (rb0 + b, 2 * h + hh)),
        ],
        out_specs=pl.BlockSpec((n_ctx, dh), lambda b, hh: (b, hh)),
        out_shape=jax.ShapeDtypeStruct((batch * n_ctx, NA_WIDTH), BF16),
        compiler_params=_params("arbitrary", "arbitrary"),
        name="context_attention",
    )(qkv, qkv, qkv)


def _gmlp_kernel(u_ref, gv_ref, ng_ref, ws_ref, bst_ref, o_ref):
    gd = GMLP_WIDTH // GMLP_GROUPS
    for c in range(u_ref.shape[0] // CHUNK):
        rows = slice(c * CHUNK, (c + 1) * CHUNK)
        v = jax.nn.gelu(gv_ref[rows, :].astype(F32))
        vc = v - jnp.mean(v, axis=-1, keepdims=True)
        y = vc * lax.rsqrt(jnp.mean(vc * vc, axis=-1, keepdims=True) + EPS) * ng_ref[...]
        yb = y.astype(BF16)
        for g in range(GMLP_GROUPS):
            sl = slice(g * gd, (g + 1) * gd)
            mixed = jnp.dot(ws_ref[g].astype(BF16), yb[:, sl], preferred_element_type=F32) + bst_ref[:, g:g + 1]
            o_ref[rows, sl] = (jax.nn.gelu(u_ref[rows, sl].astype(F32)) * mixed).astype(o_ref.dtype)


def spatial_gating(p, norm_g, ws, bs, layer, *, nrows, u_col0, gv_col0):
    w = GMLP_WIDTH
    bst = jnp.transpose(bs[layer])
    tr = 2 * CHUNK
    return pl.pallas_call(
        _gmlp_kernel,
        grid=(nrows // tr,),
        in_specs=[
            pl.BlockSpec((tr, w), lambda i: (i, u_col0 // w)),
            pl.BlockSpec((tr, w), lambda i: (i, gv_col0 // w)),
            pl.BlockSpec((None, 1, w), lambda i: (layer, 0, 0)),
            pl.BlockSpec((None, GMLP_GROUPS, CHUNK, CHUNK), lambda i: (layer, 0, 0, 0)),
            pl.BlockSpec((CHUNK, GMLP_GROUPS), lambda i: (0, 0)),
        ],
        out_specs=pl.BlockSpec((tr, w), lambda i: (i, 0)),
        out_shape=jax.ShapeDtypeStruct((nrows, w), BF16),
        compiler_params=_params("arbitrary"),
        name="spatial_gating",
    )(p, p, norm_g.reshape(norm_g.shape[0], 1, w), ws, bst)


def _split_bf16(x):
    hi = x.astype(BF16)
    return hi, (x - hi.astype(F32)).astype(BF16)


def _norm_router_kernel(x_ref, g_ref, sh_ref, sc_ref, rw_ref, h_ref, lg_ref):
    h = _rms(x_ref[...], g_ref[...]) * (1.0 + sc_ref[0]) + sh_ref[0]
    h_ref[...] = h
    hh, hl = _split_bf16(h)
    wh, wl = _split_bf16(rw_ref[...])
    lg = (jnp.dot(hh, wh, preferred_element_type=F32) + jnp.dot(hl, wh, preferred_element_type=F32)
          + jnp.dot(hh, wl, preferred_element_type=F32))
    lg_ref[...] = jnp.transpose(lg)[:N_EXPERTS, :]


def norm_mod_router(x, g, layer, mod3, rows, k_shift, k_scale, router_w, nrows):
    d = x.shape[1]
    tr = 256
    mod_row = rows.mod_row(tr)
    rw = jnp.pad(router_w[layer], ((0, 0), (0, LANES - N_EXPERTS)))
    return pl.pallas_call(
        _norm_router_kernel,
        grid=(nrows // tr,),
        in_specs=[
            pl.BlockSpec((tr, d), lambda i: (i, 0)),
            pl.BlockSpec((None, 1, d), lambda i: (layer, 0, 0)),
            pl.BlockSpec((1, 1, d), lambda i: (mod_row(i), 0, k_shift)),
            pl.BlockSpec((1, 1, d), lambda i: (mod_row(i), 0, k_scale)),
            pl.BlockSpec((d, LANES), lambda i: (0, 0)),
        ],
        out_specs=[pl.BlockSpec((tr, d), lambda i: (i, 0)), pl.BlockSpec((N_EXPERTS, tr), lambda i: (0, i))],
        out_shape=[jax.ShapeDtypeStruct((nrows, d), F32), jax.ShapeDtypeStruct((N_EXPERTS, nrows), F32)],
        compiler_params=_params("arbitrary"),
        name="norm_mod_router",
    )(x, g.reshape(g.shape[0], 1, d), mod3, mod3, rw)


def _cumsum_lanes(x):
    r, n = x.shape
    blk = min(n, LANES)
    tri = (lax.broadcasted_iota(I32, (blk, blk), 0) <= lax.broadcasted_iota(I32, (blk, blk), 1)).astype(BF16)
    off = jnp.zeros((r, 1), F32)
    out = []
    for j in range(n // blk):
        cs = jnp.dot(x[:, j * blk:(j + 1) * blk].astype(BF16), tri, preferred_element_type=F32) + off
        out.append(cs)
        off = cs[:, blk - 1:blk]
    return jnp.concatenate(out, axis=1) if len(out) > 1 else out[0]


def _route_kernel(lg_ref, idx_ref, dest_ref, gate_ref, start_ref, cnt_ref, slot_s, aff_s, *, cap):
    e_n, n = lg_ref.shape
    lg = lg_ref[...]
    ex = jnp.exp(lg - jnp.max(lg, axis=0, keepdims=True))
    aff = ex / jnp.sum(ex, axis=0, keepdims=True)
    bits = lax.bitcast_convert_type(aff, I32)

    thr = jnp.zeros((e_n, 1), I32)
    for bit in range(30, -1, -1):
        cand = thr | (1 << bit)
        n_ge = jnp.sum((bits >= cand).astype(F32), axis=1, keepdims=True)
        thr = jnp.where(n_ge >= cap, cand, thr)
    above = bits > thr
    tied = (bits == thr).astype(F32)
    need = cap - jnp.sum(above.astype(F32), axis=1, keepdims=True)
    tied_before = _cumsum_lanes(tied) - tied
    sel = jnp.where(above, 1.0, jnp.where(tied_before < need, tied, 0.0))

    slot = _cumsum_lanes(sel) - sel
    per_tok = jnp.sum(sel, axis=0, keepdims=True)
    per_tok8 = jnp.broadcast_to(per_tok, (8, n))
    start = (_cumsum_lanes(per_tok8) - per_tok8)[0:1]
    start_ref[...] = start.astype(I32)
    cnt_ref[...] = per_tok.astype(I32)
    slot_s[...] = jnp.where(sel > 0.0, slot, -1.0)
    aff_s[...] = aff

    tok = lax.broadcasted_iota(I32, (1, n), 1).astype(F32)
    cc = min(cap, LANES)

    def body(e, chosen_by_earlier):
        srow = slot_s[pl.ds(e, 1), :]
        arow = aff_s[pl.ds(e, 1), :]
        drow = start + chosen_by_earlier
        for c0 in range(0, cap, cc):
            want = (lax.broadcasted_iota(I32, (cc, 1), 0) + c0).astype(F32)
            hit = srow == want
            pick = lambda row: jnp.sum(jnp.where(hit, row, 0.0), axis=1, keepdims=True)
            idx_ref[e, pl.ds(c0, cc), :] = pick(tok).astype(I32)
            dest_ref[e, pl.ds(c0, cc), :] = pick(drow).astype(I32)
            gate_ref[e, pl.ds(c0, cc), :] = pick(arow)
        return chosen_by_earlier + (srow >= 0.0).astype(F32)

    lax.fori_loop(0, e_n, body, jnp.zeros((1, n), F32))


def expert_choice_route(logits_t, *, batch, n, col0):
    e_n = logits_t.shape[0]
    cap = CAPACITY_FACTOR * n // e_n
    cb0 = col0 // n
    per_slot = lambda dt: jax.ShapeDtypeStruct((batch, e_n, cap, 1), dt)
    per_tok = jax.ShapeDtypeStruct((batch, 1, n), I32)
    slot_spec = pl.BlockSpec((None, e_n, cap, 1), lambda b: (b, 0, 0, 0))
    tok_spec = pl.BlockSpec((None, 1, n), lambda b: (b, 0, 0))
    return pl.pallas_call(
        functools.partial(_route_kernel, cap=cap),
        grid=(batch,),
        in_specs=[pl.BlockSpec((e_n, n), lambda b: (0, cb0 + b))],
        out_specs=[slot_spec, slot_spec, slot_spec, tok_spec, tok_spec],
        out_shape=[per_slot(I32), per_slot(I32), per_slot(F32), per_tok, per_tok],
        scratch_shapes=[pltpu.VMEM((e_n, n), F32), pltpu.VMEM((e_n, n), F32)],
        compiler_params=_params("arbitrary"),
        name="expert_choice_route",
    )(logits_t)


ROWS_PER_STEP = 256
ROW_COPY_UNROLL = 8


def _row_copies(n, make_copy):
    def start(i, carry):
        make_copy(i).start()
        return carry

    def wait(i, carry):
        make_copy(i).wait()
        return carry

    lax.fori_loop(0, n, start, 0, unroll=ROW_COPY_UNROLL)
    lax.fori_loop(0, n, wait, 0, unroll=ROW_COPY_UNROLL)


def _gather_kernel(idx_ref, src_ref, o_ref, stage_ref, sem):
    rows = o_ref.shape[0]
    base = pl.program_id(0) * rows
    _row_copies(rows, lambda i: pltpu.make_async_copy(
        src_ref.at[pl.ds(idx_ref[base + i], 1), :], stage_ref.at[pl.ds(i, 1), :], sem))
    o_ref[...] = stage_ref[...].astype(o_ref.dtype)


def gather_rows(src, idx, out_dtype):
    n, d = idx.shape[0], src.shape[1]
    rows = min(n, ROWS_PER_STEP)
    return pl.pallas_call(
        _gather_kernel,
        grid_spec=pltpu.PrefetchScalarGridSpec(
            num_scalar_prefetch=1,
            grid=(n // rows,),
            in_specs=[pl.BlockSpec(memory_space=pl.ANY)],
            out_specs=pl.BlockSpec((rows, d), lambda i, idx_ref: (i, 0)),
            scratch_shapes=[pltpu.VMEM((rows, d), src.dtype), pltpu.SemaphoreType.DMA(())],
        ),
        out_shape=jax.ShapeDtypeStruct((n, d), out_dtype),
        compiler_params=_params("arbitrary"),
        name="gather_rows",
    )(idx, src)


def _scatter_kernel(idx_ref, x_ref, o_ref, stage_ref, sem):
    rows = x_ref.shape[0]
    base = pl.program_id(0) * rows
    stage_ref[...] = x_ref[...].astype(stage_ref.dtype)
    _row_copies(rows, lambda i: pltpu.make_async_copy(
        stage_ref.at[pl.ds(i, 1), :], o_ref.at[pl.ds(idx_ref[base + i], 1), :], sem))


def scatter_rows(x, idx, out_dtype):
    n, d = x.shape
    rows = min(n, ROWS_PER_STEP)
    return pl.pallas_call(
        _scatter_kernel,
        grid_spec=pltpu.PrefetchScalarGridSpec(
            num_scalar_prefetch=1,
            grid=(n // rows,),
            in_specs=[pl.BlockSpec((rows, d), lambda i, idx_ref: (i, 0))],
            out_specs=pl.BlockSpec(memory_space=pl.ANY),
            scratch_shapes=[pltpu.VMEM((rows, d), out_dtype), pltpu.SemaphoreType.DMA(())],
        ),
        out_shape=jax.ShapeDtypeStruct((n, d), out_dtype),
        compiler_params=_params("arbitrary"),
        name="scatter_rows",
    )(idx, x)


def _ffn_kernel(xg_ref, w1_ref, w3_ref, w2_ref, g_ref, y_ref, hid_s, *, f1, tf):
    s = pl.program_id(1)

    @pl.when(s < f1)
    def _():
        xs = xg_ref[...]
        h1 = jnp.dot(xs, w1_ref[...].astype(BF16), preferred_element_type=F32)
        h3 = jnp.dot(xs, w3_ref[...].astype(BF16), preferred_element_type=F32)
        hid_s[s] = (jax.nn.silu(h1) * h3).astype(BF16)

    @pl.when(s >= f1)
    def _():
        acc = jnp.dot(hid_s[0], w2_ref[0:tf, :].astype(BF16), preferred_element_type=F32)
        for f in range(1, f1):
            acc += jnp.dot(hid_s[f], w2_ref[f * tf:(f + 1) * tf, :].astype(BF16), preferred_element_type=F32)
        y_ref[...] = (acc * g_ref[...]).astype(y_ref.dtype)


def expert_ffn(xg, w1, w3, w2, gate, layer):
    e_n, m, d = xg.shape
    ff = w1.shape[3]
    tf, tn = min(ff, 256), min(d, 512)
    f1, f2 = ff // tf, d // tn
    out_step = lambda s: jnp.maximum(s - f1, 0)
    next_e = lambda e: jnp.minimum(e + 1, e_n - 1)

    def rows_map(e, s):
        return (jnp.where(s < f1, e, next_e(e)), 0, 0)

    def up_map(switch_step):
        def index(e, s):
            switched = s >= f1 + switch_step
            return (layer, jnp.where(switched, next_e(e), e), 0, jnp.where(switched, 0, jnp.minimum(s, f1 - 1)))
        return index

    def down_map(e, s):
        hold = jnp.logical_and(s == 0, e > 0)
        return (layer, jnp.where(hold, e - 1, e), 0, jnp.where(hold, f2 - 1, out_step(s)))

    return pl.pallas_call(
        functools.partial(_ffn_kernel, f1=f1, tf=tf),
        grid=(e_n, f1 + f2),
        in_specs=[
            pl.BlockSpec((None, m, d), rows_map),
            pl.BlockSpec((None, None, d, tf), up_map(f2 // 3)),
            pl.BlockSpec((None, None, d, tf), up_map(2 * f2 // 3)),
            pl.BlockSpec((None, None, ff, tn), down_map),
            pl.BlockSpec((None, m, 1), lambda e, s: (e, 0, 0)),
        ],
        out_specs=pl.BlockSpec((None, m, tn), lambda e, s: (e, 0, out_step(s))),
        out_shape=jax.ShapeDtypeStruct((e_n, m, d), BF16),
        scratch_shapes=[pltpu.VMEM((f1, m, tf), BF16)],
        compiler_params=_params("arbitrary", "arbitrary"),
        name="expert_ffn",
    )(xg, w1, w3, w2, gate)


ITEM_ADD, ITEM_FIRST, ITEM_LAST = 1, 2, 4


def _combine_kernel(tile_ref, chunk_ref, flag_ref, yg_ref, st_ref, ct_ref, x_ref, gt_ref, ng_ref, o_ref, acc_s, *,
                    ch, normalize):
    del tile_ref
    item = pl.program_id(0) * pl.num_programs(1) + pl.program_id(1)
    flags = flag_ref[item]

    @pl.when((flags & ITEM_FIRST) != 0)
    def _():
        acc_s[...] = jnp.zeros_like(acc_s)

    @pl.when((flags & ITEM_ADD) != 0)
    def _():
        pair = chunk_ref[item] * ch + lax.broadcasted_iota(I32, (ch, 1), 0)
        rel = pair - st_ref[...]
        own_t = jnp.where(jnp.logical_and(rel >= 0, rel < ct_ref[...]), 1.0, 0.0).astype(BF16)
        acc_s[...] += lax.dot_general(own_t, yg_ref[...].astype(BF16), (((0,), (0,)), ((), ())),
                                      preferred_element_type=F32)

    @pl.when((flags & ITEM_LAST) != 0)
    def _():
        out = x_ref[...] + gt_ref[0] * acc_s[...]
        o_ref[...] = _rms(out, ng_ref[...]) if normalize else out


def _combine_work_items(start, *, batch, n, tt, pairs, ch):
    tiles, n_chunks = n // tt, pairs // ch
    n_items = tiles + n_chunks
    first = start.reshape(batch, n)[:, ::tt]
    last = jnp.concatenate([first[:, 1:], jnp.full((batch, 1), pairs, I32)], axis=1)
    c_first = jnp.minimum(first // ch, n_chunks - 1)
    c_num = jnp.where(last > first, (last - 1) // ch - first // ch + 1, 0)
    t_items = jnp.maximum(c_num, 1)
    ends = jnp.cumsum(t_items, axis=1)
    k = jnp.arange(n_items, dtype=I32)[None, :]
    tile = jnp.minimum(jnp.sum((ends[:, None, :] <= k[:, :, None]).astype(I32), axis=-1), tiles - 1)
    at = lambda v: jnp.take_along_axis(v, tile, axis=1)
    j = k - at(ends - t_items)
    real = k < ends[:, -1:]
    chunk = at(c_first) + jnp.minimum(j, jnp.maximum(at(c_num) - 1, 0))
    flags = (ITEM_ADD * (real & (j < at(c_num))) + ITEM_FIRST * (real & (j == 0))
             + ITEM_LAST * (real & (j == at(t_items) - 1)))
    flat = lambda v: v.reshape(-1).astype(I32)
    return n_items, flat(tile), flat(chunk), flat(flags)


def combine_residual(yg, start, cnt, x, mod3, rows, k_gate, norm_g, *, batch, n, row0, pairs, pair_row0, normalize):
    d = x.shape[1]
    tt = min(n, 256)
    ch = min(pairs, 256)
    tiles, n_chunks = n // tt, pairs // ch
    n_items, item_tile, item_chunk, item_flags = _combine_work_items(start, batch=batch, n=n, tt=tt, pairs=pairs,
                                                                     ch=ch)
    mod_row = rows.mod_row(tt)
    rt0, chunk0 = row0 // tt, pair_row0 // ch

    def tile_of(b, k, tile_ref):
        return tile_ref[b * n_items + k]

    yg_map = lambda b, k, tile_ref, chunk_ref, flag_ref: (chunk0 + b * n_chunks + chunk_ref[b * n_items + k], 0)
    row_map = lambda b, k, tile_ref, chunk_ref, flag_ref: (rt0 + b * tiles + tile_of(b, k, tile_ref), 0)
    tok_map = lambda b, k, tile_ref, chunk_ref, flag_ref: (b, 0, tile_of(b, k, tile_ref))
    gate_map = lambda b, k, tile_ref, chunk_ref, flag_ref: (
        mod_row(rt0 + b * tiles + tile_of(b, k, tile_ref)), 0, k_gate)
    return pl.pallas_call(
        functools.partial(_combine_kernel, ch=ch, normalize=normalize),
        grid_spec=pltpu.PrefetchScalarGridSpec(
            num_scalar_prefetch=3,
            grid=(batch, n_items),
            in_specs=[
                pl.BlockSpec((ch, d), yg_map),
                pl.BlockSpec((None, 1, tt), tok_map),
                pl.BlockSpec((None, 1, tt), tok_map),
                pl.BlockSpec((tt, d), row_map),
                pl.BlockSpec((1, 1, d), gate_map),
                pl.BlockSpec((1, d), lambda b, k, tile_ref, chunk_ref, flag_ref: (0, 0)),
            ],
            out_specs=pl.BlockSpec((tt, d), row_map),
            scratch_shapes=[pltpu.VMEM((tt, d), F32)],
        ),
        out_shape=jax.ShapeDtypeStruct(x.shape, F32),
        input_output_aliases={6: 0},
        compiler_params=_params("arbitrary", "arbitrary"),
        name="combine_residual",
    )(item_tile, item_chunk, item_flags, yg, start, cnt, x, mod3, norm_g.reshape(1, d))


def expert_choice_ffn_residual(x, h, logits_t, mod3, rows, k_gate, w1, w3, w2, layer, norm_g, *, batch, sets,
                               normalize):
    e_n = logits_t.shape[0]
    d = x.shape[1]
    boff = jnp.arange(batch, dtype=I32)[:, None, None]
    routed, src_rows, pair_rows, gate_cols, pair_row0 = [], [], [], [], 0
    for n, row0 in sets:
        idx, dest, gate, start, cnt = expert_choice_route(logits_t, batch=batch, n=n, col0=row0)
        cap = idx.shape[2]
        pairs = e_n * cap
        src_rows.append(jnp.transpose(idx[..., 0] + row0 + boff * n, (1, 0, 2)).reshape(e_n, batch * cap))
        pair_rows.append(jnp.transpose(dest[..., 0] + pair_row0 + boff * pairs, (1, 0, 2)).reshape(e_n, batch * cap))
        gate_cols.append(jnp.transpose(gate, (1, 0, 2, 3)).reshape(e_n, batch * cap, 1))
        routed.append((start, cnt, n, row0, pairs, pair_row0))
        pair_row0 += batch * pairs
    src_rows = jnp.concatenate(src_rows, axis=1)
    m = src_rows.shape[1]
    xg = gather_rows(h, src_rows.reshape(-1), BF16).reshape(e_n, m, d)
    y = expert_ffn(xg, w1, w3, w2, jnp.concatenate(gate_cols, axis=1), layer).reshape(e_n * m, d)
    yg = scatter_rows(y, jnp.concatenate(pair_rows, axis=1).reshape(-1), F32)
    for start, cnt, n, row0, pairs, pair_row0 in routed:
        x = combine_residual(yg, start, cnt, x, mod3, rows, k_gate, norm_g, batch=batch, n=n, row0=row0,
                             pairs=pairs, pair_row0=pair_row0, normalize=normalize)
    return x


def kernel(x, c, ctx, c_ctx, ada_w, ada_b, norm1_g, norm2_g, w_in, na_rpb, gmlp_norm_g, gmlp_ws, gmlp_bs,
           w_branch_a, w_branch_b, w_out, router_w, exp_w1, exp_w3, exp_w2, final_norm_g):
    batch, n_lat, d = x.shape
    n_ctx = ctx.shape[1]
    depth = ada_w.shape[0]
    rows = Rows(batch, n_lat, n_ctx)
    lat_rows, all_rows = rows.lat_rows, rows.all_rows
    off_k, off_v, off_u = NA_WIDTH, 2 * NA_WIDTH, 3 * NA_WIDTH
    off_gv = off_u + GMLP_WIDTH
    off_ga = off_gv + GMLP_WIDTH
    off_gb = off_ga + d
    in_cols = off_gb + d

    x_head, x_tail = x.reshape(lat_rows, d), ctx.reshape(batch * n_ctx, d)
    cv = jnp.zeros((MOD_ROWS, d), F32).at[:batch].set(c).at[batch].set(c_ctx)

    for layer in range(depth):
        last = layer == depth - 1
        nrows = lat_rows if last else all_rows
        mod3 = ada_mod(cv, ada_w, ada_b, layer).reshape(MOD_ROWS, 1, N_MOD * d)

        h = norm_mod(x_head, x_tail, norm1_g, layer, mod3, rows, 0, 1, 0, nrows, BF16)
        qkv = rest = matmul(h, w_in, layer, row0=0, nrows=nrows, col0=0, ncols=in_cols, out_dtype=BF16,
                            scaled_cols=NA_WIDTH, scale=NA_HEAD_DIM ** -0.5)
        bias_table = build_col_table(na_rpb[layer])
        if last:
            h_ctx = norm_mod(x_head, x_tail, norm1_g, layer, mod3, rows, 0, 1, lat_rows, batch * n_ctx, BF16)
            kvc = matmul(h_ctx, w_in, layer, row0=0, nrows=batch * n_ctx, col0=off_k, ncols=2 * NA_WIDTH,
                         out_dtype=BF16)
            o_a = neighbourhood_attention(qkv, kvc, bias_table, batch=batch, n_lat=n_lat, n_ctx=n_ctx,
                                          kc_row0=0, kc_col0=0, vc_col0=NA_WIDTH)
        else:
            o_lat = neighbourhood_attention(qkv, qkv, bias_table, batch=batch, n_lat=n_lat, n_ctx=n_ctx,
                                            kc_row0=lat_rows, kc_col0=off_k, vc_col0=off_v)
            o_ctx = context_attention(qkv, batch=batch, n_ctx=n_ctx, row0=lat_rows)
            o_a = jnp.concatenate([o_lat, o_ctx], axis=0)
        o_b = spatial_gating(rest, gmlp_norm_g, gmlp_ws, gmlp_bs, layer, nrows=nrows, u_col0=off_u, gv_col0=off_gv)
        y = gated_merge(o_a, o_b, w_branch_a, w_branch_b, rest, layer, nrows=nrows, ga_col0=off_ga, gb_col0=off_gb)
        stream = outproj_residual(y, w_out, x_head, x_tail, mod3, layer, rows, 2, nrows=nrows)

        h2, logits_t = norm_mod_router(stream, norm2_g, layer, mod3, rows, 3, 4, router_w, nrows)
        sets = [(n_lat, 0)] if last else [(n_lat, 0), (n_ctx, lat_rows)]
        stream = expert_choice_ffn_residual(stream, h2, logits_t, mod3, rows, 5, exp_w1, exp_w3, exp_w2, layer,
                                            final_norm_g, batch=batch, sets=sets, normalize=last)
        x_head = x_tail = stream

    return stream.reshape(batch, n_lat, d)
```

```python
import functools

import jax
import jax.numpy as jnp
from jax import lax
from jax.experimental import pallas as pl
from jax.experimental.pallas import tpu as pltpu

F32, BF16, I32 = jnp.float32, jnp.bfloat16, jnp.int32

GRID_W = 64
NA_HEADS = 16
NA_HEAD_DIM = 128
NA_WIDTH = NA_HEADS * NA_HEAD_DIM
WIN_H = 8
WIN_W = 16
GMLP_GROUPS = 16
GMLP_WIDTH = 2048
CHUNK = 128
N_EXPERTS = 16
CAPACITY_FACTOR = 2
N_MOD = 6
EPS = 1e-6
MASK_BIAS = -1e30

V7X_VMEM_LIMIT_BYTES = 56 * 1024 * 1024
LANES = 128
MOD_ROWS = 8


def _params(*sem):
    return pltpu.CompilerParams(dimension_semantics=sem, vmem_limit_bytes=V7X_VMEM_LIMIT_BYTES)


def _cast_weight(w_ref, wb_ref):
    rows = w_ref.shape[0]
    step = min(rows, 512)

    def body(i, carry):
        r = pl.multiple_of(i * step, step)
        wb_ref[pl.ds(r, step), :] = w_ref[pl.ds(r, step), :].astype(BF16)
        return carry

    lax.fori_loop(0, rows // step, body, 0)


def _ada_kernel(c_ref, w_ref, b_ref, o_ref):
    c = c_ref[...]
    s = c * jax.nn.sigmoid(c)
    o_ref[...] = jnp.dot(s.astype(BF16), w_ref[...].astype(BF16), preferred_element_type=F32) + b_ref[...]


def ada_mod(cv, ada_w, ada_b, layer):
    _, d, m = ada_w.shape
    tn = min(m, 512)
    return pl.pallas_call(
        _ada_kernel,
        grid=(m // tn,),
        in_specs=[
            pl.BlockSpec((MOD_ROWS, d), lambda j: (0, 0)),
            pl.BlockSpec((None, d, tn), lambda j: (layer, 0, j)),
            pl.BlockSpec((None, 1, tn), lambda j: (layer, 0, j)),
        ],
        out_specs=pl.BlockSpec((MOD_ROWS, tn), lambda j: (0, j)),
        out_shape=jax.ShapeDtypeStruct((MOD_ROWS, m), F32),
        compiler_params=_params("arbitrary"),
        name="ada_mod",
    )(cv, ada_w, ada_b.reshape(ada_b.shape[0], 1, m))


class Rows:
    def __init__(self, batch, n_lat, n_ctx):
        self.batch, self.n_lat, self.n_ctx = batch, n_lat, n_ctx
        self.lat_rows = batch * n_lat
        self.all_rows = self.lat_rows + batch * n_ctx

    def mod_row(self, tile_rows):
        lat_tiles = self.lat_rows // tile_rows
        per_batch = self.n_lat // tile_rows
        batch = self.batch

        def f(i):
            return jnp.where(i < lat_tiles, i // per_batch, batch)

        return f


def _mod_spec(mod_row, k, d):
    return pl.BlockSpec((1, 1, d), lambda i: (mod_row(i), 0, k))


def _rms(x, g):
    return x * lax.rsqrt(jnp.mean(x * x, axis=-1, keepdims=True) + EPS) * g


def _two_source_specs(block, head_tiles, tile_of):
    def head(*ids):
        t, c = tile_of(*ids)
        return (jnp.minimum(t, head_tiles - 1), c)

    def tail(*ids):
        t, c = tile_of(*ids)
        return (jnp.maximum(t - head_tiles, 0), c)

    return pl.BlockSpec(block, head), pl.BlockSpec(block, tail)


def _norm_mod_kernel(xh_ref, xt_ref, g_ref, sh_ref, sc_ref, o_ref, *, head_tiles, t0):
    def emit(x_ref):
        y = _rms(x_ref[...], g_ref[...])
        o_ref[...] = (y * (1.0 + sc_ref[0]) + sh_ref[0]).astype(o_ref.dtype)

    in_head = pl.program_id(0) + t0 < head_tiles
    pl.when(in_head)(lambda: emit(xh_ref))
    pl.when(jnp.logical_not(in_head))(lambda: emit(xt_ref))


def norm_mod(x_head, x_tail, g, layer, mod3, rows, k_shift, k_scale, row0, nrows, out_dtype):
    d = x_head.shape[1]
    tr = 256
    mod_row = rows.mod_row(tr)
    t0 = row0 // tr
    head_tiles = x_head.shape[0] // tr
    head_spec, tail_spec = _two_source_specs((tr, d), head_tiles, lambda i: (i + t0, 0))
    return pl.pallas_call(
        functools.partial(_norm_mod_kernel, head_tiles=head_tiles, t0=t0),
        grid=(nrows // tr,),
        in_specs=[
            head_spec,
            tail_spec,
            pl.BlockSpec((None, 1, d), lambda i: (layer, 0, 0)),
            pl.BlockSpec((1, 1, d), lambda i: (mod_row(i + t0), 0, k_shift)),
            pl.BlockSpec((1, 1, d), lambda i: (mod_row(i + t0), 0, k_scale)),
        ],
        out_specs=pl.BlockSpec((tr, d), lambda i: (i, 0)),
        out_shape=jax.ShapeDtypeStruct((nrows, d), out_dtype),
        compiler_params=_params("arbitrary"),
        name="norm_mod",
    )(x_head, x_tail, g.reshape(g.shape[0], 1, d), mod3, mod3)


WEIGHT_TILE_BUFFERS = pl.Buffered(1)
MATMUL_ROW_TILES = 8

def _mm_kernel(a_ref, w_ref, o_ref, wb_ref, *, scaled_tiles, scale):
    @pl.when(pl.program_id(1) == 0)
    def _():
        _cast_weight(w_ref, wb_ref)

    acc = jnp.dot(a_ref[...], wb_ref[...], preferred_element_type=F32)
    if scaled_tiles:
        acc = acc * jnp.where(pl.program_id(0) < scaled_tiles, scale, 1.0)
    o_ref[...] = acc.astype(o_ref.dtype)


def matmul(a, w, layer, *, row0, nrows, col0, ncols, out_dtype, scaled_cols=0, scale=1.0, tm=512, tn=1024):
    k = a.shape[1]
    if nrows % (MATMUL_ROW_TILES * 16) == 0 and nrows // MATMUL_ROW_TILES > tm:
        tm = nrows // MATMUL_ROW_TILES
    tm, tn = min(tm, nrows), min(tn, ncols)
    r0, c0 = row0 // tm, col0 // tn
    assert scaled_cols % tn == 0 and row0 % tm == 0
    return pl.pallas_call(
        functools.partial(_mm_kernel, scaled_tiles=scaled_cols // tn, scale=scale),
        grid=(ncols // tn, nrows // tm),
        in_specs=[
            pl.BlockSpec((tm, k), lambda j, i: (i + r0, 0)),
            pl.BlockSpec((None, k, tn), lambda j, i: (layer, 0, j + c0), pipeline_mode=WEIGHT_TILE_BUFFERS),
        ],
        out_specs=pl.BlockSpec((tm, tn), lambda j, i: (i, j)),
        out_shape=jax.ShapeDtypeStruct((nrows, ncols), out_dtype),
        scratch_shapes=[pltpu.VMEM((k, tn), BF16)],
        compiler_params=_params("arbitrary", "arbitrary"),
        name="matmul",
    )(a, w)


def _merge_kernel(oa_ref, ob_ref, wa_ref, wb_ref, ga_ref, gb_ref, y_ref, wab_ref, wbb_ref):
    @pl.when(pl.program_id(1) == 0)
    def _():
        _cast_weight(wa_ref, wab_ref)
        _cast_weight(wb_ref, wbb_ref)

    ya = jnp.dot(oa_ref[...], wab_ref[...], preferred_element_type=F32)
    yb = jnp.dot(ob_ref[...], wbb_ref[...], preferred_element_type=F32)
    y = jax.nn.sigmoid(ga_ref[...].astype(F32)) * ya + jax.nn.sigmoid(gb_ref[...].astype(F32)) * yb
    y_ref[...] = y.astype(y_ref.dtype)


def gated_merge(o_a, o_b, w_a, w_b, gates, layer, *, nrows, ga_col0, gb_col0, tm=512, tn=1024):
    ka, kb = o_a.shape[1], o_b.shape[1]
    d = w_a.shape[2]
    tm, tn = min(tm, nrows), min(tn, d)
    ca, cb = ga_col0 // tn, gb_col0 // tn
    return pl.pallas_call(
        _merge_kernel,
        grid=(d // tn, nrows // tm),
        in_specs=[
            pl.BlockSpec((tm, ka), lambda j, i: (i, 0)),
            pl.BlockSpec((tm, kb), lambda j, i: (i, 0)),
            pl.BlockSpec((None, ka, tn), lambda j, i: (layer, 0, j), pipeline_mode=WEIGHT_TILE_BUFFERS),
            pl.BlockSpec((None, kb, tn), lambda j, i: (layer, 0, j), pipeline_mode=WEIGHT_TILE_BUFFERS),
            pl.BlockSpec((tm, tn), lambda j, i: (i, j + ca)),
            pl.BlockSpec((tm, tn), lambda j, i: (i, j + cb)),
        ],
        out_specs=pl.BlockSpec((tm, tn), lambda j, i: (i, j)),
        out_shape=jax.ShapeDtypeStruct((nrows, d), BF16),
        scratch_shapes=[pltpu.VMEM((ka, tn), BF16), pltpu.VMEM((kb, tn), BF16)],
        compiler_params=_params("arbitrary", "arbitrary"),
        name="gated_merge",
    )(o_a, o_b, w_a, w_b, gates, gates)


def _outproj_kernel(y_ref, w_ref, xh_ref, xt_ref, gt_ref, o_ref, wb_ref, *, head_tiles):
    @pl.when(pl.program_id(1) == 0)
    def _():
        _cast_weight(w_ref, wb_ref)

    upd = gt_ref[0] * jnp.dot(y_ref[...], wb_ref[...], preferred_element_type=F32)

    def emit(x_ref):
        o_ref[...] = x_ref[...] + upd

    in_head = pl.program_id(1) < head_tiles
    pl.when(in_head)(lambda: emit(xh_ref))
    pl.when(jnp.logical_not(in_head))(lambda: emit(xt_ref))


def outproj_residual(y, w_out, x_head, x_tail, mod3, layer, rows, k_gate, *, nrows, tm=512, tn=1024):
    k = y.shape[1]
    d = w_out.shape[2]
    tm, tn = min(tm, nrows), min(tn, d)
    mod_row = rows.mod_row(tm)
    gate_col0 = k_gate * (d // tn)
    head_tiles = x_head.shape[0] // tm
    head_spec, tail_spec = _two_source_specs((tm, tn), head_tiles, lambda j, i: (i, j))
    return pl.pallas_call(
        functools.partial(_outproj_kernel, head_tiles=head_tiles),
        grid=(d // tn, nrows // tm),
        in_specs=[
            pl.BlockSpec((tm, k), lambda j, i: (i, 0)),
            pl.BlockSpec((None, k, tn), lambda j, i: (layer, 0, j), pipeline_mode=WEIGHT_TILE_BUFFERS),
            head_spec,
            tail_spec,
            pl.BlockSpec((1, 1, tn), lambda j, i: (mod_row(i), 0, j + gate_col0)),
        ],
        out_specs=pl.BlockSpec((tm, tn), lambda j, i: (i, j)),
        out_shape=jax.ShapeDtypeStruct((nrows, d), F32),
        scratch_shapes=[pltpu.VMEM((k, tn), BF16)],
        compiler_params=_params("arbitrary", "arbitrary"),
        name="outproj_residual",
    )(y, w_out, x_head, x_tail, mod3)


Q_ROWS = 4
BAND_ROWS = WIN_H + Q_ROWS
N_PATTERNS = 3


def build_col_table(rpb):
    cols = jnp.arange(GRID_W)
    col_start = jnp.clip(cols - WIN_W // 2, 0, GRID_W - WIN_W)
    col_ok = (cols[None, :] >= col_start[:, None]) & (cols[None, :] < col_start[:, None] + WIN_W)
    col_off = jnp.clip(cols[None, :] - cols[:, None] + (WIN_W - 1), 0, 2 * WIN_W - 2)
    return jnp.where(col_ok[None, None], rpb[:, :, col_off], MASK_BIAS).astype(F32)


def _band_patterns(grid_rows):
    half = WIN_H // 2
    groups = grid_rows // Q_ROWS
    assert grid_rows % Q_ROWS == 0 and grid_rows >= BAND_ROWS
    patterns = []
    for g in (0, min(1, groups - 1), groups - 1):
        band_start = min(max(g * Q_ROWS - half, 0), grid_rows - BAND_ROWS)
        pattern = []
        for a in range(Q_ROWS):
            q_row = g * Q_ROWS + a
            win_start = min(max(q_row - half, 0), grid_rows - WIN_H)
            pattern.append([band_start + i - q_row + WIN_H - 1 if win_start <= band_start + i < win_start + WIN_H
                            else None for i in range(BAND_ROWS)])
        patterns.append(pattern)
    return patterns


def _softmax_pv(parts):
    m = functools.reduce(jnp.maximum, [jnp.max(s, axis=-1, keepdims=True) for s, _ in parts])
    es = [jnp.exp(s - m) for s, _ in parts]
    denom = functools.reduce(jnp.add, [jnp.sum(e, axis=-1, keepdims=True) for e in es])
    acc = functools.reduce(jnp.add, [jnp.dot(e.astype(BF16), v, preferred_element_type=F32)
                                     for e, (_, v) in zip(es, parts)])
    return acc / denom


def _qk(q, k):
    return lax.dot_general(q, k, (((1,), (1,)), ((), ())), preferred_element_type=F32)


def _natten_kernel(q_ref, k_ref, v_ref, kc_ref, vc_ref, ct_ref, o_ref, bt_ref, *, grid_rows):
    groups = grid_rows // Q_ROWS
    nq, band = Q_ROWS * GRID_W, BAND_ROWS * GRID_W
    kc = kc_ref[...]
    vc = vc_ref[...]

    @pl.when(jnp.logical_and(pl.program_id(0) == 0, pl.program_id(1) == 0))
    def _():
        bt_ref[...] = jnp.full(bt_ref.shape, MASK_BIAS, F32)

    for p, pattern in enumerate(_band_patterns(grid_rows)):
        for a, row in enumerate(pattern):
            for i, off in enumerate(row):
                if off is not None:
                    bt_ref[p, a * GRID_W:(a + 1) * GRID_W, i * GRID_W:(i + 1) * GRID_W] = ct_ref[off]

    def body(g, carry):
        band_start = jnp.clip(g * Q_ROWS - WIN_H // 2, 0, grid_rows - BAND_ROWS)
        pattern = jnp.where(g == 0, 0, jnp.where(g == groups - 1, 2, 1))
        q0 = pl.multiple_of(g * nq, nq)
        k0 = pl.multiple_of(band_start * GRID_W, GRID_W)
        q = q_ref[pl.ds(q0, nq), :]
        kb = k_ref[pl.ds(k0, band), :]
        vb = v_ref[pl.ds(k0, band), :]
        s_loc = _qk(q, kb) + bt_ref[pattern]
        s_ctx = _qk(q, kc)
        o_ref[pl.ds(q0, nq), :] = _softmax_pv([(s_loc, vb), (s_ctx, vc)]).astype(o_ref.dtype)
        return carry

    lax.fori_loop(0, groups, body, 0, unroll=2)


def neighbourhood_attention(qkv, kvc, col_table, *, batch, n_lat, n_ctx, kc_row0, kc_col0, vc_col0):
    dh, h = NA_HEAD_DIM, NA_HEADS
    grid_rows = n_lat // GRID_W
    rb0, ck, cv = kc_row0 // n_ctx, kc_col0 // dh, vc_col0 // dh
    return pl.pallas_call(
        functools.partial(_natten_kernel, grid_rows=grid_rows),
        grid=(batch, h),
        in_specs=[
            pl.BlockSpec((n_lat, dh), lambda b, hh: (b, hh)),
            pl.BlockSpec((n_lat, dh), lambda b, hh: (b, h + hh)),
            pl.BlockSpec((n_lat, dh), lambda b, hh: (b, 2 * h + hh)),
            pl.BlockSpec((n_ctx, dh), lambda b, hh: (rb0 + b, ck + hh)),
            pl.BlockSpec((n_ctx, dh), lambda b, hh: (rb0 + b, cv + hh)),
            pl.BlockSpec((None, 2 * WIN_H - 1, GRID_W, GRID_W), lambda b, hh: (hh, 0, 0, 0)),
        ],
        out_specs=pl.BlockSpec((n_lat, dh), lambda b, hh: (b, hh)),
        out_shape=jax.ShapeDtypeStruct((batch * n_lat, NA_WIDTH), BF16),
        scratch_shapes=[pltpu.VMEM((N_PATTERNS, Q_ROWS * GRID_W, BAND_ROWS * GRID_W), F32)],
        compiler_params=_params("arbitrary", "arbitrary"),
        name="neighbourhood_attention",
    )(qkv, qkv, qkv, kvc, kvc, col_table)


def _ctx_attn_kernel(q_ref, k_ref, v_ref, o_ref):
    o_ref[...] = _softmax_pv([(_qk(q_ref[...], k_ref[...]), v_ref[...])]).astype(o_ref.dtype)


def context_attention(qkv, *, batch, n_ctx, row0):
    dh, h = NA_HEAD_DIM, NA_HEADS
    rb0 = row0 // n_ctx
    return pl.pallas_call(
        _ctx_attn_kernel,
        grid=(batch, h),
        in_specs=[
            pl.BlockSpec((n_ctx, dh), lambda b, hh: (rb0 + b, hh)),
            pl.BlockSpec((n_ctx, dh), lambda b, hh: (rb0 + b, h + hh)),
            pl.BlockSpec((n_ctx, dh), lambda b, hh: (rb0 + b, 2 * h + hh)),
        ],
        out_specs=pl.BlockSpec((n_ctx, dh), lambda b, hh: (b, hh)),
        out_shape=jax.ShapeDtypeStruct((batch * n_ctx, NA_WIDTH), BF16),
        compiler_params=_params("arbitrary", "arbitrary"),
        name="context_attention",
    )(qkv, qkv, qkv)


def _gmlp_kernel(u_ref, gv_ref, ng_ref, ws_ref, bst_ref, o_ref):
    gd = GMLP_WIDTH // GMLP_GROUPS
    for c in range(u_ref.shape[0] // CHUNK):
        rows = slice(c * CHUNK, (c + 1) * CHUNK)
        v = jax.nn.gelu(gv_ref[rows, :].astype(F32))
        vc = v - jnp.mean(v, axis=-1, keepdims=True)
        y = vc * lax.rsqrt(jnp.mean(vc * vc, axis=-1, keepdims=True) + EPS) * ng_ref[...]
        yb = y.astype(BF16)
        for g in range(GMLP_GROUPS):
            sl = slice(g * gd, (g + 1) * gd)
            mixed = jnp.dot(ws_ref[g].astype(BF16), yb[:, sl], preferred_element_type=F32) + bst_ref[:, g:g + 1]
            o_ref[rows, sl] = (jax.nn.gelu(u_ref[rows, sl].astype(F32)) * mixed).astype(o_ref.dtype)


def spatial_gating(p, norm_g, ws, bs, layer, *, nrows, u_col0, gv_col0):
    w = GMLP_WIDTH
    bst = jnp.transpose(bs[layer])
    tr = 2 * CHUNK
    return pl.pallas_call(
        _gmlp_kernel,
        grid=(nrows // tr,),
        in_specs=[
            pl.BlockSpec((tr, w), lambda i: (i, u_col0 // w)),
            pl.BlockSpec((tr, w), lambda i: (i, gv_col0 // w)),
            pl.BlockSpec((None, 1, w), lambda i: (layer, 0, 0)),
            pl.BlockSpec((None, GMLP_GROUPS, CHUNK, CHUNK), lambda i: (layer, 0, 0, 0)),
            pl.BlockSpec((CHUNK, GMLP_GROUPS), lambda i: (0, 0)),
        ],
        out_specs=pl.BlockSpec((tr, w), lambda i: (i, 0)),
        out_shape=jax.ShapeDtypeStruct((nrows, w), BF16),
        compiler_params=_params("arbitrary"),
        name="spatial_gating",
    )(p, p, norm_g.reshape(norm_g.shape[0], 1, w), ws, bst)


def _split_bf16(x):
    hi = x.astype(BF16)
    return hi, (x - hi.astype(F32)).astype(BF16)


def _norm_router_kernel(x_ref, g_ref, sh_ref, sc_ref, rw_ref, h_ref, lg_ref):
    h = _rms(x_ref[...], g_ref[...]) * (1.0 + sc_ref[0]) + sh_ref[0]
    h_ref[...] = h
    hh, hl = _split_bf16(h)
    wh, wl = _split_bf16(rw_ref[...])
    lg = (jnp.dot(hh, wh, preferred_element_type=F32) + jnp.dot(hl, wh, preferred_element_type=F32)
          + jnp.dot(hh, wl, preferred_element_type=F32))
    lg_ref[...] = jnp.transpose(lg)[:N_EXPERTS, :]


def norm_mod_router(x, g, layer, mod3, rows, k_shift, k_scale, router_w, nrows):
    d = x.shape[1]
    tr = 256
    mod_row = rows.mod_row(tr)
    rw = jnp.pad(router_w[layer], ((0, 0), (0, LANES - N_EXPERTS)))
    return pl.pallas_call(
        _norm_router_kernel,
        grid=(nrows // tr,),
        in_specs=[
            pl.BlockSpec((tr, d), lambda i: (i, 0)),
            pl.BlockSpec((None, 1, d), lambda i: (layer, 0, 0)),
            pl.BlockSpec((1, 1, d), lambda i: (mod_row(i), 0, k_shift)),
            pl.BlockSpec((1, 1, d), lambda i: (mod_row(i), 0, k_scale)),
            pl.BlockSpec((d, LANES), lambda i: (0, 0)),
        ],
        out_specs=[pl.BlockSpec((tr, d), lambda i: (i, 0)), pl.BlockSpec((N_EXPERTS, tr), lambda i: (0, i))],
        out_shape=[jax.ShapeDtypeStruct((nrows, d), F32), jax.ShapeDtypeStruct((N_EXPERTS, nrows), F32)],
        compiler_params=_params("arbitrary"),
        name="norm_mod_router",
    )(x, g.reshape(g.shape[0], 1, d), mod3, mod3, rw)


def _cumsum_lanes(x):
    r, n = x.shape
    blk = min(n, LANES)
    tri = (lax.broadcasted_iota(I32, (blk, blk), 0) <= lax.broadcasted_iota(I32, (blk, blk), 1)).astype(BF16)
    off = jnp.zeros((r, 1), F32)
    out = []
    for j in range(n // blk):
        cs = jnp.dot(x[:, j * blk:(j + 1) * blk].astype(BF16), tri, preferred_element_type=F32) + off
        out.append(cs)
        off = cs[:, blk - 1:blk]
    return jnp.concatenate(out, axis=1) if len(out) > 1 else out[0]


def _route_kernel(lg_ref, idx_ref, dest_ref, gate_ref, start_ref, cnt_ref, slot_s, aff_s, *, cap):
    e_n, n = lg_ref.shape
    lg = lg_ref[...]
    ex = jnp.exp(lg - jnp.max(lg, axis=0, keepdims=True))
    aff = ex / jnp.sum(ex, axis=0, keepdims=True)
    bits = lax.bitcast_convert_type(aff, I32)

    thr = jnp.zeros((e_n, 1), I32)
    for bit in range(30, -1, -1):
        cand = thr | (1 << bit)
        n_ge = jnp.sum((bits >= cand).astype(F32), axis=1, keepdims=True)
        thr = jnp.where(n_ge >= cap, cand, thr)
    above = bits > thr
    tied = (bits == thr).astype(F32)
    need = cap - jnp.sum(above.astype(F32), axis=1, keepdims=True)
    tied_before = _cumsum_lanes(tied) - tied
    sel = jnp.where(above, 1.0, jnp.where(tied_before < need, tied, 0.0))

    slot = _cumsum_lanes(sel) - sel
    per_tok = jnp.sum(sel, axis=0, keepdims=True)
    per_tok8 = jnp.broadcast_to(per_tok, (8, n))
    start = (_cumsum_lanes(per_tok8) - per_tok8)[0:1]
    start_ref[...] = start.astype(I32)
    cnt_ref[...] = per_tok.astype(I32)
    slot_s[...] = jnp.where(sel > 0.0, slot, -1.0)
    aff_s[...] = aff

    tok = lax.broadcasted_iota(I32, (1, n), 1).astype(F32)
    cc = min(cap, LANES)

    def body(e, chosen_by_earlier):
        srow = slot_s[pl.ds(e, 1), :]
        arow = aff_s[pl.ds(e, 1), :]
        drow = start + chosen_by_earlier
        for c0 in range(0, cap, cc):
            want = (lax.broadcasted_iota(I32, (cc, 1), 0) + c0).astype(F32)
            hit = srow == want
            pick = lambda row: jnp.sum(jnp.where(hit, row, 0.0), axis=1, keepdims=True)
            idx_ref[e, pl.ds(c0, cc), :] = pick(tok).astype(I32)
            dest_ref[e, pl.ds(c0, cc), :] = pick(drow).astype(I32)
            gate_ref[e, pl.ds(c0, cc), :] = pick(arow)
        return chosen_by_earlier + (srow >= 0.0).astype(F32)

    lax.fori_loop(0, e_n, body, jnp.zeros((1, n), F32))


def expert_choice_route(logits_t, *, batch, n, col0):
    e_n = logits_t.shape[0]
    cap = CAPACITY_FACTOR * n // e_n
    cb0 = col0 // n
    per_slot = lambda dt: jax.ShapeDtypeStruct((batch, e_n, cap, 1), dt)
    per_tok = jax.ShapeDtypeStruct((batch, 1, n), I32)
    slot_spec = pl.BlockSpec((None, e_n, cap, 1), lambda b: (b, 0, 0, 0))
    tok_spec = pl.BlockSpec((None, 1, n), lambda b: (b, 0, 0))
    return pl.pallas_call(
        functools.partial(_route_kernel, cap=cap),
        grid=(batch,),
        in_specs=[pl.BlockSpec((e_n, n), lambda b: (0, cb0 + b))],
        out_specs=[slot_spec, slot_spec, slot_spec, tok_spec, tok_spec],
        out_shape=[per_slot(I32), per_slot(I32), per_slot(F32), per_tok, per_tok],
        scratch_shapes=[pltpu.VMEM((e_n, n), F32), pltpu.VMEM((e_n, n), F32)],
        compiler_params=_params("arbitrary"),
        name="expert_choice_route",
    )(logits_t)


ROWS_PER_STEP = 256
ROW_COPY_UNROLL = 8


def _start_rows(n, make_copy):
    def start(r, carry):
        make_copy(r).start()
        return carry

    lax.fori_loop(0, n, start, 0, unroll=ROW_COPY_UNROLL)


def _wait_rows(n, make_copy):
    def wait(r, carry):
        make_copy(r).wait()
        return carry

    lax.fori_loop(0, n, wait, 0, unroll=ROW_COPY_UNROLL)


def _gather_kernel(idx_ref, src_ref, o_ref, stage_ref, sem):
    step, n_steps, rows = pl.program_id(0), pl.num_programs(0), o_ref.shape[0]

    def copies(s):
        slot = s % 2
        return lambda r: pltpu.make_async_copy(src_ref.at[pl.ds(idx_ref[s * rows + r], 1), :],
                                               stage_ref.at[slot, pl.ds(r, 1), :], sem.at[slot])

    @pl.when(step == 0)
    def _():
        _start_rows(rows, copies(step))

    @pl.when(step + 1 < n_steps)
    def _():
        _start_rows(rows, copies(step + 1))

    _wait_rows(rows, copies(step))
    o_ref[...] = stage_ref[step % 2].astype(o_ref.dtype)


def gather_rows(src, idx, out_dtype):
    n, d = idx.shape[0], src.shape[1]
    rows = min(n, ROWS_PER_STEP)
    return pl.pallas_call(
        _gather_kernel,
        grid_spec=pltpu.PrefetchScalarGridSpec(
            num_scalar_prefetch=1,
            grid=(n // rows,),
            in_specs=[pl.BlockSpec(memory_space=pl.ANY)],
            out_specs=pl.BlockSpec((rows, d), lambda i, idx_ref: (i, 0)),
            scratch_shapes=[pltpu.VMEM((2, rows, d), src.dtype), pltpu.SemaphoreType.DMA((2,))],
        ),
        out_shape=jax.ShapeDtypeStruct((n, d), out_dtype),
        compiler_params=_params("arbitrary"),
        name="gather_rows",
    )(idx, src)


def _scatter_kernel(idx_ref, x_ref, o_ref, stage_ref, sem):
    step, n_steps, rows = pl.program_id(0), pl.num_programs(0), x_ref.shape[0]

    def copies(s):
        slot = s % 2
        return lambda r: pltpu.make_async_copy(stage_ref.at[slot, pl.ds(r, 1), :],
                                               o_ref.at[pl.ds(idx_ref[s * rows + r], 1), :], sem.at[slot])

    stage_ref[step % 2] = x_ref[...].astype(stage_ref.dtype)
    _start_rows(rows, copies(step))

    @pl.when(step > 0)
    def _():
        _wait_rows(rows, copies(step - 1))

    @pl.when(step + 1 == n_steps)
    def _():
        _wait_rows(rows, copies(step))


def scatter_rows(x, idx, out_dtype):
    n, d = x.shape
    rows = min(n, ROWS_PER_STEP)
    return pl.pallas_call(
        _scatter_kernel,
        grid_spec=pltpu.PrefetchScalarGridSpec(
            num_scalar_prefetch=1,
            grid=(n // rows,),
            in_specs=[pl.BlockSpec((rows, d), lambda i, idx_ref: (i, 0))],
            out_specs=pl.BlockSpec(memory_space=pl.ANY),
            scratch_shapes=[pltpu.VMEM((2, rows, d), out_dtype), pltpu.SemaphoreType.DMA((2,))],
        ),
        out_shape=jax.ShapeDtypeStruct((n, d), out_dtype),
        compiler_params=_params("arbitrary"),
        name="scatter_rows",
    )(idx, x)


def _ffn_kernel(xg_ref, w1_ref, w3_ref, w2_ref, g_ref, y_ref, hid_s, *, f1, tf):
    s = pl.program_id(1)

    @pl.when(s < f1)
    def _():
        xs = xg_ref[...]
        h1 = jnp.dot(xs, w1_ref[...].astype(BF16), preferred_element_type=F32)
        h3 = jnp.dot(xs, w3_ref[...].astype(BF16), preferred_element_type=F32)
        hid_s[s] = (jax.nn.silu(h1) * h3).astype(BF16)

    @pl.when(s >= f1)
    def _():
        acc = jnp.dot(hid_s[0], w2_ref[0:tf, :].astype(BF16), preferred_element_type=F32)
        for f in range(1, f1):
            acc += jnp.dot(hid_s[f], w2_ref[f * tf:(f + 1) * tf, :].astype(BF16), preferred_element_type=F32)
        y_ref[...] = (acc * g_ref[...]).astype(y_ref.dtype)


def expert_ffn(xg, w1, w3, w2, gate, layer):
    e_n, m, d = xg.shape
    ff = w1.shape[3]
    tf, tn = min(ff, 256), min(d, 512)
    f1, f2 = ff // tf, d // tn
    out_step = lambda s: jnp.maximum(s - f1, 0)
    next_e = lambda e: jnp.minimum(e + 1, e_n - 1)

    def rows_map(e, s):
        return (jnp.where(s < f1, e, next_e(e)), 0, 0)

    def up_map(switch_step):
        def index(e, s):
            switched = s >= f1 + switch_step
            return (layer, jnp.where(switched, next_e(e), e), 0, jnp.where(switched, 0, jnp.minimum(s, f1 - 1)))
        return index

    def down_map(e, s):
        hold = jnp.logical_and(s == 0, e > 0)
        return (layer, jnp.where(hold, e - 1, e), 0, jnp.where(hold, f2 - 1, out_step(s)))

    return pl.pallas_call(
        functools.partial(_ffn_kernel, f1=f1, tf=tf),
        grid=(e_n, f1 + f2),
        in_specs=[
            pl.BlockSpec((None, m, d), rows_map),
            pl.BlockSpec((None, None, d, tf), up_map(f2 // 3)),
            pl.BlockSpec((None, None, d, tf), up_map(2 * f2 // 3)),
            pl.BlockSpec((None, None, ff, tn), down_map),
            pl.BlockSpec((None, m, 1), lambda e, s: (e, 0, 0)),
        ],
        out_specs=pl.BlockSpec((None, m, tn), lambda e, s: (e, 0, out_step(s))),
        out_shape=jax.ShapeDtypeStruct((e_n, m, d), BF16),
        scratch_shapes=[pltpu.VMEM((f1, m, tf), BF16)],
        compiler_params=_params("arbitrary", "arbitrary"),
        name="expert_ffn",
    )(xg, w1, w3, w2, gate)


ITEM_ADD, ITEM_FIRST, ITEM_LAST = 1, 2, 4


def _combine_kernel(tile_ref, chunk_ref, flag_ref, yg_ref, st_ref, ct_ref, x_ref, gt_ref, ng_ref, o_ref, acc_s, *,
                    ch, normalize):
    del tile_ref
    item = pl.program_id(0) * pl.num_programs(1) + pl.program_id(1)
    flags = flag_ref[item]

    @pl.when((flags & ITEM_FIRST) != 0)
    def _():
        acc_s[...] = jnp.zeros_like(acc_s)

    @pl.when((flags & ITEM_ADD) != 0)
    def _():
        pair = chunk_ref[item] * ch + lax.broadcasted_iota(I32, (ch, 1), 0)
        rel = pair - st_ref[...]
        own_t = jnp.where(jnp.logical_and(rel >= 0, rel < ct_ref[...]), 1.0, 0.0).astype(BF16)
        acc_s[...] += lax.dot_general(own_t, yg_ref[...].astype(BF16), (((0,), (0,)), ((), ())),
                                      preferred_element_type=F32)

    @pl.when((flags & ITEM_LAST) != 0)
    def _():
        out = x_ref[...] + gt_ref[0] * acc_s[...]
        o_ref[...] = _rms(out, ng_ref[...]) if normalize else out


def _combine_work_items(start, *, batch, n, tt, pairs, ch):
    tiles, n_chunks = n // tt, pairs // ch
    n_items = tiles + n_chunks
    first = start.reshape(batch, n)[:, ::tt]
    last = jnp.concatenate([first[:, 1:], jnp.full((batch, 1), pairs, I32)], axis=1)
    c_first = jnp.minimum(first // ch, n_chunks - 1)
    c_num = jnp.where(last > first, (last - 1) // ch - first // ch + 1, 0)
    t_items = jnp.maximum(c_num, 1)
    ends = jnp.cumsum(t_items, axis=1)
    k = jnp.arange(n_items, dtype=I32)[None, :]
    tile = jnp.minimum(jnp.sum((ends[:, None, :] <= k[:, :, None]).astype(I32), axis=-1), tiles - 1)
    at = lambda v: jnp.take_along_axis(v, tile, axis=1)
    j = k - at(ends - t_items)
    real = k < ends[:, -1:]
    chunk = at(c_first) + jnp.minimum(j, jnp.maximum(at(c_num) - 1, 0))
    flags = (ITEM_ADD * (real & (j < at(c_num))) + ITEM_FIRST * (real & (j == 0))
             + ITEM_LAST * (real & (j == at(t_items) - 1)))
    flat = lambda v: v.reshape(-1).astype(I32)
    return n_items, flat(tile), flat(chunk), flat(flags)


def combine_residual(yg, start, cnt, x, mod3, rows, k_gate, norm_g, *, batch, n, row0, pairs, pair_row0, normalize):
    d = x.shape[1]
    tt = min(n, 256)
    ch = min(pairs, 256)
    tiles, n_chunks = n // tt, pairs // ch
    n_items, item_tile, item_chunk, item_flags = _combine_work_items(start, batch=batch, n=n, tt=tt, pairs=pairs,
                                                                     ch=ch)
    mod_row = rows.mod_row(tt)
    rt0, chunk0 = row0 // tt, pair_row0 // ch

    def tile_of(b, k, tile_ref):
        return tile_ref[b * n_items + k]

    yg_map = lambda b, k, tile_ref, chunk_ref, flag_ref: (chunk0 + b * n_chunks + chunk_ref[b * n_items + k], 0)
    row_map = lambda b, k, tile_ref, chunk_ref, flag_ref: (rt0 + b * tiles + tile_of(b, k, tile_ref), 0)
    tok_map = lambda b, k, tile_ref, chunk_ref, flag_ref: (b, 0, tile_of(b, k, tile_ref))
    gate_map = lambda b, k, tile_ref, chunk_ref, flag_ref: (
        mod_row(rt0 + b * tiles + tile_of(b, k, tile_ref)), 0, k_gate)
    return pl.pallas_call(
        functools.partial(_combine_kernel, ch=ch, normalize=normalize),
        grid_spec=pltpu.PrefetchScalarGridSpec(
            num_scalar_prefetch=3,
            grid=(batch, n_items),
            in_specs=[
                pl.BlockSpec((ch, d), yg_map),
                pl.BlockSpec((None, 1, tt), tok_map),
                pl.BlockSpec((None, 1, tt), tok_map),
                pl.BlockSpec((tt, d), row_map),
                pl.BlockSpec((1, 1, d), gate_map),
                pl.BlockSpec((1, d), lambda b, k, tile_ref, chunk_ref, flag_ref: (0, 0)),
            ],
            out_specs=pl.BlockSpec((tt, d), row_map),
            scratch_shapes=[pltpu.VMEM((tt, d), F32)],
        ),
        out_shape=jax.ShapeDtypeStruct(x.shape, F32),
        input_output_aliases={6: 0},
        compiler_params=_params("arbitrary", "arbitrary"),
        name="combine_residual",
    )(item_tile, item_chunk, item_flags, yg, start, cnt, x, mod3, norm_g.reshape(1, d))


def expert_choice_ffn_residual(x, h, logits_t, mod3, rows, k_gate, w1, w3, w2, layer, norm_g, *, batch, sets,
                               normalize):
    e_n = logits_t.shape[0]
    d = x.shape[1]
    boff = jnp.arange(batch, dtype=I32)[:, None, None]
    routed, src_rows, pair_rows, gate_cols, pair_row0 = [], [], [], [], 0
    for n, row0 in sets:
        idx, dest, gate, start, cnt = expert_choice_route(logits_t, batch=batch, n=n, col0=row0)
        cap = idx.shape[2]
        pairs = e_n * cap
        src_rows.append(jnp.transpose(idx[..., 0] + row0 + boff * n, (1, 0, 2)).reshape(e_n, batch * cap))
        pair_rows.append(jnp.transpose(dest[..., 0] + pair_row0 + boff * pairs, (1, 0, 2)).reshape(e_n, batch * cap))
        gate_cols.append(jnp.transpose(gate, (1, 0, 2, 3)).reshape(e_n, batch * cap, 1))
        routed.append((start, cnt, n, row0, pairs, pair_row0))
        pair_row0 += batch * pairs
    src_rows = jnp.concatenate(src_rows, axis=1)
    m = src_rows.shape[1]
    xg = gather_rows(h, src_rows.reshape(-1), BF16).reshape(e_n, m, d)
    y = expert_ffn(xg, w1, w3, w2, jnp.concatenate(gate_cols, axis=1), layer).reshape(e_n * m, d)
    yg = scatter_rows(y, jnp.concatenate(pair_rows, axis=1).reshape(-1), F32)
    for start, cnt, n, row0, pairs, pair_row0 in routed:
        x = combine_residual(yg, start, cnt, x, mod3, rows, k_gate, norm_g, batch=batch, n=n, row0=row0,
                             pairs=pairs, pair_row0=pair_row0, normalize=normalize)
    return x


def kernel(x, c, ctx, c_ctx, ada_w, ada_b, norm1_g, norm2_g, w_in, na_rpb, gmlp_norm_g, gmlp_ws, gmlp_bs,
           w_branch_a, w_branch_b, w_out, router_w, exp_w1, exp_w3, exp_w2, final_norm_g):
    batch, n_lat, d = x.shape
    n_ctx = ctx.shape[1]
    depth = ada_w.shape[0]
    rows = Rows(batch, n_lat, n_ctx)
    lat_rows, all_rows = rows.lat_rows, rows.all_rows
    off_k, off_v, off_u = NA_WIDTH, 2 * NA_WIDTH, 3 * NA_WIDTH
    off_gv = off_u + GMLP_WIDTH
    off_ga = off_gv + GMLP_WIDTH
    off_gb = off_ga + d
    in_cols = off_gb + d

    x_head, x_tail = x.reshape(lat_rows, d), ctx.reshape(batch * n_ctx, d)
    cv = jnp.zeros((MOD_ROWS, d), F32).at[:batch].set(c).at[batch].set(c_ctx)

    for layer in range(depth):
        last = layer == depth - 1
        nrows = lat_rows if last else all_rows
        mod3 = ada_mod(cv, ada_w, ada_b, layer).reshape(MOD_ROWS, 1, N_MOD * d)

        h = norm_mod(x_head, x_tail, norm1_g, layer, mod3, rows, 0, 1, 0, nrows, BF16)
        qkv = rest = matmul(h, w_in, layer, row0=0, nrows=nrows, col0=0, ncols=in_cols, out_dtype=BF16,
                            scaled_cols=NA_WIDTH, scale=NA_HEAD_DIM ** -0.5)
        bias_table = build_col_table(na_rpb[layer])
        if last:
            h_ctx = norm_mod(x_head, x_tail, norm1_g, layer, mod3, rows, 0, 1, lat_rows, batch * n_ctx, BF16)
            kvc = matmul(h_ctx, w_in, layer, row0=0, nrows=batch * n_ctx, col0=off_k, ncols=2 * NA_WIDTH,
                         out_dtype=BF16)
            o_a = neighbourhood_attention(qkv, kvc, bias_table, batch=batch, n_lat=n_lat, n_ctx=n_ctx,
                                          kc_row0=0, kc_col0=0, vc_col0=NA_WIDTH)
        else:
            o_lat = neighbourhood_attention(qkv, qkv, bias_table, batch=batch, n_lat=n_lat, n_ctx=n_ctx,
                                            kc_row0=lat_rows, kc_col0=off_k, vc_col0=off_v)
            o_ctx = context_attention(qkv, batch=batch, n_ctx=n_ctx, row0=lat_rows)
            o_a = jnp.concatenate([o_lat, o_ctx], axis=0)
        o_b = spatial_gating(rest, gmlp_norm_g, gmlp_ws, gmlp_bs, layer, nrows=nrows, u_col0=off_u, gv_col0=off_gv)
        y = gated_merge(o_a, o_b, w_branch_a, w_branch_b, rest, layer, nrows=nrows, ga_col0=off_ga, gb_col0=off_gb)
        stream = outproj_residual(y, w_out, x_head, x_tail, mod3, layer, rows, 2, nrows=nrows)

        h2, logits_t = norm_mod_router(stream, norm2_g, layer, mod3, rows, 3, 4, router_w, nrows)
        sets = [(n_lat, 0)] if last else [(n_lat, 0), (n_ctx, lat_rows)]
        stream = expert_choice_ffn_residual(stream, h2, logits_t, mod3, rows, 5, exp_w1, exp_w3, exp_w2, layer,
                                            final_norm_g, batch=batch, sets=sets, normalize=last)
        x_head = x_tail = stream

    return stream.reshape(batch, n_lat, d)
```

```python
import functools

import jax
import jax.numpy as jnp
from jax import lax
from jax.experimental import pallas as pl
from jax.experimental.pallas import tpu as pltpu

F32, BF16, I32 = jnp.float32, jnp.bfloat16, jnp.int32

GRID_W = 64
NA_HEADS = 16
NA_HEAD_DIM = 128
NA_WIDTH = NA_HEADS * NA_HEAD_DIM
WIN_H = 8
WIN_W = 16
GMLP_GROUPS = 16
GMLP_WIDTH = 2048
CHUNK = 128
N_EXPERTS = 16
CAPACITY_FACTOR = 2
N_MOD = 6
EPS = 1e-6
MASK_BIAS = -1e30

V7X_VMEM_LIMIT_BYTES = 56 * 1024 * 1024
LANES = 128
MOD_ROWS = 8


def _params(*sem):
    return pltpu.CompilerParams(dimension_semantics=sem, vmem_limit_bytes=V7X_VMEM_LIMIT_BYTES)


def _cast_weight(w_ref, wb_ref):
    rows = w_ref.shape[0]
    step = min(rows, 512)

    def body(i, carry):
        r = pl.multiple_of(i * step, step)
        wb_ref[pl.ds(r, step), :] = w_ref[pl.ds(r, step), :].astype(BF16)
        return carry

    lax.fori_loop(0, rows // step, body, 0)


def _ada_kernel(c_ref, w_ref, b_ref, o_ref):
    c = c_ref[...]
    s = c * jax.nn.sigmoid(c)
    o_ref[...] = jnp.dot(s.astype(BF16), w_ref[...].astype(BF16), preferred_element_type=F32) + b_ref[...]


def ada_mod(cv, ada_w, ada_b, layer):
    _, d, m = ada_w.shape
    tn = min(m, 512)
    return pl.pallas_call(
        _ada_kernel,
        grid=(m // tn,),
        in_specs=[
            pl.BlockSpec((MOD_ROWS, d), lambda j: (0, 0)),
            pl.BlockSpec((None, d, tn), lambda j: (layer, 0, j)),
            pl.BlockSpec((None, 1, tn), lambda j: (layer, 0, j)),
        ],
        out_specs=pl.BlockSpec((MOD_ROWS, tn), lambda j: (0, j)),
        out_shape=jax.ShapeDtypeStruct((MOD_ROWS, m), F32),
        compiler_params=_params("arbitrary"),
        name="ada_mod",
    )(cv, ada_w, ada_b.reshape(ada_b.shape[0], 1, m))


class Rows:
    def __init__(self, batch, n_lat, n_ctx):
        self.batch, self.n_lat, self.n_ctx = batch, n_lat, n_ctx
        self.lat_rows = batch * n_lat
        self.all_rows = self.lat_rows + batch * n_ctx

    def mod_row(self, tile_rows):
        lat_tiles = self.lat_rows // tile_rows
        per_batch = self.n_lat // tile_rows
        batch = self.batch

        def f(i):
            return jnp.where(i < lat_tiles, i // per_batch, batch)

        return f


def _mod_spec(mod_row, k, d):
    return pl.BlockSpec((1, 1, d), lambda i: (mod_row(i), 0, k))


def _rms(x, g):
    return x * lax.rsqrt(jnp.mean(x * x, axis=-1, keepdims=True) + EPS) * g


def _two_source_specs(block, head_tiles, tile_of):
    def head(*ids):
        t, c = tile_of(*ids)
        return (jnp.minimum(t, head_tiles - 1), c)

    def tail(*ids):
        t, c = tile_of(*ids)
        return (jnp.maximum(t - head_tiles, 0), c)

    return pl.BlockSpec(block, head), pl.BlockSpec(block, tail)


def _norm_mod_kernel(xh_ref, xt_ref, g_ref, sh_ref, sc_ref, o_ref, *, head_tiles, t0):
    def emit(x_ref):
        y = _rms(x_ref[...], g_ref[...])
        o_ref[...] = (y * (1.0 + sc_ref[0]) + sh_ref[0]).astype(o_ref.dtype)

    in_head = pl.program_id(0) + t0 < head_tiles
    pl.when(in_head)(lambda: emit(xh_ref))
    pl.when(jnp.logical_not(in_head))(lambda: emit(xt_ref))


def norm_mod(x_head, x_tail, g, layer, mod3, rows, k_shift, k_scale, row0, nrows, out_dtype):
    d = x_head.shape[1]
    tr = 256
    mod_row = rows.mod_row(tr)
    t0 = row0 // tr
    head_tiles = x_head.shape[0] // tr
    head_spec, tail_spec = _two_source_specs((tr, d), head_tiles, lambda i: (i + t0, 0))
    return pl.pallas_call(
        functools.partial(_norm_mod_kernel, head_tiles=head_tiles, t0=t0),
        grid=(nrows // tr,),
        in_specs=[
            head_spec,
            tail_spec,
            pl.BlockSpec((None, 1, d), lambda i: (layer, 0, 0)),
            pl.BlockSpec((1, 1, d), lambda i: (mod_row(i + t0), 0, k_shift)),
            pl.BlockSpec((1, 1, d), lambda i: (mod_row(i + t0), 0, k_scale)),
        ],
        out_specs=pl.BlockSpec((tr, d), lambda i: (i, 0)),
        out_shape=jax.ShapeDtypeStruct((nrows, d), out_dtype),
        compiler_params=_params("arbitrary"),
        name="norm_mod",
    )(x_head, x_tail, g.reshape(g.shape[0], 1, d), mod3, mod3)


WEIGHT_TILE_BUFFERS = pl.Buffered(1)
MATMUL_ROW_TILES = 8

def _mm_kernel(a_ref, w_ref, o_ref, wb_ref, *, scaled_tiles, scale):
    @pl.when(pl.program_id(1) == 0)
    def _():
        _cast_weight(w_ref, wb_ref)

    acc = jnp.dot(a_ref[...], wb_ref[...], preferred_element_type=F32)
    if scaled_tiles:
        acc = acc * jnp.where(pl.program_id(0) < scaled_tiles, scale, 1.0)
    o_ref[...] = acc.astype(o_ref.dtype)


def matmul(a, w, layer, *, row0, nrows, col0, ncols, out_dtype, scaled_cols=0, scale=1.0, tm=512, tn=1024):
    k = a.shape[1]
    if nrows % (MATMUL_ROW_TILES * 16) == 0 and nrows // MATMUL_ROW_TILES > tm:
        tm = nrows // MATMUL_ROW_TILES
    tm, tn = min(tm, nrows), min(tn, ncols)
    r0, c0 = row0 // tm, col0 // tn
    assert scaled_cols % tn == 0 and row0 % tm == 0
    return pl.pallas_call(
        functools.partial(_mm_kernel, scaled_tiles=scaled_cols // tn, scale=scale),
        grid=(ncols // tn, nrows // tm),
        in_specs=[
            pl.BlockSpec((tm, k), lambda j, i: (i + r0, 0)),
            pl.BlockSpec((None, k, tn), lambda j, i: (layer, 0, j + c0), pipeline_mode=WEIGHT_TILE_BUFFERS),
        ],
        out_specs=pl.BlockSpec((tm, tn), lambda j, i: (i, j)),
        out_shape=jax.ShapeDtypeStruct((nrows, ncols), out_dtype),
        scratch_shapes=[pltpu.VMEM((k, tn), BF16)],
        compiler_params=_params("arbitrary", "arbitrary"),
        name="matmul",
    )(a, w)


def _merge_kernel(oa_ref, ob_ref, wa_ref, wb_ref, ga_ref, gb_ref, y_ref, wab_ref, wbb_ref):
    @pl.when(pl.program_id(1) == 0)
    def _():
        _cast_weight(wa_ref, wab_ref)
        _cast_weight(wb_ref, wbb_ref)

    ya = jnp.dot(oa_ref[...], wab_ref[...], preferred_element_type=F32)
    yb = jnp.dot(ob_ref[...], wbb_ref[...], preferred_element_type=F32)
    y = jax.nn.sigmoid(ga_ref[...].astype(F32)) * ya + jax.nn.sigmoid(gb_ref[...].astype(F32)) * yb
    y_ref[...] = y.astype(y_ref.dtype)


def gated_merge(o_a, o_b, w_a, w_b, gates, layer, *, nrows, ga_col0, gb_col0, tm=512, tn=1024):
    ka, kb = o_a.shape[1], o_b.shape[1]
    d = w_a.shape[2]
    tm, tn = min(tm, nrows), min(tn, d)
    ca, cb = ga_col0 // tn, gb_col0 // tn
    return pl.pallas_call(
        _merge_kernel,
        grid=(d // tn, nrows // tm),
        in_specs=[
            pl.BlockSpec((tm, ka), lambda j, i: (i, 0)),
            pl.BlockSpec((tm, kb), lambda j, i: (i, 0)),
            pl.BlockSpec((None, ka, tn), lambda j, i: (layer, 0, j), pipeline_mode=WEIGHT_TILE_BUFFERS),
            pl.BlockSpec((None, kb, tn), lambda j, i: (layer, 0, j), pipeline_mode=WEIGHT_TILE_BUFFERS),
            pl.BlockSpec((tm, tn), lambda j, i: (i, j + ca)),
            pl.BlockSpec((tm, tn), lambda j, i: (i, j + cb)),
        ],
        out_specs=pl.BlockSpec((tm, tn), lambda j, i: (i, j)),
        out_shape=jax.ShapeDtypeStruct((nrows, d), BF16),
        scratch_shapes=[pltpu.VMEM((ka, tn), BF16), pltpu.VMEM((kb, tn), BF16)],
        compiler_params=_params("arbitrary", "arbitrary"),
        name="gated_merge",
    )(o_a, o_b, w_a, w_b, gates, gates)


def _outproj_kernel(y_ref, w_ref, xh_ref, xt_ref, gt_ref, o_ref, wb_ref, *, head_tiles):
    @pl.when(pl.program_id(1) == 0)
    def _():
        _cast_weight(w_ref, wb_ref)

    upd = gt_ref[0] * jnp.dot(y_ref[...], wb_ref[...], preferred_element_type=F32)

    def emit(x_ref):
        o_ref[...] = x_ref[...] + upd

    in_head = pl.program_id(1) < head_tiles
    pl.when(in_head)(lambda: emit(xh_ref))
    pl.when(jnp.logical_not(in_head))(lambda: emit(xt_ref))


def outproj_residual(y, w_out, x_head, x_tail, mod3, layer, rows, k_gate, *, nrows, tm=512, tn=1024):
    k = y.shape[1]
    d = w_out.shape[2]
    tm, tn = min(tm, nrows), min(tn, d)
    mod_row = rows.mod_row(tm)
    gate_col0 = k_gate * (d // tn)
    head_tiles = x_head.shape[0] // tm
    head_spec, tail_spec = _two_source_specs((tm, tn), head_tiles, lambda j, i: (i, j))
    return pl.pallas_call(
        functools.partial(_outproj_kernel, head_tiles=head_tiles),
        grid=(d // tn, nrows // tm),
        in_specs=[
            pl.BlockSpec((tm, k), lambda j, i: (i, 0)),
            pl.BlockSpec((None, k, tn), lambda j, i: (layer, 0, j), pipeline_mode=WEIGHT_TILE_BUFFERS),
            head_spec,
            tail_spec,
            pl.BlockSpec((1, 1, tn), lambda j, i: (mod_row(i), 0, j + gate_col0)),
        ],
        out_specs=pl.BlockSpec((tm, tn), lambda j, i: (i, j)),
        out_shape=jax.ShapeDtypeStruct((nrows, d), F32),
        scratch_shapes=[pltpu.VMEM((k, tn), BF16)],
        compiler_params=_params("arbitrary", "arbitrary"),
        name="outproj_residual",
    )(y, w_out, x_head, x_tail, mod3)


Q_ROWS = 4
BAND_ROWS = WIN_H + Q_ROWS
N_PATTERNS = 3


def build_col_table(rpb):
    cols = jnp.arange(GRID_W)
    col_start = jnp.clip(cols - WIN_W // 2, 0, GRID_W - WIN_W)
    col_ok = (cols[None, :] >= col_start[:, None]) & (cols[None, :] < col_start[:, None] + WIN_W)
    col_off = jnp.clip(cols[None, :] - cols[:, None] + (WIN_W - 1), 0, 2 * WIN_W - 2)
    return jnp.where(col_ok[None, None], rpb[:, :, col_off], MASK_BIAS).astype(F32)


def _band_patterns(grid_rows):
    half = WIN_H // 2
    groups = grid_rows // Q_ROWS
    assert grid_rows % Q_ROWS == 0 and grid_rows >= BAND_ROWS
    patterns = []
    for g in (0, min(1, groups - 1), groups - 1):
        band_start = min(max(g * Q_ROWS - half, 0), grid_rows - BAND_ROWS)
        pattern = []
        for a in range(Q_ROWS):
            q_row = g * Q_ROWS + a
            win_start = min(max(q_row - half, 0), grid_rows - WIN_H)
            pattern.append([band_start + i - q_row + WIN_H - 1 if win_start <= band_start + i < win_start + WIN_H
                            else None for i in range(BAND_ROWS)])
        patterns.append(pattern)
    return patterns


def _softmax_pv(parts):
    m = functools.reduce(jnp.maximum, [jnp.max(s, axis=-1, keepdims=True) for s, _ in parts])
    es = [jnp.exp(s - m) for s, _ in parts]
    denom = functools.reduce(jnp.add, [jnp.sum(e, axis=-1, keepdims=True) for e in es])
    acc = functools.reduce(jnp.add, [jnp.dot(e.astype(BF16), v, preferred_element_type=F32)
                                     for e, (_, v) in zip(es, parts)])
    return acc / denom


def _qk(q, k):
    return lax.dot_general(q, k, (((1,), (1,)), ((), ())), preferred_element_type=F32)


def _natten_kernel(q_ref, k_ref, v_ref, kc_ref, vc_ref, ct_ref, o_ref, bt_ref, *, grid_rows):
    groups = grid_rows // Q_ROWS
    nq, band = Q_ROWS * GRID_W, BAND_ROWS * GRID_W
    kc = kc_ref[...]
    vc = vc_ref[...]

    @pl.when(jnp.logical_and(pl.program_id(0) == 0, pl.program_id(1) == 0))
    def _():
        bt_ref[...] = jnp.full(bt_ref.shape, MASK_BIAS, F32)

    @pl.when(pl.program_id(1) == 0)
    def _():
        for p, pattern in enumerate(_band_patterns(grid_rows)):
            for a, row in enumerate(pattern):
                for i, off in enumerate(row):
                    if off is not None:
                        bt_ref[p, a * GRID_W:(a + 1) * GRID_W, i * GRID_W:(i + 1) * GRID_W] = ct_ref[off]

    def body(g, carry):
        band_start = jnp.clip(g * Q_ROWS - WIN_H // 2, 0, grid_rows - BAND_ROWS)
        pattern = jnp.where(g == 0, 0, jnp.where(g == groups - 1, 2, 1))
        q0 = pl.multiple_of(g * nq, nq)
        k0 = pl.multiple_of(band_start * GRID_W, GRID_W)
        q = q_ref[pl.ds(q0, nq), :]
        kb = k_ref[pl.ds(k0, band), :]
        vb = v_ref[pl.ds(k0, band), :]
        s_loc = _qk(q, kb) + bt_ref[pattern]
        s_ctx = _qk(q, kc)
        o_ref[pl.ds(q0, nq), :] = _softmax_pv([(s_loc, vb), (s_ctx, vc)]).astype(o_ref.dtype)
        return carry

    lax.fori_loop(0, groups, body, 0, unroll=2)


def neighbourhood_attention(qkv, kvc, col_table, *, batch, n_lat, n_ctx, kc_row0, kc_col0, vc_col0):
    dh, h = NA_HEAD_DIM, NA_HEADS
    grid_rows = n_lat // GRID_W
    rb0, ck, cv = kc_row0 // n_ctx, kc_col0 // dh, vc_col0 // dh
    return pl.pallas_call(
        functools.partial(_natten_kernel, grid_rows=grid_rows),
        grid=(h, batch),
        in_specs=[
            pl.BlockSpec((n_lat, dh), lambda hh, b: (b, hh)),
            pl.BlockSpec((n_lat, dh), lambda hh, b: (b, h + hh)),
            pl.BlockSpec((n_lat, dh), lambda hh, b: (b, 2 * h + hh)),
            pl.BlockSpec((n_ctx, dh), lambda hh, b: (rb0 + b, ck + hh)),
            pl.BlockSpec((n_ctx, dh), lambda hh, b: (rb0 + b, cv + hh)),
            pl.BlockSpec((None, 2 * WIN_H - 1, GRID_W, GRID_W), lambda hh, b: (hh, 0, 0, 0)),
        ],
        out_specs=pl.BlockSpec((n_lat, dh), lambda hh, b: (b, hh)),
        out_shape=jax.ShapeDtypeStruct((batch * n_lat, NA_WIDTH), BF16),
        scratch_shapes=[pltpu.VMEM((N_PATTERNS, Q_ROWS * GRID_W, BAND_ROWS * GRID_W), F32)],
        compiler_params=_params("arbitrary", "arbitrary"),
        name="neighbourhood_attention",
    )(qkv, qkv, qkv, kvc, kvc, col_table)


def _ctx_attn_kernel(q_ref, k_ref, v_ref, o_ref):
    o_ref[...] = _softmax_pv([(_qk(q_ref[...], k_ref[...]), v_ref[...])]).astype(o_ref.dtype)


def context_attention(qkv, *, batch, n_ctx, row0):
    dh, h = NA_HEAD_DIM, NA_HEADS
    rb0 = row0 // n_ctx
    return pl.pallas_call(
        _ctx_attn_kernel,
        grid=(batch, h),
        in_specs=[
            pl.BlockSpec((n_ctx, dh), lambda b, hh: (rb0 + b, hh)),
            pl.BlockSpec((n_ctx, dh), lambda b, hh: (rb0 + b, h + hh)),
            pl.BlockSpec((n_ctx, dh), lambda b, hh: (rb0 + b, 2 * h + hh)),
        ],
        out_specs=pl.BlockSpec((n_ctx, dh), lambda b, hh: (b, hh)),
        out_shape=jax.ShapeDtypeStruct((batch * n_ctx, NA_WIDTH), BF16),
        compiler_params=_params("arbitrary", "arbitrary"),
        name="context_attention",
    )(qkv, qkv, qkv)


def _gmlp_kernel(u_ref, gv_ref, ng_ref, ws_ref, bst_ref, o_ref):
    gd = GMLP_WIDTH // GMLP_GROUPS
    for c in range(u_ref.shape[0] // CHUNK):
        rows = slice(c * CHUNK, (c + 1) * CHUNK)
        v = jax.nn.gelu(gv_ref[rows, :].astype(F32))
        vc = v - jnp.mean(v, axis=-1, keepdims=True)
        y = vc * lax.rsqrt(jnp.mean(vc * vc, axis=-1, keepdims=True) + EPS) * ng_ref[...]
        yb = y.astype(BF16)
        for g in range(GMLP_GROUPS):
            sl = slice(g * gd, (g + 1) * gd)
            mixed = jnp.dot(ws_ref[g].astype(BF16), yb[:, sl], preferred_element_type=F32) + bst_ref[:, g:g + 1]
            o_ref[rows, sl] = (jax.nn.gelu(u_ref[rows, sl].astype(F32)) * mixed).astype(o_ref.dtype)


def spatial_gating(p, norm_g, ws, bs, layer, *, nrows, u_col0, gv_col0):
    w = GMLP_WIDTH
    bst = jnp.transpose(bs[layer])
    tr = 2 * CHUNK
    return pl.pallas_call(
        _gmlp_kernel,
        grid=(nrows // tr,),
        in_specs=[
            pl.BlockSpec((tr, w), lambda i: (i, u_col0 // w)),
            pl.BlockSpec((tr, w), lambda i: (i, gv_col0 // w)),
            pl.BlockSpec((None, 1, w), lambda i: (layer, 0, 0)),
            pl.BlockSpec((None, GMLP_GROUPS, CHUNK, CHUNK), lambda i: (layer, 0, 0, 0)),
            pl.BlockSpec((CHUNK, GMLP_GROUPS), lambda i: (0, 0)),
        ],
        out_specs=pl.BlockSpec((tr, w), lambda i: (i, 0)),
        out_shape=jax.ShapeDtypeStruct((nrows, w), BF16),
        compiler_params=_params("arbitrary"),
        name="spatial_gating",
    )(p, p, norm_g.reshape(norm_g.shape[0], 1, w), ws, bst)


def _split_bf16(x):
    hi = x.astype(BF16)
    return hi, (x - hi.astype(F32)).astype(BF16)


def _norm_router_kernel(x_ref, g_ref, sh_ref, sc_ref, rw_ref, h_ref, lg_ref):
    h = _rms(x_ref[...], g_ref[...]) * (1.0 + sc_ref[0]) + sh_ref[0]
    h_ref[...] = h
    hh, hl = _split_bf16(h)
    wh, wl = _split_bf16(rw_ref[...])
    lg = (jnp.dot(hh, wh, preferred_element_type=F32) + jnp.dot(hl, wh, preferred_element_type=F32)
          + jnp.dot(hh, wl, preferred_element_type=F32))
    lg_ref[...] = jnp.transpose(lg)[:N_EXPERTS, :]


def norm_mod_router(x, g, layer, mod3, rows, k_shift, k_scale, router_w, nrows):
    d = x.shape[1]
    tr = 256
    mod_row = rows.mod_row(tr)
    rw = jnp.pad(router_w[layer], ((0, 0), (0, LANES - N_EXPERTS)))
    return pl.pallas_call(
        _norm_router_kernel,
        grid=(nrows // tr,),
        in_specs=[
            pl.BlockSpec((tr, d), lambda i: (i, 0)),
            pl.BlockSpec((None, 1, d), lambda i: (layer, 0, 0)),
            pl.BlockSpec((1, 1, d), lambda i: (mod_row(i), 0, k_shift)),
            pl.BlockSpec((1, 1, d), lambda i: (mod_row(i), 0, k_scale)),
            pl.BlockSpec((d, LANES), lambda i: (0, 0)),
        ],
        out_specs=[pl.BlockSpec((tr, d), lambda i: (i, 0)), pl.BlockSpec((N_EXPERTS, tr), lambda i: (0, i))],
        out_shape=[jax.ShapeDtypeStruct((nrows, d), F32), jax.ShapeDtypeStruct((N_EXPERTS, nrows), F32)],
        compiler_params=_params("arbitrary"),
        name="norm_mod_router",
    )(x, g.reshape(g.shape[0], 1, d), mod3, mod3, rw)


def _cumsum_lanes(x):
    r, n = x.shape
    blk = min(n, LANES)
    tri = (lax.broadcasted_iota(I32, (blk, blk), 0) <= lax.broadcasted_iota(I32, (blk, blk), 1)).astype(BF16)
    off = jnp.zeros((r, 1), F32)
    out = []
    for j in range(n // blk):
        cs = jnp.dot(x[:, j * blk:(j + 1) * blk].astype(BF16), tri, preferred_element_type=F32) + off
        out.append(cs)
        off = cs[:, blk - 1:blk]
    return jnp.concatenate(out, axis=1) if len(out) > 1 else out[0]


def _route_kernel(lg_ref, idx_ref, dest_ref, gate_ref, start_ref, cnt_ref, slot_s, aff_s, *, cap):
    e_n, n = lg_ref.shape
    lg = lg_ref[...]
    ex = jnp.exp(lg - jnp.max(lg, axis=0, keepdims=True))
    aff = ex / jnp.sum(ex, axis=0, keepdims=True)
    bits = lax.bitcast_convert_type(aff, I32)

    thr = jnp.zeros((e_n, 1), I32)
    for bit in range(30, -1, -1):
        cand = thr | (1 << bit)
        n_ge = jnp.sum((bits >= cand).astype(F32), axis=1, keepdims=True)
        thr = jnp.where(n_ge >= cap, cand, thr)
    above = bits > thr
    tied = (bits == thr).astype(F32)
    need = cap - jnp.sum(above.astype(F32), axis=1, keepdims=True)
    tied_before = _cumsum_lanes(tied) - tied
    sel = jnp.where(above, 1.0, jnp.where(tied_before < need, tied, 0.0))

    slot = _cumsum_lanes(sel) - sel
    per_tok = jnp.sum(sel, axis=0, keepdims=True)
    per_tok8 = jnp.broadcast_to(per_tok, (8, n))
    start = (_cumsum_lanes(per_tok8) - per_tok8)[0:1]
    start_ref[...] = start.astype(I32)
    cnt_ref[...] = per_tok.astype(I32)
    slot_s[...] = jnp.where(sel > 0.0, slot, -1.0)
    aff_s[...] = aff

    tok = lax.broadcasted_iota(I32, (1, n), 1).astype(F32)
    cc = min(cap, LANES)

    def body(e, chosen_by_earlier):
        srow = slot_s[pl.ds(e, 1), :]
        arow = aff_s[pl.ds(e, 1), :]
        drow = start + chosen_by_earlier
        for c0 in range(0, cap, cc):
            want = (lax.broadcasted_iota(I32, (cc, 1), 0) + c0).astype(F32)
            hit = srow == want
            pick = lambda row: jnp.sum(jnp.where(hit, row, 0.0), axis=1, keepdims=True)
            idx_ref[e, pl.ds(c0, cc), :] = pick(tok).astype(I32)
            dest_ref[e, pl.ds(c0, cc), :] = pick(drow).astype(I32)
            gate_ref[e, pl.ds(c0, cc), :] = pick(arow)
        return chosen_by_earlier + (srow >= 0.0).astype(F32)

    lax.fori_loop(0, e_n, body, jnp.zeros((1, n), F32))


def expert_choice_route(logits_t, *, batch, n, col0):
    e_n = logits_t.shape[0]
    cap = CAPACITY_FACTOR * n // e_n
    cb0 = col0 // n
    per_slot = lambda dt: jax.ShapeDtypeStruct((batch, e_n, cap, 1), dt)
    per_tok = jax.ShapeDtypeStruct((batch, 1, n), I32)
    slot_spec = pl.BlockSpec((None, e_n, cap, 1), lambda b: (b, 0, 0, 0))
    tok_spec = pl.BlockSpec((None, 1, n), lambda b: (b, 0, 0))
    return pl.pallas_call(
        functools.partial(_route_kernel, cap=cap),
        grid=(batch,),
        in_specs=[pl.BlockSpec((e_n, n), lambda b: (0, cb0 + b))],
        out_specs=[slot_spec, slot_spec, slot_spec, tok_spec, tok_spec],
        out_shape=[per_slot(I32), per_slot(I32), per_slot(F32), per_tok, per_tok],
        scratch_shapes=[pltpu.VMEM((e_n, n), F32), pltpu.VMEM((e_n, n), F32)],
        compiler_params=_params("arbitrary"),
        name="expert_choice_route",
    )(logits_t)


ROWS_PER_STEP = 256
ROW_COPY_UNROLL = 8


def _start_rows(n, make_copy):
    def start(r, carry):
        make_copy(r).start()
        return carry

    lax.fori_loop(0, n, start, 0, unroll=ROW_COPY_UNROLL)


def _wait_rows(n, make_copy):
    def wait(r, carry):
        make_copy(r).wait()
        return carry

    lax.fori_loop(0, n, wait, 0, unroll=ROW_COPY_UNROLL)


def _gather_kernel(idx_ref, src_ref, o_ref, stage_ref, sem):
    step, n_steps, rows = pl.program_id(0), pl.num_programs(0), o_ref.shape[0]

    def copies(s):
        slot = s % 2
        return lambda r: pltpu.make_async_copy(src_ref.at[pl.ds(idx_ref[s * rows + r], 1), :],
                                               stage_ref.at[slot, pl.ds(r, 1), :], sem.at[slot])

    @pl.when(step == 0)
    def _():
        _start_rows(rows, copies(step))

    @pl.when(step + 1 < n_steps)
    def _():
        _start_rows(rows, copies(step + 1))

    _wait_rows(rows, copies(step))
    o_ref[...] = stage_ref[step % 2].astype(o_ref.dtype)


def gather_rows(src, idx, out_dtype):
    n, d = idx.shape[0], src.shape[1]
    rows = min(n, ROWS_PER_STEP)
    return pl.pallas_call(
        _gather_kernel,
        grid_spec=pltpu.PrefetchScalarGridSpec(
            num_scalar_prefetch=1,
            grid=(n // rows,),
            in_specs=[pl.BlockSpec(memory_space=pl.ANY)],
            out_specs=pl.BlockSpec((rows, d), lambda i, idx_ref: (i, 0)),
            scratch_shapes=[pltpu.VMEM((2, rows, d), src.dtype), pltpu.SemaphoreType.DMA((2,))],
        ),
        out_shape=jax.ShapeDtypeStruct((n, d), out_dtype),
        compiler_params=_params("arbitrary"),
        name="gather_rows",
    )(idx, src)


def _scatter_kernel(idx_ref, x_ref, o_ref, stage_ref, sem):
    step, n_steps, rows = pl.program_id(0), pl.num_programs(0), x_ref.shape[0]

    def copies(s):
        slot = s % 2
        return lambda r: pltpu.make_async_copy(stage_ref.at[slot, pl.ds(r, 1), :],
                                               o_ref.at[pl.ds(idx_ref[s * rows + r], 1), :], sem.at[slot])

    stage_ref[step % 2] = x_ref[...].astype(stage_ref.dtype)
    _start_rows(rows, copies(step))

    @pl.when(step > 0)
    def _():
        _wait_rows(rows, copies(step - 1))

    @pl.when(step + 1 == n_steps)
    def _():
        _wait_rows(rows, copies(step))


def scatter_rows(x, idx, out_dtype):
    n, d = x.shape
    rows = min(n, ROWS_PER_STEP)
    return pl.pallas_call(
        _scatter_kernel,
        grid_spec=pltpu.PrefetchScalarGridSpec(
            num_scalar_prefetch=1,
            grid=(n // rows,),
            in_specs=[pl.BlockSpec((rows, d), lambda i, idx_ref: (i, 0))],
            out_specs=pl.BlockSpec(memory_space=pl.ANY),
            scratch_shapes=[pltpu.VMEM((2, rows, d), out_dtype), pltpu.SemaphoreType.DMA((2,))],
        ),
        out_shape=jax.ShapeDtypeStruct((n, d), out_dtype),
        compiler_params=_params("arbitrary"),
        name="scatter_rows",
    )(idx, x)


def _ffn_kernel(xg_ref, w1_ref, w3_ref, w2_ref, g_ref, y_ref, hid_s, *, f1, tf):
    s = pl.program_id(1)

    @pl.when(s < f1)
    def _():
        xs = xg_ref[...]
        h1 = jnp.dot(xs, w1_ref[...].astype(BF16), preferred_element_type=F32)
        h3 = jnp.dot(xs, w3_ref[...].astype(BF16), preferred_element_type=F32)
        hid_s[s] = (jax.nn.silu(h1) * h3).astype(BF16)

    @pl.when(s >= f1)
    def _():
        acc = jnp.dot(hid_s[0], w2_ref[0:tf, :].astype(BF16), preferred_element_type=F32)
        for f in range(1, f1):
            acc += jnp.dot(hid_s[f], w2_ref[f * tf:(f + 1) * tf, :].astype(BF16), preferred_element_type=F32)
        y_ref[...] = (acc * g_ref[...]).astype(y_ref.dtype)


def expert_ffn(xg, w1, w3, w2, gate, layer):
    e_n, m, d = xg.shape
    ff = w1.shape[3]
    tf, tn = min(ff, 256), min(d, 512)
    f1, f2 = ff // tf, d // tn
    out_step = lambda s: jnp.maximum(s - f1, 0)
    next_e = lambda e: jnp.minimum(e + 1, e_n - 1)

    def rows_map(e, s):
        return (jnp.where(s < f1, e, next_e(e)), 0, 0)

    def up_map(switch_step):
        def index(e, s):
            switched = s >= f1 + switch_step
            return (layer, jnp.where(switched, next_e(e), e), 0, jnp.where(switched, 0, jnp.minimum(s, f1 - 1)))
        return index

    def down_map(e, s):
        hold = jnp.logical_and(s == 0, e > 0)
        return (layer, jnp.where(hold, e - 1, e), 0, jnp.where(hold, f2 - 1, out_step(s)))

    return pl.pallas_call(
        functools.partial(_ffn_kernel, f1=f1, tf=tf),
        grid=(e_n, f1 + f2),
        in_specs=[
            pl.BlockSpec((None, m, d), rows_map),
            pl.BlockSpec((None, None, d, tf), up_map(f2 // 3)),
            pl.BlockSpec((None, None, d, tf), up_map(2 * f2 // 3)),
            pl.BlockSpec((None, None, ff, tn), down_map),
            pl.BlockSpec((None, m, 1), lambda e, s: (e, 0, 0)),
        ],
        out_specs=pl.BlockSpec((None, m, tn), lambda e, s: (e, 0, out_step(s))),
        out_shape=jax.ShapeDtypeStruct((e_n, m, d), BF16),
        scratch_shapes=[pltpu.VMEM((f1, m, tf), BF16)],
        compiler_params=_params("arbitrary", "arbitrary"),
        name="expert_ffn",
    )(xg, w1, w3, w2, gate)


ITEM_ADD, ITEM_FIRST, ITEM_LAST = 1, 2, 4


def _combine_kernel(tile_ref, chunk_ref, flag_ref, yg_ref, st_ref, ct_ref, x_ref, gt_ref, ng_ref, o_ref, acc_s, *,
                    ch, normalize):
    del tile_ref
    item = pl.program_id(0) * pl.num_programs(1) + pl.program_id(1)
    flags = flag_ref[item]

    @pl.when((flags & ITEM_FIRST) != 0)
    def _():
        acc_s[...] = jnp.zeros_like(acc_s)

    @pl.when((flags & ITEM_ADD) != 0)
    def _():
        pair = chunk_ref[item] * ch + lax.broadcasted_iota(I32, (ch, 1), 0)
        rel = pair - st_ref[...]
        own_t = jnp.where(jnp.logical_and(rel >= 0, rel < ct_ref[...]), 1.0, 0.0).astype(BF16)
        acc_s[...] += lax.dot_general(own_t, yg_ref[...].astype(BF16), (((0,), (0,)), ((), ())),
                                      preferred_element_type=F32)

    @pl.when((flags & ITEM_LAST) != 0)
    def _():
        out = x_ref[...] + gt_ref[0] * acc_s[...]
        o_ref[...] = _rms(out, ng_ref[...]) if normalize else out


def _combine_work_items(start, *, batch, n, tt, pairs, ch):
    tiles, n_chunks = n // tt, pairs // ch
    n_items = tiles + n_chunks
    first = start.reshape(batch, n)[:, ::tt]
    last = jnp.concatenate([first[:, 1:], jnp.full((batch, 1), pairs, I32)], axis=1)
    c_first = jnp.minimum(first // ch, n_chunks - 1)
    c_num = jnp.where(last > first, (last - 1) // ch - first // ch + 1, 0)
    t_items = jnp.maximum(c_num, 1)
    ends = jnp.cumsum(t_items, axis=1)
    k = jnp.arange(n_items, dtype=I32)[None, :]
    tile = jnp.minimum(jnp.sum((ends[:, None, :] <= k[:, :, None]).astype(I32), axis=-1), tiles - 1)
    at = lambda v: jnp.take_along_axis(v, tile, axis=1)
    j = k - at(ends - t_items)
    real = k < ends[:, -1:]
    chunk = at(c_first) + jnp.minimum(j, jnp.maximum(at(c_num) - 1, 0))
    flags = (ITEM_ADD * (real & (j < at(c_num))) + ITEM_FIRST * (real & (j == 0))
             + ITEM_LAST * (real & (j == at(t_items) - 1)))
    flat = lambda v: v.reshape(-1).astype(I32)
    return n_items, flat(tile), flat(chunk), flat(flags)


def combine_residual(yg, start, cnt, x, mod3, rows, k_gate, norm_g, *, batch, n, row0, pairs, pair_row0, normalize):
    d = x.shape[1]
    tt = min(n, 256)
    ch = min(pairs, 512)
    tiles, n_chunks = n // tt, pairs // ch
    n_items, item_tile, item_chunk, item_flags = _combine_work_items(start, batch=batch, n=n, tt=tt, pairs=pairs,
                                                                     ch=ch)
    mod_row = rows.mod_row(tt)
    rt0, chunk0 = row0 // tt, pair_row0 // ch

    def tile_of(b, k, tile_ref):
        return tile_ref[b * n_items + k]

    yg_map = lambda b, k, tile_ref, chunk_ref, flag_ref: (chunk0 + b * n_chunks + chunk_ref[b * n_items + k], 0)
    row_map = lambda b, k, tile_ref, chunk_ref, flag_ref: (rt0 + b * tiles + tile_of(b, k, tile_ref), 0)
    tok_map = lambda b, k, tile_ref, chunk_ref, flag_ref: (b, 0, tile_of(b, k, tile_ref))
    gate_map = lambda b, k, tile_ref, chunk_ref, flag_ref: (
        mod_row(rt0 + b * tiles + tile_of(b, k, tile_ref)), 0, k_gate)
    return pl.pallas_call(
        functools.partial(_combine_kernel, ch=ch, normalize=normalize),
        grid_spec=pltpu.PrefetchScalarGridSpec(
            num_scalar_prefetch=3,
            grid=(batch, n_items),
            in_specs=[
                pl.BlockSpec((ch, d), yg_map),
                pl.BlockSpec((None, 1, tt), tok_map),
                pl.BlockSpec((None, 1, tt), tok_map),
                pl.BlockSpec((tt, d), row_map),
                pl.BlockSpec((1, 1, d), gate_map),
                pl.BlockSpec((1, d), lambda b, k, tile_ref, chunk_ref, flag_ref: (0, 0)),
            ],
            out_specs=pl.BlockSpec((tt, d), row_map),
            scratch_shapes=[pltpu.VMEM((tt, d), F32)],
        ),
        out_shape=jax.ShapeDtypeStruct(x.shape, F32),
        input_output_aliases={6: 0},
        compiler_params=_params("arbitrary", "arbitrary"),
        name="combine_residual",
    )(item_tile, item_chunk, item_flags, yg, start, cnt, x, mod3, norm_g.reshape(1, d))


def expert_choice_ffn_residual(x, h, logits_t, mod3, rows, k_gate, w1, w3, w2, layer, norm_g, *, batch, sets,
                               normalize):
    e_n = logits_t.shape[0]
    d = x.shape[1]
    boff = jnp.arange(batch, dtype=I32)[:, None, None]
    routed, src_rows, pair_rows, gate_cols, pair_row0 = [], [], [], [], 0
    for n, row0 in sets:
        idx, dest, gate, start, cnt = expert_choice_route(logits_t, batch=batch, n=n, col0=row0)
        cap = idx.shape[2]
        pairs = e_n * cap
        src_rows.append(jnp.transpose(idx[..., 0] + row0 + boff * n, (1, 0, 2)).reshape(e_n, batch * cap))
        pair_rows.append(jnp.transpose(dest[..., 0] + pair_row0 + boff * pairs, (1, 0, 2)).reshape(e_n, batch * cap))
        gate_cols.append(jnp.transpose(gate, (1, 0, 2, 3)).reshape(e_n, batch * cap, 1))
        routed.append((start, cnt, n, row0, pairs, pair_row0))
        pair_row0 += batch * pairs
    src_rows = jnp.concatenate(src_rows, axis=1)
    m = src_rows.shape[1]
    xg = gather_rows(h, src_rows.reshape(-1), BF16).reshape(e_n, m, d)
    y = expert_ffn(xg, w1, w3, w2, jnp.concatenate(gate_cols, axis=1), layer).reshape(e_n * m, d)
    yg = scatter_rows(y, jnp.concatenate(pair_rows, axis=1).reshape(-1), F32)
    for start, cnt, n, row0, pairs, pair_row0 in routed:
        x = combine_residual(yg, start, cnt, x, mod3, rows, k_gate, norm_g, batch=batch, n=n, row0=row0,
                             pairs=pairs, pair_row0=pair_row0, normalize=normalize)
    return x


def kernel(x, c, ctx, c_ctx, ada_w, ada_b, norm1_g, norm2_g, w_in, na_rpb, gmlp_norm_g, gmlp_ws, gmlp_bs,
           w_branch_a, w_branch_b, w_out, router_w, exp_w1, exp_w3, exp_w2, final_norm_g):
    batch, n_lat, d = x.shape
    n_ctx = ctx.shape[1]
    depth = ada_w.shape[0]
    rows = Rows(batch, n_lat, n_ctx)
    lat_rows, all_rows = rows.lat_rows, rows.all_rows
    off_k, off_v, off_u = NA_WIDTH, 2 * NA_WIDTH, 3 * NA_WIDTH
    off_gv = off_u + GMLP_WIDTH
    off_ga = off_gv + GMLP_WIDTH
    off_gb = off_ga + d
    in_cols = off_gb + d

    x_head, x_tail = x.reshape(lat_rows, d), ctx.reshape(batch * n_ctx, d)
    cv = jnp.zeros((MOD_ROWS, d), F32).at[:batch].set(c).at[batch].set(c_ctx)

    for layer in range(depth):
        last = layer == depth - 1
        nrows = lat_rows if last else all_rows
        mod3 = ada_mod(cv, ada_w, ada_b, layer).reshape(MOD_ROWS, 1, N_MOD * d)

        h = norm_mod(x_head, x_tail, norm1_g, layer, mod3, rows, 0, 1, 0, nrows, BF16)
        qkv = rest = matmul(h, w_in, layer, row0=0, nrows=nrows, col0=0, ncols=in_cols, out_dtype=BF16,
                            scaled_cols=NA_WIDTH, scale=NA_HEAD_DIM ** -0.5)
        bias_table = build_col_table(na_rpb[layer])
        if last:
            h_ctx = norm_mod(x_head, x_tail, norm1_g, layer, mod3, rows, 0, 1, lat_rows, batch * n_ctx, BF16)
            kvc = matmul(h_ctx, w_in, layer, row0=0, nrows=batch * n_ctx, col0=off_k, ncols=2 * NA_WIDTH,
                         out_dtype=BF16)
            o_a = neighbourhood_attention(qkv, kvc, bias_table, batch=batch, n_lat=n_lat, n_ctx=n_ctx,
                                          kc_row0=0, kc_col0=0, vc_col0=NA_WIDTH)
        else:
            o_lat = neighbourhood_attention(qkv, qkv, bias_table, batch=batch, n_lat=n_lat, n_ctx=n_ctx,
                                            kc_row0=lat_rows, kc_col0=off_k, vc_col0=off_v)
            o_ctx = context_attention(qkv, batch=batch, n_ctx=n_ctx, row0=lat_rows)
            o_a = jnp.concatenate([o_lat, o_ctx], axis=0)
        o_b = spatial_gating(rest, gmlp_norm_g, gmlp_ws, gmlp_bs, layer, nrows=nrows, u_col0=off_u, gv_col0=off_gv)
        y = gated_merge(o_a, o_b, w_branch_a, w_branch_b, rest, layer, nrows=nrows, ga_col0=off_ga, gb_col0=off_gb)
        stream = outproj_residual(y, w_out, x_head, x_tail, mod3, layer, rows, 2, nrows=nrows)

        h2, logits_t = norm_mod_router(stream, norm2_g, layer, mod3, rows, 3, 4, router_w, nrows)
        sets = [(n_lat, 0)] if last else [(n_lat, 0), (n_ctx, lat_rows)]
        stream = expert_choice_ffn_residual(stream, h2, logits_t, mod3, rows, 5, exp_w1, exp_w3, exp_w2, layer,
                                            final_norm_g, batch=batch, sets=sets, normalize=last)
        x_head = x_tail = stream

    return stream.reshape(batch, n_lat, d)
```

```python
import functools

import jax
import jax.numpy as jnp
from jax import lax
from jax.experimental import pallas as pl
from jax.experimental.pallas import tpu as pltpu

F32, BF16, I32 = jnp.float32, jnp.bfloat16, jnp.int32

GRID_W = 64
NA_HEADS = 16
NA_HEAD_DIM = 128
NA_WIDTH = NA_HEADS * NA_HEAD_DIM
WIN_H = 8
WIN_W = 16
GMLP_GROUPS = 16
GMLP_WIDTH = 2048
CHUNK = 128
N_EXPERTS = 16
CAPACITY_FACTOR = 2
N_MOD = 6
EPS = 1e-6
MASK_BIAS = -1e30

V7X_VMEM_LIMIT_BYTES = 56 * 1024 * 1024
LANES = 128
MOD_ROWS = 8
NORM_ROW_TILE = 512


def _params(*sem):
    return pltpu.CompilerParams(dimension_semantics=sem, vmem_limit_bytes=V7X_VMEM_LIMIT_BYTES)


def _cast_weight(w_ref, wb_ref):
    rows = w_ref.shape[0]
    step = min(rows, 512)

    def body(i, carry):
        r = pl.multiple_of(i * step, step)
        wb_ref[pl.ds(r, step), :] = w_ref[pl.ds(r, step), :].astype(BF16)
        return carry

    lax.fori_loop(0, rows // step, body, 0)


def _ada_kernel(c_ref, w_ref, b_ref, o_ref):
    c = c_ref[...]
    s = c * jax.nn.sigmoid(c)
    o_ref[...] = jnp.dot(s.astype(BF16), w_ref[...].astype(BF16), preferred_element_type=F32) + b_ref[...]


def ada_mod(cv, ada_w, ada_b, layer):
    _, d, m = ada_w.shape
    tn = min(m, 512)
    return pl.pallas_call(
        _ada_kernel,
        grid=(m // tn,),
        in_specs=[
            pl.BlockSpec((MOD_ROWS, d), lambda j: (0, 0)),
            pl.BlockSpec((None, d, tn), lambda j: (layer, 0, j)),
            pl.BlockSpec((None, 1, tn), lambda j: (layer, 0, j)),
        ],
        out_specs=pl.BlockSpec((MOD_ROWS, tn), lambda j: (0, j)),
        out_shape=jax.ShapeDtypeStruct((MOD_ROWS, m), F32),
        compiler_params=_params("arbitrary"),
        name="ada_mod",
    )(cv, ada_w, ada_b.reshape(ada_b.shape[0], 1, m))


class Rows:
    def __init__(self, batch, n_lat, n_ctx):
        self.batch, self.n_lat, self.n_ctx = batch, n_lat, n_ctx
        self.lat_rows = batch * n_lat
        self.all_rows = self.lat_rows + batch * n_ctx

    def mod_row(self, tile_rows):
        lat_tiles = self.lat_rows // tile_rows
        per_batch = self.n_lat // tile_rows
        batch = self.batch

        def f(i):
            return jnp.where(i < lat_tiles, i // per_batch, batch)

        return f


def _mod_spec(mod_row, k, d):
    return pl.BlockSpec((1, 1, d), lambda i: (mod_row(i), 0, k))


def _rms(x, g):
    return x * lax.rsqrt(jnp.mean(x * x, axis=-1, keepdims=True) + EPS) * g


def _two_source_specs(block, head_tiles, tile_of):
    def head(*ids):
        t, c = tile_of(*ids)
        return (jnp.minimum(t, head_tiles - 1), c)

    def tail(*ids):
        t, c = tile_of(*ids)
        return (jnp.maximum(t - head_tiles, 0), c)

    return pl.BlockSpec(block, head), pl.BlockSpec(block, tail)


def _norm_mod_kernel(xh_ref, xt_ref, g_ref, sh_ref, sc_ref, o_ref, *, head_tiles, t0):
    def emit(x_ref):
        y = _rms(x_ref[...], g_ref[...])
        o_ref[...] = (y * (1.0 + sc_ref[0]) + sh_ref[0]).astype(o_ref.dtype)

    in_head = pl.program_id(0) + t0 < head_tiles
    pl.when(in_head)(lambda: emit(xh_ref))
    pl.when(jnp.logical_not(in_head))(lambda: emit(xt_ref))


def norm_mod(x_head, x_tail, g, layer, mod3, rows, k_shift, k_scale, row0, nrows, out_dtype):
    d = x_head.shape[1]
    tr = NORM_ROW_TILE
    mod_row = rows.mod_row(tr)
    t0 = row0 // tr
    head_tiles = x_head.shape[0] // tr
    head_spec, tail_spec = _two_source_specs((tr, d), head_tiles, lambda i: (i + t0, 0))
    return pl.pallas_call(
        functools.partial(_norm_mod_kernel, head_tiles=head_tiles, t0=t0),
        grid=(nrows // tr,),
        in_specs=[
            head_spec,
            tail_spec,
            pl.BlockSpec((None, 1, d), lambda i: (layer, 0, 0)),
            pl.BlockSpec((1, 1, d), lambda i: (mod_row(i + t0), 0, k_shift)),
            pl.BlockSpec((1, 1, d), lambda i: (mod_row(i + t0), 0, k_scale)),
        ],
        out_specs=pl.BlockSpec((tr, d), lambda i: (i, 0)),
        out_shape=jax.ShapeDtypeStruct((nrows, d), out_dtype),
        compiler_params=_params("arbitrary"),
        name="norm_mod",
    )(x_head, x_tail, g.reshape(g.shape[0], 1, d), mod3, mod3)


WEIGHT_TILE_BUFFERS = pl.Buffered(1)
MATMUL_ROW_TILES = 8

def _mm_kernel(a_ref, w_ref, o_ref, wb_ref, *, scaled_tiles, scale):
    @pl.when(pl.program_id(1) == 0)
    def _():
        _cast_weight(w_ref, wb_ref)

    acc = jnp.dot(a_ref[...], wb_ref[...], preferred_element_type=F32)
    if scaled_tiles:
        acc = acc * jnp.where(pl.program_id(0) < scaled_tiles, scale, 1.0)
    o_ref[...] = acc.astype(o_ref.dtype)


def matmul(a, w, layer, *, row0, nrows, col0, ncols, out_dtype, scaled_cols=0, scale=1.0, tm=512, tn=1024):
    k = a.shape[1]
    if nrows % (MATMUL_ROW_TILES * 16) == 0 and nrows // MATMUL_ROW_TILES > tm:
        tm = nrows // MATMUL_ROW_TILES
    tm, tn = min(tm, nrows), min(tn, ncols)
    r0, c0 = row0 // tm, col0 // tn
    assert scaled_cols % tn == 0 and row0 % tm == 0
    return pl.pallas_call(
        functools.partial(_mm_kernel, scaled_tiles=scaled_cols // tn, scale=scale),
        grid=(ncols // tn, nrows // tm),
        in_specs=[
            pl.BlockSpec((tm, k), lambda j, i: (i + r0, 0)),
            pl.BlockSpec((None, k, tn), lambda j, i: (layer, 0, j + c0), pipeline_mode=WEIGHT_TILE_BUFFERS),
        ],
        out_specs=pl.BlockSpec((tm, tn), lambda j, i: (i, j)),
        out_shape=jax.ShapeDtypeStruct((nrows, ncols), out_dtype),
        scratch_shapes=[pltpu.VMEM((k, tn), BF16)],
        compiler_params=_params("arbitrary", "arbitrary"),
        name="matmul",
    )(a, w)


def _merge_kernel(oa_ref, ob_ref, wa_ref, wb_ref, ga_ref, gb_ref, y_ref, wab_ref, wbb_ref):
    @pl.when(pl.program_id(1) == 0)
    def _():
        _cast_weight(wa_ref, wab_ref)
        _cast_weight(wb_ref, wbb_ref)

    ya = jnp.dot(oa_ref[...], wab_ref[...], preferred_element_type=F32)
    yb = jnp.dot(ob_ref[...], wbb_ref[...], preferred_element_type=F32)
    y = jax.nn.sigmoid(ga_ref[...].astype(F32)) * ya + jax.nn.sigmoid(gb_ref[...].astype(F32)) * yb
    y_ref[...] = y.astype(y_ref.dtype)


def gated_merge(o_a, o_b, w_a, w_b, gates, layer, *, nrows, ga_col0, gb_col0, tm=512, tn=1024):
    ka, kb = o_a.shape[1], o_b.shape[1]
    d = w_a.shape[2]
    tm, tn = min(tm, nrows), min(tn, d)
    ca, cb = ga_col0 // tn, gb_col0 // tn
    return pl.pallas_call(
        _merge_kernel,
        grid=(d // tn, nrows // tm),
        in_specs=[
            pl.BlockSpec((tm, ka), lambda j, i: (i, 0)),
            pl.BlockSpec((tm, kb), lambda j, i: (i, 0)),
            pl.BlockSpec((None, ka, tn), lambda j, i: (layer, 0, j), pipeline_mode=WEIGHT_TILE_BUFFERS),
            pl.BlockSpec((None, kb, tn), lambda j, i: (layer, 0, j), pipeline_mode=WEIGHT_TILE_BUFFERS),
            pl.BlockSpec((tm, tn), lambda j, i: (i, j + ca)),
            pl.BlockSpec((tm, tn), lambda j, i: (i, j + cb)),
        ],
        out_specs=pl.BlockSpec((tm, tn), lambda j, i: (i, j)),
        out_shape=jax.ShapeDtypeStruct((nrows, d), BF16),
        scratch_shapes=[pltpu.VMEM((ka, tn), BF16), pltpu.VMEM((kb, tn), BF16)],
        compiler_params=_params("arbitrary", "arbitrary"),
        name="gated_merge",
    )(o_a, o_b, w_a, w_b, gates, gates)


def _outproj_kernel(y_ref, w_ref, xh_ref, xt_ref, gt_ref, o_ref, wb_ref, *, head_tiles):
    @pl.when(pl.program_id(1) == 0)
    def _():
        _cast_weight(w_ref, wb_ref)

    upd = gt_ref[0] * jnp.dot(y_ref[...], wb_ref[...], preferred_element_type=F32)

    def emit(x_ref):
        o_ref[...] = x_ref[...] + upd

    in_head = pl.program_id(1) < head_tiles
    pl.when(in_head)(lambda: emit(xh_ref))
    pl.when(jnp.logical_not(in_head))(lambda: emit(xt_ref))


def outproj_residual(y, w_out, x_head, x_tail, mod3, layer, rows, k_gate, *, nrows, tm=512, tn=1024):
    k = y.shape[1]
    d = w_out.shape[2]
    tm, tn = min(tm, nrows), min(tn, d)
    mod_row = rows.mod_row(tm)
    gate_col0 = k_gate * (d // tn)
    head_tiles = x_head.shape[0] // tm
    head_spec, tail_spec = _two_source_specs((tm, tn), head_tiles, lambda j, i: (i, j))
    return pl.pallas_call(
        functools.partial(_outproj_kernel, head_tiles=head_tiles),
        grid=(d // tn, nrows // tm),
        in_specs=[
            pl.BlockSpec((tm, k), lambda j, i: (i, 0)),
            pl.BlockSpec((None, k, tn), lambda j, i: (layer, 0, j), pipeline_mode=WEIGHT_TILE_BUFFERS),
            head_spec,
            tail_spec,
            pl.BlockSpec((1, 1, tn), lambda j, i: (mod_row(i), 0, j + gate_col0)),
        ],
        out_specs=pl.BlockSpec((tm, tn), lambda j, i: (i, j)),
        out_shape=jax.ShapeDtypeStruct((nrows, d), F32),
        scratch_shapes=[pltpu.VMEM((k, tn), BF16)],
        compiler_params=_params("arbitrary", "arbitrary"),
        name="outproj_residual",
    )(y, w_out, x_head, x_tail, mod3)


Q_ROWS = 4
BAND_ROWS = WIN_H + Q_ROWS
N_PATTERNS = 3


def build_col_table(rpb):
    cols = jnp.arange(GRID_W)
    col_start = jnp.clip(cols - WIN_W // 2, 0, GRID_W - WIN_W)
    col_ok = (cols[None, :] >= col_start[:, None]) & (cols[None, :] < col_start[:, None] + WIN_W)
    col_off = cols[None, :] - cols[:, None] + (WIN_W - 1)
    pick = (col_off[None] == jnp.arange(2 * WIN_W - 1)[:, None, None]).astype(F32)
    t = jnp.einsum("hrj,jqk->hrqk", rpb.astype(F32), pick, precision=lax.Precision.HIGHEST)
    return jnp.where(col_ok[None, None], t, MASK_BIAS)


def _band_patterns(grid_rows):
    half = WIN_H // 2
    groups = grid_rows // Q_ROWS
    assert grid_rows % Q_ROWS == 0 and grid_rows >= BAND_ROWS
    patterns = []
    for g in (0, min(1, groups - 1), groups - 1):
        band_start = min(max(g * Q_ROWS - half, 0), grid_rows - BAND_ROWS)
        pattern = []
        for a in range(Q_ROWS):
            q_row = g * Q_ROWS + a
            win_start = min(max(q_row - half, 0), grid_rows - WIN_H)
            pattern.append([band_start + i - q_row + WIN_H - 1 if win_start <= band_start + i < win_start + WIN_H
                            else None for i in range(BAND_ROWS)])
        patterns.append(pattern)
    return patterns


def _softmax_pv(parts):
    m = functools.reduce(jnp.maximum, [jnp.max(s, axis=-1, keepdims=True) for s, _ in parts])
    es = [jnp.exp(s - m) for s, _ in parts]
    denom = functools.reduce(jnp.add, [jnp.sum(e, axis=-1, keepdims=True) for e in es])
    acc = functools.reduce(jnp.add, [jnp.dot(e.astype(BF16), v, preferred_element_type=F32)
                                     for e, (_, v) in zip(es, parts)])
    return acc / denom


def _qk(q, k):
    return lax.dot_general(q, k, (((1,), (1,)), ((), ())), preferred_element_type=F32)


def _natten_kernel(q_ref, k_ref, v_ref, kc_ref, vc_ref, ct_ref, o_ref, bt_ref, *, grid_rows):
    groups = grid_rows // Q_ROWS
    nq, band = Q_ROWS * GRID_W, BAND_ROWS * GRID_W
    kc = kc_ref[...]
    vc = vc_ref[...]

    @pl.when(jnp.logical_and(pl.program_id(0) == 0, pl.program_id(1) == 0))
    def _():
        bt_ref[...] = jnp.full(bt_ref.shape, MASK_BIAS, F32)

    @pl.when(pl.program_id(1) == 0)
    def _():
        for p, pattern in enumerate(_band_patterns(grid_rows)):
            for a, row in enumerate(pattern):
                for i, off in enumerate(row):
                    if off is not None:
                        bt_ref[p, a * GRID_W:(a + 1) * GRID_W, i * GRID_W:(i + 1) * GRID_W] = ct_ref[off]

    def body(g, carry):
        band_start = jnp.clip(g * Q_ROWS - WIN_H // 2, 0, grid_rows - BAND_ROWS)
        pattern = jnp.where(g == 0, 0, jnp.where(g == groups - 1, 2, 1))
        q0 = pl.multiple_of(g * nq, nq)
        k0 = pl.multiple_of(band_start * GRID_W, GRID_W)
        q = q_ref[pl.ds(q0, nq), :]
        kb = k_ref[pl.ds(k0, band), :]
        vb = v_ref[pl.ds(k0, band), :]
        s_loc = _qk(q, kb) + bt_ref[pattern]
        s_ctx = _qk(q, kc)
        o_ref[pl.ds(q0, nq), :] = _softmax_pv([(s_loc, vb), (s_ctx, vc)]).astype(o_ref.dtype)
        return carry

    lax.fori_loop(0, groups, body, 0, unroll=4)


def neighbourhood_attention(qkv, kvc, col_table, *, batch, n_lat, n_ctx, kc_row0, kc_col0, vc_col0):
    dh, h = NA_HEAD_DIM, NA_HEADS
    grid_rows = n_lat // GRID_W
    rb0, ck, cv = kc_row0 // n_ctx, kc_col0 // dh, vc_col0 // dh
    return pl.pallas_call(
        functools.partial(_natten_kernel, grid_rows=grid_rows),
        grid=(h, batch),
        in_specs=[
            pl.BlockSpec((n_lat, dh), lambda hh, b: (b, hh)),
            pl.BlockSpec((n_lat, dh), lambda hh, b: (b, h + hh)),
            pl.BlockSpec((n_lat, dh), lambda hh, b: (b, 2 * h + hh)),
            pl.BlockSpec((n_ctx, dh), lambda hh, b: (rb0 + b, ck + hh)),
            pl.BlockSpec((n_ctx, dh), lambda hh, b: (rb0 + b, cv + hh)),
            pl.BlockSpec((None, 2 * WIN_H - 1, GRID_W, GRID_W), lambda hh, b: (hh, 0, 0, 0)),
        ],
        out_specs=pl.BlockSpec((n_lat, dh), lambda hh, b: (b, hh)),
        out_shape=jax.ShapeDtypeStruct((batch * n_lat, NA_WIDTH), BF16),
        scratch_shapes=[pltpu.VMEM((N_PATTERNS, Q_ROWS * GRID_W, BAND_ROWS * GRID_W), F32)],
        compiler_params=_params("arbitrary", "arbitrary"),
        name="neighbourhood_attention",
    )(qkv, qkv, qkv, kvc, kvc, col_table)


def _ctx_attn_kernel(q_ref, k_ref, v_ref, o_ref):
    o_ref[...] = _softmax_pv([(_qk(q_ref[...], k_ref[...]), v_ref[...])]).astype(o_ref.dtype)


def context_attention(qkv, *, batch, n_ctx, row0):
    dh, h = NA_HEAD_DIM, NA_HEADS
    rb0 = row0 // n_ctx
    return pl.pallas_call(
        _ctx_attn_kernel,
        grid=(batch, h),
        in_specs=[
            pl.BlockSpec((n_ctx, dh), lambda b, hh: (rb0 + b, hh)),
            pl.BlockSpec((n_ctx, dh), lambda b, hh: (rb0 + b, h + hh)),
            pl.BlockSpec((n_ctx, dh), lambda b, hh: (rb0 + b, 2 * h + hh)),
        ],
        out_specs=pl.BlockSpec((n_ctx, dh), lambda b, hh: (b, hh)),
        out_shape=jax.ShapeDtypeStruct((batch * n_ctx, NA_WIDTH), BF16),
        compiler_params=_params("arbitrary", "arbitrary"),
        name="context_attention",
    )(qkv, qkv, qkv)


def _gmlp_kernel(u_ref, gv_ref, ng_ref, ws_ref, bst_ref, o_ref):
    gd = GMLP_WIDTH // GMLP_GROUPS
    for c in range(u_ref.shape[0] // CHUNK):
        rows = slice(c * CHUNK, (c + 1) * CHUNK)
        v = jax.nn.gelu(gv_ref[rows, :].astype(F32))
        vc = v - jnp.mean(v, axis=-1, keepdims=True)
        y = vc * lax.rsqrt(jnp.mean(vc * vc, axis=-1, keepdims=True) + EPS) * ng_ref[...]
        yb = y.astype(BF16)
        for g in range(GMLP_GROUPS):
            sl = slice(g * gd, (g + 1) * gd)
            mixed = jnp.dot(ws_ref[g].astype(BF16), yb[:, sl], preferred_element_type=F32) + bst_ref[:, g:g + 1]
            o_ref[rows, sl] = (jax.nn.gelu(u_ref[rows, sl].astype(F32)) * mixed).astype(o_ref.dtype)


def spatial_gating(p, norm_g, ws, bs, layer, *, nrows, u_col0, gv_col0):
    w = GMLP_WIDTH
    bst = jnp.transpose(bs[layer])
    tr = 2 * CHUNK
    return pl.pallas_call(
        _gmlp_kernel,
        grid=(nrows // tr,),
        in_specs=[
            pl.BlockSpec((tr, w), lambda i: (i, u_col0 // w)),
            pl.BlockSpec((tr, w), lambda i: (i, gv_col0 // w)),
            pl.BlockSpec((None, 1, w), lambda i: (layer, 0, 0)),
            pl.BlockSpec((None, GMLP_GROUPS, CHUNK, CHUNK), lambda i: (layer, 0, 0, 0)),
            pl.BlockSpec((CHUNK, GMLP_GROUPS), lambda i: (0, 0)),
        ],
        out_specs=pl.BlockSpec((tr, w), lambda i: (i, 0)),
        out_shape=jax.ShapeDtypeStruct((nrows, w), BF16),
        compiler_params=_params("arbitrary"),
        name="spatial_gating",
    )(p, p, norm_g.reshape(norm_g.shape[0], 1, w), ws, bst)


def _split_bf16(x):
    hi = x.astype(BF16)
    return hi, (x - hi.astype(F32)).astype(BF16)


def _norm_router_kernel(x_ref, g_ref, sh_ref, sc_ref, rw_ref, h_ref, lg_ref):
    h = _rms(x_ref[...], g_ref[...]) * (1.0 + sc_ref[0]) + sh_ref[0]
    h_ref[...] = h
    hh, hl = _split_bf16(h)
    wh, wl = _split_bf16(rw_ref[...])
    lg = (jnp.dot(hh, wh, preferred_element_type=F32) + jnp.dot(hl, wh, preferred_element_type=F32)
          + jnp.dot(hh, wl, preferred_element_type=F32))
    lg_ref[...] = jnp.transpose(lg)[:N_EXPERTS, :]


def norm_mod_router(x, g, layer, mod3, rows, k_shift, k_scale, router_w, nrows):
    d = x.shape[1]
    tr = NORM_ROW_TILE
    mod_row = rows.mod_row(tr)
    rw = jnp.pad(router_w[layer], ((0, 0), (0, LANES - N_EXPERTS)))
    return pl.pallas_call(
        _norm_router_kernel,
        grid=(nrows // tr,),
        in_specs=[
            pl.BlockSpec((tr, d), lambda i: (i, 0)),
            pl.BlockSpec((None, 1, d), lambda i: (layer, 0, 0)),
            pl.BlockSpec((1, 1, d), lambda i: (mod_row(i), 0, k_shift)),
            pl.BlockSpec((1, 1, d), lambda i: (mod_row(i), 0, k_scale)),
            pl.BlockSpec((d, LANES), lambda i: (0, 0)),
        ],
        out_specs=[pl.BlockSpec((tr, d), lambda i: (i, 0)), pl.BlockSpec((N_EXPERTS, tr), lambda i: (0, i))],
        out_shape=[jax.ShapeDtypeStruct((nrows, d), F32), jax.ShapeDtypeStruct((N_EXPERTS, nrows), F32)],
        compiler_params=_params("arbitrary"),
        name="norm_mod_router",
    )(x, g.reshape(g.shape[0], 1, d), mod3, mod3, rw)


def _cumsum_lanes(x):
    r, n = x.shape
    blk = min(n, LANES)
    tri = (lax.broadcasted_iota(I32, (blk, blk), 0) <= lax.broadcasted_iota(I32, (blk, blk), 1)).astype(BF16)
    off = jnp.zeros((r, 1), F32)
    out = []
    for j in range(n // blk):
        cs = jnp.dot(x[:, j * blk:(j + 1) * blk].astype(BF16), tri, preferred_element_type=F32) + off
        out.append(cs)
        off = cs[:, blk - 1:blk]
    return jnp.concatenate(out, axis=1) if len(out) > 1 else out[0]


def _route_kernel(lg_ref, idx_ref, dest_ref, gate_ref, start_ref, cnt_ref, slot_s, aff_s, *, cap):
    e_n, n = lg_ref.shape
    lg = lg_ref[...]
    ex = jnp.exp(lg - jnp.max(lg, axis=0, keepdims=True))
    aff = ex / jnp.sum(ex, axis=0, keepdims=True)
    bits = lax.bitcast_convert_type(aff, I32)

    thr = jnp.zeros((e_n, 1), I32)
    for bit in range(30, -1, -1):
        cand = thr | (1 << bit)
        n_ge = jnp.sum((bits >= cand).astype(F32), axis=1, keepdims=True)
        thr = jnp.where(n_ge >= cap, cand, thr)
    above = bits > thr
    tied = (bits == thr).astype(F32)
    need = cap - jnp.sum(above.astype(F32), axis=1, keepdims=True)
    tied_before = _cumsum_lanes(tied) - tied
    sel = jnp.where(above, 1.0, jnp.where(tied_before < need, tied, 0.0))

    slot = _cumsum_lanes(sel) - sel
    per_tok = jnp.sum(sel, axis=0, keepdims=True)
    per_tok8 = jnp.broadcast_to(per_tok, (8, n))
    start = (_cumsum_lanes(per_tok8) - per_tok8)[0:1]
    start_ref[...] = start.astype(I32)
    cnt_ref[...] = per_tok.astype(I32)
    slot_s[...] = jnp.where(sel > 0.0, slot, -1.0)
    aff_s[...] = aff

    tok = lax.broadcasted_iota(I32, (1, n), 1).astype(F32)
    cc = min(cap, LANES)

    def body(e, chosen_by_earlier):
        srow = slot_s[pl.ds(e, 1), :]
        arow = aff_s[pl.ds(e, 1), :]
        drow = start + chosen_by_earlier
        for c0 in range(0, cap, cc):
            want = (lax.broadcasted_iota(I32, (cc, 1), 0) + c0).astype(F32)
            hit = srow == want
            pick = lambda row: jnp.sum(jnp.where(hit, row, 0.0), axis=1, keepdims=True)
            idx_ref[e, pl.ds(c0, cc), :] = pick(tok).astype(I32)
            dest_ref[e, pl.ds(c0, cc), :] = pick(drow).astype(I32)
            gate_ref[e, pl.ds(c0, cc), :] = pick(arow)
        return chosen_by_earlier + (srow >= 0.0).astype(F32)

    lax.fori_loop(0, e_n, body, jnp.zeros((1, n), F32))


def expert_choice_route(logits_t, *, batch, n, col0):
    e_n = logits_t.shape[0]
    cap = CAPACITY_FACTOR * n // e_n
    cb0 = col0 // n
    per_slot = lambda dt: jax.ShapeDtypeStruct((batch, e_n, cap, 1), dt)
    per_tok = jax.ShapeDtypeStruct((batch, 1, n), I32)
    slot_spec = pl.BlockSpec((None, e_n, cap, 1), lambda b: (b, 0, 0, 0))
    tok_spec = pl.BlockSpec((None, 1, n), lambda b: (b, 0, 0))
    return pl.pallas_call(
        functools.partial(_route_kernel, cap=cap),
        grid=(batch,),
        in_specs=[pl.BlockSpec((e_n, n), lambda b: (0, cb0 + b))],
        out_specs=[slot_spec, slot_spec, slot_spec, tok_spec, tok_spec],
        out_shape=[per_slot(I32), per_slot(I32), per_slot(F32), per_tok, per_tok],
        scratch_shapes=[pltpu.VMEM((e_n, n), F32), pltpu.VMEM((e_n, n), F32)],
        compiler_params=_params("arbitrary"),
        name="expert_choice_route",
    )(logits_t)


ROWS_PER_STEP = 256
ROW_COPY_UNROLL = 8


def _start_rows(n, make_copy):
    def start(r, carry):
        make_copy(r).start()
        return carry

    lax.fori_loop(0, n, start, 0, unroll=ROW_COPY_UNROLL)


def _wait_rows(n, make_copy):
    def wait(r, carry):
        make_copy(r).wait()
        return carry

    lax.fori_loop(0, n, wait, 0, unroll=ROW_COPY_UNROLL)


def _gather_kernel(idx_ref, src_ref, o_ref, stage_ref, sem):
    step, n_steps, rows = pl.program_id(0), pl.num_programs(0), o_ref.shape[0]

    def copies(s):
        slot = s % 2
        return lambda r: pltpu.make_async_copy(src_ref.at[pl.ds(idx_ref[s * rows + r], 1), :],
                                               stage_ref.at[slot, pl.ds(r, 1), :], sem.at[slot])

    @pl.when(step == 0)
    def _():
        _start_rows(rows, copies(step))

    @pl.when(step + 1 < n_steps)
    def _():
        _start_rows(rows, copies(step + 1))

    _wait_rows(rows, copies(step))
    o_ref[...] = stage_ref[step % 2].astype(o_ref.dtype)


def gather_rows(src, idx, out_dtype):
    n, d = idx.shape[0], src.shape[1]
    rows = min(n, ROWS_PER_STEP)
    return pl.pallas_call(
        _gather_kernel,
        grid_spec=pltpu.PrefetchScalarGridSpec(
            num_scalar_prefetch=1,
            grid=(n // rows,),
            in_specs=[pl.BlockSpec(memory_space=pl.ANY)],
            out_specs=pl.BlockSpec((rows, d), lambda i, idx_ref: (i, 0)),
            scratch_shapes=[pltpu.VMEM((2, rows, d), src.dtype), pltpu.SemaphoreType.DMA((2,))],
        ),
        out_shape=jax.ShapeDtypeStruct((n, d), out_dtype),
        compiler_params=_params("arbitrary"),
        name="gather_rows",
    )(idx, src)


def _scatter_kernel(idx_ref, x_ref, o_ref, stage_ref, sem):
    step, n_steps, rows = pl.program_id(0), pl.num_programs(0), x_ref.shape[0]

    def copies(s):
        slot = s % 2
        return lambda r: pltpu.make_async_copy(stage_ref.at[slot, pl.ds(r, 1), :],
                                               o_ref.at[pl.ds(idx_ref[s * rows + r], 1), :], sem.at[slot])

    stage_ref[step % 2] = x_ref[...].astype(stage_ref.dtype)
    _start_rows(rows, copies(step))

    @pl.when(step > 0)
    def _():
        _wait_rows(rows, copies(step - 1))

    @pl.when(step + 1 == n_steps)
    def _():
        _wait_rows(rows, copies(step))


def scatter_rows(x, idx, out_dtype):
    n, d = x.shape
    rows = min(n, ROWS_PER_STEP)
    return pl.pallas_call(
        _scatter_kernel,
        grid_spec=pltpu.PrefetchScalarGridSpec(
            num_scalar_prefetch=1,
            grid=(n // rows,),
            in_specs=[pl.BlockSpec((rows, d), lambda i, idx_ref: (i, 0))],
            out_specs=pl.BlockSpec(memory_space=pl.ANY),
            scratch_shapes=[pltpu.VMEM((2, rows, d), out_dtype), pltpu.SemaphoreType.DMA((2,))],
        ),
        out_shape=jax.ShapeDtypeStruct((n, d), out_dtype),
        compiler_params=_params("arbitrary"),
        name="scatter_rows",
    )(idx, x)


def _ffn_kernel(xg_ref, w1_ref, w3_ref, w2_ref, g_ref, y_ref, hid_s, *, f1, tf):
    s = pl.program_id(1)

    @pl.when(s < f1)
    def _():
        xs = xg_ref[...]
        h1 = jnp.dot(xs, w1_ref[...].astype(BF16), preferred_element_type=F32)
        h3 = jnp.dot(xs, w3_ref[...].astype(BF16), preferred_element_type=F32)
        hid_s[s] = (jax.nn.silu(h1) * h3).astype(BF16)

    @pl.when(s >= f1)
    def _():
        acc = jnp.dot(hid_s[0], w2_ref[0:tf, :].astype(BF16), preferred_element_type=F32)
        for f in range(1, f1):
            acc += jnp.dot(hid_s[f], w2_ref[f * tf:(f + 1) * tf, :].astype(BF16), preferred_element_type=F32)
        y_ref[...] = (acc * g_ref[...]).astype(y_ref.dtype)


def expert_ffn(xg, w1, w3, w2, gate, layer):
    e_n, m, d = xg.shape
    ff = w1.shape[3]
    tf, tn = min(ff, 256), min(d, 512)
    f1, f2 = ff // tf, d // tn
    out_step = lambda s: jnp.maximum(s - f1, 0)
    next_e = lambda e: jnp.minimum(e + 1, e_n - 1)

    def rows_map(e, s):
        return (jnp.where(s < f1, e, next_e(e)), 0, 0)

    def up_map(switch_step):
        def index(e, s):
            switched = s >= f1 + switch_step
            return (layer, jnp.where(switched, next_e(e), e), 0, jnp.where(switched, 0, jnp.minimum(s, f1 - 1)))
        return index

    def down_map(e, s):
        hold = jnp.logical_and(s == 0, e > 0)
        return (layer, jnp.where(hold, e - 1, e), 0, jnp.where(hold, f2 - 1, out_step(s)))

    return pl.pallas_call(
        functools.partial(_ffn_kernel, f1=f1, tf=tf),
        grid=(e_n, f1 + f2),
        in_specs=[
            pl.BlockSpec((None, m, d), rows_map),
            pl.BlockSpec((None, None, d, tf), up_map(f2 // 3)),
            pl.BlockSpec((None, None, d, tf), up_map(2 * f2 // 3)),
            pl.BlockSpec((None, None, ff, tn), down_map),
            pl.BlockSpec((None, m, 1), lambda e, s: (e, 0, 0)),
        ],
        out_specs=pl.BlockSpec((None, m, tn), lambda e, s: (e, 0, out_step(s))),
        out_shape=jax.ShapeDtypeStruct((e_n, m, d), BF16),
        scratch_shapes=[pltpu.VMEM((f1, m, tf), BF16)],
        compiler_params=_params("arbitrary", "arbitrary"),
        name="expert_ffn",
    )(xg, w1, w3, w2, gate)


ITEM_ADD, ITEM_FIRST, ITEM_LAST = 1, 2, 4


def _combine_kernel(tile_ref, chunk_ref, flag_ref, yg_ref, st_ref, ct_ref, x_ref, gt_ref, ng_ref, o_ref, acc_s, *,
                    ch, normalize):
    del tile_ref
    item = pl.program_id(0) * pl.num_programs(1) + pl.program_id(1)
    flags = flag_ref[item]

    @pl.when((flags & ITEM_FIRST) != 0)
    def _():
        acc_s[...] = jnp.zeros_like(acc_s)

    @pl.when((flags & ITEM_ADD) != 0)
    def _():
        pair = chunk_ref[item] * ch + lax.broadcasted_iota(I32, (ch, 1), 0)
        rel = pair - st_ref[...]
        own_t = jnp.where(jnp.logical_and(rel >= 0, rel < ct_ref[...]), 1.0, 0.0).astype(BF16)
        acc_s[...] += lax.dot_general(own_t, yg_ref[...].astype(BF16), (((0,), (0,)), ((), ())),
                                      preferred_element_type=F32)

    @pl.when((flags & ITEM_LAST) != 0)
    def _():
        out = x_ref[...] + gt_ref[0] * acc_s[...]
        o_ref[...] = _rms(out, ng_ref[...]) if normalize else out


def _combine_work_items(start, *, batch, n, tt, pairs, ch):
    tiles, n_chunks = n // tt, pairs // ch
    n_items = tiles + n_chunks
    first = start.reshape(batch, n)[:, ::tt]
    last = jnp.concatenate([first[:, 1:], jnp.full((batch, 1), pairs, I32)], axis=1)
    c_first = jnp.minimum(first // ch, n_chunks - 1)
    c_num = jnp.where(last > first, (last - 1) // ch - first // ch + 1, 0)
    t_items = jnp.maximum(c_num, 1)
    ends = jnp.cumsum(t_items, axis=1)
    k = jnp.arange(n_items, dtype=I32)[None, :]
    tile = jnp.minimum(jnp.sum((ends[:, None, :] <= k[:, :, None]).astype(I32), axis=-1), tiles - 1)
    at = lambda v: jnp.take_along_axis(v, tile, axis=1)
    j = k - at(ends - t_items)
    real = k < ends[:, -1:]
    chunk = at(c_first) + jnp.minimum(j, jnp.maximum(at(c_num) - 1, 0))
    flags = (ITEM_ADD * (real & (j < at(c_num))) + ITEM_FIRST * (real & (j == 0))
             + ITEM_LAST * (real & (j == at(t_items) - 1)))
    flat = lambda v: v.reshape(-1).astype(I32)
    return n_items, flat(tile), flat(chunk), flat(flags)


def combine_residual(yg, start, cnt, x, mod3, rows, k_gate, norm_g, *, batch, n, row0, pairs, pair_row0, normalize):
    d = x.shape[1]
    tt = min(n, 256)
    ch = min(pairs, 512)
    tiles, n_chunks = n // tt, pairs // ch
    n_items, item_tile, item_chunk, item_flags = _combine_work_items(start, batch=batch, n=n, tt=tt, pairs=pairs,
                                                                     ch=ch)
    mod_row = rows.mod_row(tt)
    rt0, chunk0 = row0 // tt, pair_row0 // ch

    def tile_of(b, k, tile_ref):
        return tile_ref[b * n_items + k]

    yg_map = lambda b, k, tile_ref, chunk_ref, flag_ref: (chunk0 + b * n_chunks + chunk_ref[b * n_items + k], 0)
    row_map = lambda b, k, tile_ref, chunk_ref, flag_ref: (rt0 + b * tiles + tile_of(b, k, tile_ref), 0)
    tok_map = lambda b, k, tile_ref, chunk_ref, flag_ref: (b, 0, tile_of(b, k, tile_ref))
    gate_map = lambda b, k, tile_ref, chunk_ref, flag_ref: (
        mod_row(rt0 + b * tiles + tile_of(b, k, tile_ref)), 0, k_gate)
    return pl.pallas_call(
        functools.partial(_combine_kernel, ch=ch, normalize=normalize),
        grid_spec=pltpu.PrefetchScalarGridSpec(
            num_scalar_prefetch=3,
            grid=(batch, n_items),
            in_specs=[
                pl.BlockSpec((ch, d), yg_map),
                pl.BlockSpec((None, 1, tt), tok_map),
                pl.BlockSpec((None, 1, tt), tok_map),
                pl.BlockSpec((tt, d), row_map),
                pl.BlockSpec((1, 1, d), gate_map),
                pl.BlockSpec((1, d), lambda b, k, tile_ref, chunk_ref, flag_ref: (0, 0)),
            ],
            out_specs=pl.BlockSpec((tt, d), row_map),
            scratch_shapes=[pltpu.VMEM((tt, d), F32)],
        ),
        out_shape=jax.ShapeDtypeStruct(x.shape, F32),
        input_output_aliases={6: 0},
        compiler_params=_params("arbitrary", "arbitrary"),
        name="combine_residual",
    )(item_tile, item_chunk, item_flags, yg, start, cnt, x, mod3, norm_g.reshape(1, d))


def expert_choice_ffn_residual(x, h, logits_t, mod3, rows, k_gate, w1, w3, w2, layer, norm_g, *, batch, sets,
                               normalize):
    e_n = logits_t.shape[0]
    d = x.shape[1]
    boff = jnp.arange(batch, dtype=I32)[:, None, None]
    routed, src_rows, pair_rows, gate_cols, pair_row0 = [], [], [], [], 0
    for n, row0 in sets:
        idx, dest, gate, start, cnt = expert_choice_route(logits_t, batch=batch, n=n, col0=row0)
        cap = idx.shape[2]
        pairs = e_n * cap
        src_rows.append(jnp.transpose(idx[..., 0] + row0 + boff * n, (1, 0, 2)).reshape(e_n, batch * cap))
        pair_rows.append(jnp.transpose(dest[..., 0] + pair_row0 + boff * pairs, (1, 0, 2)).reshape(e_n, batch * cap))
        gate_cols.append(jnp.transpose(gate, (1, 0, 2, 3)).reshape(e_n, batch * cap, 1))
        routed.append((start, cnt, n, row0, pairs, pair_row0))
        pair_row0 += batch * pairs
    src_rows = jnp.concatenate(src_rows, axis=1)
    m = src_rows.shape[1]
    xg = gather_rows(h, src_rows.reshape(-1), BF16).reshape(e_n, m, d)
    y = expert_ffn(xg, w1, w3, w2, jnp.concatenate(gate_cols, axis=1), layer).reshape(e_n * m, d)
    yg = scatter_rows(y, jnp.concatenate(pair_rows, axis=1).reshape(-1), F32)
    for start, cnt, n, row0, pairs, pair_row0 in routed:
        x = combine_residual(yg, start, cnt, x, mod3, rows, k_gate, norm_g, batch=batch, n=n, row0=row0,
                             pairs=pairs, pair_row0=pair_row0, normalize=normalize)
    return x


def kernel(x, c, ctx, c_ctx, ada_w, ada_b, norm1_g, norm2_g, w_in, na_rpb, gmlp_norm_g, gmlp_ws, gmlp_bs,
           w_branch_a, w_branch_b, w_out, router_w, exp_w1, exp_w3, exp_w2, final_norm_g):
    batch, n_lat, d = x.shape
    n_ctx = ctx.shape[1]
    depth = ada_w.shape[0]
    rows = Rows(batch, n_lat, n_ctx)
    lat_rows, all_rows = rows.lat_rows, rows.all_rows
    off_k, off_v, off_u = NA_WIDTH, 2 * NA_WIDTH, 3 * NA_WIDTH
    off_gv = off_u + GMLP_WIDTH
    off_ga = off_gv + GMLP_WIDTH
    off_gb = off_ga + d
    in_cols = off_gb + d

    x_head, x_tail = x.reshape(lat_rows, d), ctx.reshape(batch * n_ctx, d)
    cv = jnp.zeros((MOD_ROWS, d), F32).at[:batch].set(c).at[batch].set(c_ctx)

    for layer in range(depth):
        last = layer == depth - 1
        nrows = lat_rows if last else all_rows
        mod3 = ada_mod(cv, ada_w, ada_b, layer).reshape(MOD_ROWS, 1, N_MOD * d)

        h = norm_mod(x_head, x_tail, norm1_g, layer, mod3, rows, 0, 1, 0, nrows, BF16)
        qkv = rest = matmul(h, w_in, layer, row0=0, nrows=nrows, col0=0, ncols=in_cols, out_dtype=BF16,
                            scaled_cols=NA_WIDTH, scale=NA_HEAD_DIM ** -0.5)
        bias_table = build_col_table(na_rpb[layer])
        if last:
            h_ctx = norm_mod(x_head, x_tail, norm1_g, layer, mod3, rows, 0, 1, lat_rows, batch * n_ctx, BF16)
            kvc = matmul(h_ctx, w_in, layer, row0=0, nrows=batch * n_ctx, col0=off_k, ncols=2 * NA_WIDTH,
                         out_dtype=BF16)
            o_a = neighbourhood_attention(qkv, kvc, bias_table, batch=batch, n_lat=n_lat, n_ctx=n_ctx,
                                          kc_row0=0, kc_col0=0, vc_col0=NA_WIDTH)
        else:
            o_lat = neighbourhood_attention(qkv, qkv, bias_table, batch=batch, n_lat=n_lat, n_ctx=n_ctx,
                                            kc_row0=lat_rows, kc_col0=off_k, vc_col0=off_v)
            o_ctx = context_attention(qkv, batch=batch, n_ctx=n_ctx, row0=lat_rows)
            o_a = jnp.concatenate([o_lat, o_ctx], axis=0)
        o_b = spatial_gating(rest, gmlp_norm_g, gmlp_ws, gmlp_bs, layer, nrows=nrows, u_col0=off_u, gv_col0=off_gv)
        y = gated_merge(o_a, o_b, w_branch_a, w_branch_b, rest, layer, nrows=nrows, ga_col0=off_ga, gb_col0=off_gb)
        stream = outproj_residual(y, w_out, x_head, x_tail, mod3, layer, rows, 2, nrows=nrows)

        h2, logits_t = norm_mod_router(stream, norm2_g, layer, mod3, rows, 3, 4, router_w, nrows)
        sets = [(n_lat, 0)] if last else [(n_lat, 0), (n_ctx, lat_rows)]
        stream = expert_choice_ffn_residual(stream, h2, logits_t, mod3, rows, 5, exp_w1, exp_w3, exp_w2, layer,
                                            final_norm_g, batch=batch, sets=sets, normalize=last)
        x_head = x_tail = stream

    return stream.reshape(batch, n_lat, d)
```

```python
import functools

import jax
import jax.numpy as jnp
from jax import lax
from jax.experimental import pallas as pl
from jax.experimental.pallas import tpu as pltpu

F32, BF16, I32 = jnp.float32, jnp.bfloat16, jnp.int32

GRID_W = 64
NA_HEADS = 16
NA_HEAD_DIM = 128
NA_WIDTH = NA_HEADS * NA_HEAD_DIM
WIN_H = 8
WIN_W = 16
GMLP_GROUPS = 16
GMLP_WIDTH = 2048
CHUNK = 128
N_EXPERTS = 16
CAPACITY_FACTOR = 2
N_MOD = 6
EPS = 1e-6
MASK_BIAS = -1e30

V7X_VMEM_LIMIT_BYTES = 56 * 1024 * 1024
LANES = 128
MOD_ROWS = 8
NORM_ROW_TILE = 512


def _params(*sem):
    return pltpu.CompilerParams(dimension_semantics=sem, vmem_limit_bytes=V7X_VMEM_LIMIT_BYTES)


def _cast_weight(w_ref, wb_ref):
    rows = w_ref.shape[0]
    step = min(rows, 512)

    def body(i, carry):
        r = pl.multiple_of(i * step, step)
        wb_ref[pl.ds(r, step), :] = w_ref[pl.ds(r, step), :].astype(BF16)
        return carry

    lax.fori_loop(0, rows // step, body, 0)


def _ada_kernel(c_ref, w_ref, b_ref, o_ref):
    c = c_ref[...]
    s = c * jax.nn.sigmoid(c)
    o_ref[...] = jnp.dot(s.astype(BF16), w_ref[...].astype(BF16), preferred_element_type=F32) + b_ref[...]


def ada_mod(cv, ada_w, ada_b, layer):
    _, d, m = ada_w.shape
    tn = min(m, 1024)
    return pl.pallas_call(
        _ada_kernel,
        grid=(m // tn,),
        in_specs=[
            pl.BlockSpec((MOD_ROWS, d), lambda j: (0, 0)),
            pl.BlockSpec((None, d, tn), lambda j: (layer, 0, j)),
            pl.BlockSpec((None, 1, tn), lambda j: (layer, 0, j)),
        ],
        out_specs=pl.BlockSpec((MOD_ROWS, tn), lambda j: (0, j)),
        out_shape=jax.ShapeDtypeStruct((MOD_ROWS, m), F32),
        compiler_params=_params("arbitrary"),
        name="ada_mod",
    )(cv, ada_w, ada_b.reshape(ada_b.shape[0], 1, m))


class Rows:
    def __init__(self, batch, n_lat, n_ctx):
        self.batch, self.n_lat, self.n_ctx = batch, n_lat, n_ctx
        self.lat_rows = batch * n_lat
        self.all_rows = self.lat_rows + batch * n_ctx

    def mod_row(self, tile_rows):
        lat_tiles = self.lat_rows // tile_rows
        per_batch = self.n_lat // tile_rows
        batch = self.batch

        def f(i):
            return jnp.where(i < lat_tiles, i // per_batch, batch)

        return f


def _mod_spec(mod_row, k, d):
    return pl.BlockSpec((1, 1, d), lambda i: (mod_row(i), 0, k))


def _rms(x, g):
    return x * lax.rsqrt(jnp.mean(x * x, axis=-1, keepdims=True) + EPS) * g


def _two_source_specs(block, head_tiles, tile_of):
    def head(*ids):
        t, c = tile_of(*ids)
        return (jnp.minimum(t, head_tiles - 1), c)

    def tail(*ids):
        t, c = tile_of(*ids)
        return (jnp.maximum(t - head_tiles, 0), c)

    return pl.BlockSpec(block, head), pl.BlockSpec(block, tail)


def _norm_mod_kernel(xh_ref, xt_ref, g_ref, sh_ref, sc_ref, o_ref, *, head_tiles, t0):
    def emit(x_ref):
        y = _rms(x_ref[...], g_ref[...])
        o_ref[...] = (y * (1.0 + sc_ref[0]) + sh_ref[0]).astype(o_ref.dtype)

    in_head = pl.program_id(0) + t0 < head_tiles
    pl.when(in_head)(lambda: emit(xh_ref))
    pl.when(jnp.logical_not(in_head))(lambda: emit(xt_ref))


def norm_mod(x_head, x_tail, g, layer, mod3, rows, k_shift, k_scale, row0, nrows, out_dtype):
    d = x_head.shape[1]
    tr = NORM_ROW_TILE
    mod_row = rows.mod_row(tr)
    t0 = row0 // tr
    head_tiles = x_head.shape[0] // tr
    head_spec, tail_spec = _two_source_specs((tr, d), head_tiles, lambda i: (i + t0, 0))
    return pl.pallas_call(
        functools.partial(_norm_mod_kernel, head_tiles=head_tiles, t0=t0),
        grid=(nrows // tr,),
        in_specs=[
            head_spec,
            tail_spec,
            pl.BlockSpec((None, 1, d), lambda i: (layer, 0, 0)),
            pl.BlockSpec((1, 1, d), lambda i: (mod_row(i + t0), 0, k_shift)),
            pl.BlockSpec((1, 1, d), lambda i: (mod_row(i + t0), 0, k_scale)),
        ],
        out_specs=pl.BlockSpec((tr, d), lambda i: (i, 0)),
        out_shape=jax.ShapeDtypeStruct((nrows, d), out_dtype),
        compiler_params=_params("arbitrary"),
        name="norm_mod",
    )(x_head, x_tail, g.reshape(g.shape[0], 1, d), mod3, mod3)


WEIGHT_TILE_BUFFERS = pl.Buffered(1)
MATMUL_ROW_TILES = 8

def _mm_kernel(a_ref, w_ref, o_ref, wb_ref, *, scaled_tiles, scale):
    @pl.when(pl.program_id(1) == 0)
    def _():
        _cast_weight(w_ref, wb_ref)

    acc = jnp.dot(a_ref[...], wb_ref[...], preferred_element_type=F32)
    if scaled_tiles:
        acc = acc * jnp.where(pl.program_id(0) < scaled_tiles, scale, 1.0)
    o_ref[...] = acc.astype(o_ref.dtype)


def matmul(a, w, layer, *, row0, nrows, col0, ncols, out_dtype, scaled_cols=0, scale=1.0, tm=512, tn=1024):
    k = a.shape[1]
    if nrows % (MATMUL_ROW_TILES * 16) == 0 and nrows // MATMUL_ROW_TILES > tm:
        tm = nrows // MATMUL_ROW_TILES
    tm, tn = min(tm, nrows), min(tn, ncols)
    r0, c0 = row0 // tm, col0 // tn
    assert scaled_cols % tn == 0 and row0 % tm == 0
    return pl.pallas_call(
        functools.partial(_mm_kernel, scaled_tiles=scaled_cols // tn, scale=scale),
        grid=(ncols // tn, nrows // tm),
        in_specs=[
            pl.BlockSpec((tm, k), lambda j, i: (i + r0, 0)),
            pl.BlockSpec((None, k, tn), lambda j, i: (layer, 0, j + c0), pipeline_mode=WEIGHT_TILE_BUFFERS),
        ],
        out_specs=pl.BlockSpec((tm, tn), lambda j, i: (i, j)),
        out_shape=jax.ShapeDtypeStruct((nrows, ncols), out_dtype),
        scratch_shapes=[pltpu.VMEM((k, tn), BF16)],
        compiler_params=_params("arbitrary", "arbitrary"),
        name="matmul",
    )(a, w)


def _merge_kernel(oah_ref, oat_ref, ob_ref, wa_ref, wb_ref, ga_ref, gb_ref, y_ref, wab_ref, wbb_ref, *, head_tiles):
    @pl.when(pl.program_id(1) == 0)
    def _():
        _cast_weight(wa_ref, wab_ref)
        _cast_weight(wb_ref, wbb_ref)

    gated_b = jax.nn.sigmoid(gb_ref[...].astype(F32)) * jnp.dot(ob_ref[...], wbb_ref[...],
                                                                 preferred_element_type=F32)

    def emit(oa_ref):
        ya = jnp.dot(oa_ref[...], wab_ref[...], preferred_element_type=F32)
        y_ref[...] = (jax.nn.sigmoid(ga_ref[...].astype(F32)) * ya + gated_b).astype(y_ref.dtype)

    in_head = pl.program_id(1) < head_tiles
    pl.when(in_head)(lambda: emit(oah_ref))
    pl.when(jnp.logical_not(in_head))(lambda: emit(oat_ref))


def gated_merge(oa_head, oa_tail, o_b, w_a, w_b, gates, layer, *, nrows, ga_col0, gb_col0, tm=512, tn=1024):
    ka, kb = oa_head.shape[1], o_b.shape[1]
    d = w_a.shape[2]
    tm, tn = min(tm, nrows), min(tn, d)
    ca, cb = ga_col0 // tn, gb_col0 // tn
    head_tiles = oa_head.shape[0] // tm
    head_spec, tail_spec = _two_source_specs((tm, ka), head_tiles, lambda j, i: (i, 0))
    return pl.pallas_call(
        functools.partial(_merge_kernel, head_tiles=head_tiles),
        grid=(d // tn, nrows // tm),
        in_specs=[
            head_spec,
            tail_spec,
            pl.BlockSpec((tm, kb), lambda j, i: (i, 0)),
            pl.BlockSpec((None, ka, tn), lambda j, i: (layer, 0, j), pipeline_mode=WEIGHT_TILE_BUFFERS),
            pl.BlockSpec((None, kb, tn), lambda j, i: (layer, 0, j), pipeline_mode=WEIGHT_TILE_BUFFERS),
            pl.BlockSpec((tm, tn), lambda j, i: (i, j + ca)),
            pl.BlockSpec((tm, tn), lambda j, i: (i, j + cb)),
        ],
        out_specs=pl.BlockSpec((tm, tn), lambda j, i: (i, j)),
        out_shape=jax.ShapeDtypeStruct((nrows, d), BF16),
        scratch_shapes=[pltpu.VMEM((ka, tn), BF16), pltpu.VMEM((kb, tn), BF16)],
        compiler_params=_params("arbitrary", "arbitrary"),
        name="gated_merge",
    )(oa_head, oa_tail, o_b, w_a, w_b, gates, gates)


def _outproj_kernel(y_ref, w_ref, xh_ref, xt_ref, gt_ref, o_ref, wb_ref, *, head_tiles):
    @pl.when(pl.program_id(1) == 0)
    def _():
        _cast_weight(w_ref, wb_ref)

    upd = gt_ref[0] * jnp.dot(y_ref[...], wb_ref[...], preferred_element_type=F32)

    def emit(x_ref):
        o_ref[...] = x_ref[...] + upd

    in_head = pl.program_id(1) < head_tiles
    pl.when(in_head)(lambda: emit(xh_ref))
    pl.when(jnp.logical_not(in_head))(lambda: emit(xt_ref))


def outproj_residual(y, w_out, x_head, x_tail, mod3, layer, rows, k_gate, *, nrows, tm=512, tn=1024):
    k = y.shape[1]
    d = w_out.shape[2]
    tm, tn = min(tm, nrows), min(tn, d)
    mod_row = rows.mod_row(tm)
    gate_col0 = k_gate * (d // tn)
    head_tiles = x_head.shape[0] // tm
    head_spec, tail_spec = _two_source_specs((tm, tn), head_tiles, lambda j, i: (i, j))
    return pl.pallas_call(
        functools.partial(_outproj_kernel, head_tiles=head_tiles),
        grid=(d // tn, nrows // tm),
        in_specs=[
            pl.BlockSpec((tm, k), lambda j, i: (i, 0)),
            pl.BlockSpec((None, k, tn), lambda j, i: (layer, 0, j), pipeline_mode=WEIGHT_TILE_BUFFERS),
            head_spec,
            tail_spec,
            pl.BlockSpec((1, 1, tn), lambda j, i: (mod_row(i), 0, j + gate_col0)),
        ],
        out_specs=pl.BlockSpec((tm, tn), lambda j, i: (i, j)),
        out_shape=jax.ShapeDtypeStruct((nrows, d), F32),
        scratch_shapes=[pltpu.VMEM((k, tn), BF16)],
        compiler_params=_params("arbitrary", "arbitrary"),
        name="outproj_residual",
    )(y, w_out, x_head, x_tail, mod3)


Q_ROWS = 4
BAND_ROWS = WIN_H + Q_ROWS
N_PATTERNS = 3


def build_col_table(rpb):
    cols = jnp.arange(GRID_W)
    col_start = jnp.clip(cols - WIN_W // 2, 0, GRID_W - WIN_W)
    col_ok = (cols[None, :] >= col_start[:, None]) & (cols[None, :] < col_start[:, None] + WIN_W)
    col_off = cols[None, :] - cols[:, None] + (WIN_W - 1)
    pick = (col_off[None] == jnp.arange(2 * WIN_W - 1)[:, None, None]).astype(F32)
    t = jnp.einsum("hrj,jqk->hrqk", rpb.astype(F32), pick, precision=lax.Precision.HIGHEST)
    return jnp.where(col_ok[None, None], t, MASK_BIAS)


def _band_patterns(grid_rows):
    half = WIN_H // 2
    groups = grid_rows // Q_ROWS
    assert grid_rows % Q_ROWS == 0 and grid_rows >= BAND_ROWS
    patterns = []
    for g in (0, min(1, groups - 1), groups - 1):
        band_start = min(max(g * Q_ROWS - half, 0), grid_rows - BAND_ROWS)
        pattern = []
        for a in range(Q_ROWS):
            q_row = g * Q_ROWS + a
            win_start = min(max(q_row - half, 0), grid_rows - WIN_H)
            pattern.append([band_start + i - q_row + WIN_H - 1 if win_start <= band_start + i < win_start + WIN_H
                            else None for i in range(BAND_ROWS)])
        patterns.append(pattern)
    return patterns


def _softmax_pv(parts):
    m = functools.reduce(jnp.maximum, [jnp.max(s, axis=-1, keepdims=True) for s, _ in parts])
    es = [jnp.exp(s - m) for s, _ in parts]
    denom = functools.reduce(jnp.add, [jnp.sum(e, axis=-1, keepdims=True) for e in es])
    acc = functools.reduce(jnp.add, [jnp.dot(e.astype(BF16), v, preferred_element_type=F32)
                                     for e, (_, v) in zip(es, parts)])
    return acc / denom


def _qk(q, k):
    return lax.dot_general(q, k, (((1,), (1,)), ((), ())), preferred_element_type=F32)


def _natten_kernel(q_ref, k_ref, v_ref, kc_ref, vc_ref, ct_ref, o_ref, bt_ref, *, grid_rows):
    groups = grid_rows // Q_ROWS
    nq, band = Q_ROWS * GRID_W, BAND_ROWS * GRID_W
    kc = kc_ref[...]
    vc = vc_ref[...]

    @pl.when(jnp.logical_and(pl.program_id(0) == 0, pl.program_id(1) == 0))
    def _():
        bt_ref[...] = jnp.full(bt_ref.shape, MASK_BIAS, F32)

    @pl.when(pl.program_id(1) == 0)
    def _():
        for p, pattern in enumerate(_band_patterns(grid_rows)):
            for a, row in enumerate(pattern):
                for i, off in enumerate(row):
                    if off is not None:
                        bt_ref[p, a * GRID_W:(a + 1) * GRID_W, i * GRID_W:(i + 1) * GRID_W] = ct_ref[off]

    def body(g, carry):
        band_start = jnp.clip(g * Q_ROWS - WIN_H // 2, 0, grid_rows - BAND_ROWS)
        pattern = jnp.where(g == 0, 0, jnp.where(g == groups - 1, 2, 1))
        q0 = pl.multiple_of(g * nq, nq)
        k0 = pl.multiple_of(band_start * GRID_W, GRID_W)
        q = q_ref[pl.ds(q0, nq), :]
        kb = k_ref[pl.ds(k0, band), :]
        vb = v_ref[pl.ds(k0, band), :]
        s_loc = _qk(q, kb) + bt_ref[pattern]
        s_ctx = _qk(q, kc)
        o_ref[pl.ds(q0, nq), :] = _softmax_pv([(s_loc, vb), (s_ctx, vc)]).astype(o_ref.dtype)
        return carry

    lax.fori_loop(0, groups, body, 0, unroll=4)


def neighbourhood_attention(qkv, kvc, col_table, *, batch, n_lat, n_ctx, kc_row0, kc_col0, vc_col0):
    dh, h = NA_HEAD_DIM, NA_HEADS
    grid_rows = n_lat // GRID_W
    rb0, ck, cv = kc_row0 // n_ctx, kc_col0 // dh, vc_col0 // dh
    return pl.pallas_call(
        functools.partial(_natten_kernel, grid_rows=grid_rows),
        grid=(h, batch),
        in_specs=[
            pl.BlockSpec((n_lat, dh), lambda hh, b: (b, hh)),
            pl.BlockSpec((n_lat, dh), lambda hh, b: (b, h + hh)),
            pl.BlockSpec((n_lat, dh), lambda hh, b: (b, 2 * h + hh)),
            pl.BlockSpec((n_ctx, dh), lambda hh, b: (rb0 + b, ck + hh)),
            pl.BlockSpec((n_ctx, dh), lambda hh, b: (rb0 + b, cv + hh)),
            pl.BlockSpec((None, 2 * WIN_H - 1, GRID_W, GRID_W), lambda hh, b: (hh, 0, 0, 0)),
        ],
        out_specs=pl.BlockSpec((n_lat, dh), lambda hh, b: (b, hh)),
        out_shape=jax.ShapeDtypeStruct((batch * n_lat, NA_WIDTH), BF16),
        scratch_shapes=[pltpu.VMEM((N_PATTERNS, Q_ROWS * GRID_W, BAND_ROWS * GRID_W), F32)],
        compiler_params=_params("arbitrary", "arbitrary"),
        name="neighbourhood_attention",
    )(qkv, qkv, qkv, kvc, kvc, col_table)


def _ctx_attn_kernel(q_ref, k_ref, v_ref, o_ref):
    o_ref[...] = _softmax_pv([(_qk(q_ref[...], k_ref[...]), v_ref[...])]).astype(o_ref.dtype)


def context_attention(qkv, *, batch, n_ctx, row0):
    dh, h = NA_HEAD_DIM, NA_HEADS
    rb0 = row0 // n_ctx
    return pl.pallas_call(
        _ctx_attn_kernel,
        grid=(batch, h),
        in_specs=[
            pl.BlockSpec((n_ctx, dh), lambda b, hh: (rb0 + b, hh)),
            pl.BlockSpec((n_ctx, dh), lambda b, hh: (rb0 + b, h + hh)),
            pl.BlockSpec((n_ctx, dh), lambda b, hh: (rb0 + b, 2 * h + hh)),
        ],
        out_specs=pl.BlockSpec((n_ctx, dh), lambda b, hh: (b, hh)),
        out_shape=jax.ShapeDtypeStruct((batch * n_ctx, NA_WIDTH), BF16),
        compiler_params=_params("arbitrary", "arbitrary"),
        name="context_attention",
    )(qkv, qkv, qkv)


def _gmlp_kernel(u_ref, gv_ref, ng_ref, ws_ref, bst_ref, o_ref):
    gd = GMLP_WIDTH // GMLP_GROUPS
    for c in range(u_ref.shape[0] // CHUNK):
        rows = slice(c * CHUNK, (c + 1) * CHUNK)
        v = jax.nn.gelu(gv_ref[rows, :].astype(F32))
        vc = v - jnp.mean(v, axis=-1, keepdims=True)
        y = vc * lax.rsqrt(jnp.mean(vc * vc, axis=-1, keepdims=True) + EPS) * ng_ref[...]
        yb = y.astype(BF16)
        for g in range(GMLP_GROUPS):
            sl = slice(g * gd, (g + 1) * gd)
            mixed = jnp.dot(ws_ref[g].astype(BF16), yb[:, sl], preferred_element_type=F32) + bst_ref[:, g:g + 1]
            o_ref[rows, sl] = (jax.nn.gelu(u_ref[rows, sl].astype(F32)) * mixed).astype(o_ref.dtype)


def spatial_gating(p, norm_g, ws, bs, layer, *, nrows, u_col0, gv_col0):
    w = GMLP_WIDTH
    bst = jnp.transpose(bs[layer])
    tr = 4 * CHUNK
    return pl.pallas_call(
        _gmlp_kernel,
        grid=(nrows // tr,),
        in_specs=[
            pl.BlockSpec((tr, w), lambda i: (i, u_col0 // w)),
            pl.BlockSpec((tr, w), lambda i: (i, gv_col0 // w)),
            pl.BlockSpec((None, 1, w), lambda i: (layer, 0, 0)),
            pl.BlockSpec((None, GMLP_GROUPS, CHUNK, CHUNK), lambda i: (layer, 0, 0, 0)),
            pl.BlockSpec((CHUNK, GMLP_GROUPS), lambda i: (0, 0)),
        ],
        out_specs=pl.BlockSpec((tr, w), lambda i: (i, 0)),
        out_shape=jax.ShapeDtypeStruct((nrows, w), BF16),
        compiler_params=_params("arbitrary"),
        name="spatial_gating",
    )(p, p, norm_g.reshape(norm_g.shape[0], 1, w), ws, bst)


def _split_bf16(x):
    hi = x.astype(BF16)
    return hi, (x - hi.astype(F32)).astype(BF16)


def _norm_router_kernel(x_ref, g_ref, sh_ref, sc_ref, rw_ref, h_ref, lg_ref):
    h = _rms(x_ref[...], g_ref[...]) * (1.0 + sc_ref[0]) + sh_ref[0]
    h_ref[...] = h
    hh, hl = _split_bf16(h)
    wh, wl = _split_bf16(rw_ref[...])
    lg = (jnp.dot(hh, wh, preferred_element_type=F32) + jnp.dot(hl, wh, preferred_element_type=F32)
          + jnp.dot(hh, wl, preferred_element_type=F32))
    lg_ref[...] = jnp.transpose(lg)[:N_EXPERTS, :]


def norm_mod_router(x, g, layer, mod3, rows, k_shift, k_scale, router_w, nrows):
    d = x.shape[1]
    tr = NORM_ROW_TILE
    mod_row = rows.mod_row(tr)
    rw = jnp.pad(router_w[layer], ((0, 0), (0, LANES - N_EXPERTS)))
    return pl.pallas_call(
        _norm_router_kernel,
        grid=(nrows // tr,),
        in_specs=[
            pl.BlockSpec((tr, d), lambda i: (i, 0)),
            pl.BlockSpec((None, 1, d), lambda i: (layer, 0, 0)),
            pl.BlockSpec((1, 1, d), lambda i: (mod_row(i), 0, k_shift)),
            pl.BlockSpec((1, 1, d), lambda i: (mod_row(i), 0, k_scale)),
            pl.BlockSpec((d, LANES), lambda i: (0, 0)),
        ],
        out_specs=[pl.BlockSpec((tr, d), lambda i: (i, 0)), pl.BlockSpec((N_EXPERTS, tr), lambda i: (0, i))],
        out_shape=[jax.ShapeDtypeStruct((nrows, d), F32), jax.ShapeDtypeStruct((N_EXPERTS, nrows), F32)],
        compiler_params=_params("arbitrary"),
        name="norm_mod_router",
    )(x, g.reshape(g.shape[0], 1, d), mod3, mod3, rw)


def _cumsum_lanes(x):
    r, n = x.shape
    blk = min(n, LANES)
    tri = (lax.broadcasted_iota(I32, (blk, blk), 0) <= lax.broadcasted_iota(I32, (blk, blk), 1)).astype(BF16)
    off = jnp.zeros((r, 1), F32)
    out = []
    for j in range(n // blk):
        cs = jnp.dot(x[:, j * blk:(j + 1) * blk].astype(BF16), tri, preferred_element_type=F32) + off
        out.append(cs)
        off = cs[:, blk - 1:blk]
    return jnp.concatenate(out, axis=1) if len(out) > 1 else out[0]


def _route_kernel(lg_ref, idx_ref, dest_ref, gate_ref, start_ref, cnt_ref, slot_s, aff_s, *, cap):
    e_n, n = lg_ref.shape
    lg = lg_ref[...]
    ex = jnp.exp(lg - jnp.max(lg, axis=0, keepdims=True))
    aff = ex / jnp.sum(ex, axis=0, keepdims=True)
    bits = lax.bitcast_convert_type(aff, I32)

    thr = jnp.zeros((e_n, 1), I32)
    for bit in range(30, -1, -1):
        cand = thr | (1 << bit)
        n_ge = jnp.sum((bits >= cand).astype(F32), axis=1, keepdims=True)
        thr = jnp.where(n_ge >= cap, cand, thr)
    above = bits > thr
    tied = (bits == thr).astype(F32)
    need = cap - jnp.sum(above.astype(F32), axis=1, keepdims=True)
    tied_before = _cumsum_lanes(tied) - tied
    sel = jnp.where(above, 1.0, jnp.where(tied_before < need, tied, 0.0))

    slot = _cumsum_lanes(sel) - sel
    per_tok = jnp.sum(sel, axis=0, keepdims=True)
    per_tok8 = jnp.broadcast_to(per_tok, (8, n))
    start = (_cumsum_lanes(per_tok8) - per_tok8)[0:1]
    start_ref[...] = start.astype(I32)
    cnt_ref[...] = per_tok.astype(I32)
    slot_s[...] = jnp.where(sel > 0.0, slot, -1.0)
    aff_s[...] = aff

    tok = lax.broadcasted_iota(I32, (1, n), 1).astype(F32)
    cc = min(cap, LANES)

    def body(e, chosen_by_earlier):
        srow = slot_s[pl.ds(e, 1), :]
        arow = aff_s[pl.ds(e, 1), :]
        drow = start + chosen_by_earlier
        for c0 in range(0, cap, cc):
            want = (lax.broadcasted_iota(I32, (cc, 1), 0) + c0).astype(F32)
            hit = srow == want
            pick = lambda row: jnp.sum(jnp.where(hit, row, 0.0), axis=1, keepdims=True)
            idx_ref[e, pl.ds(c0, cc), :] = pick(tok).astype(I32)
            dest_ref[e, pl.ds(c0, cc), :] = pick(drow).astype(I32)
            gate_ref[e, pl.ds(c0, cc), :] = pick(arow)
        return chosen_by_earlier + (srow >= 0.0).astype(F32)

    lax.fori_loop(0, e_n, body, jnp.zeros((1, n), F32))


def expert_choice_route(logits_t, *, batch, n, col0):
    e_n = logits_t.shape[0]
    cap = CAPACITY_FACTOR * n // e_n
    cb0 = col0 // n
    per_slot = lambda dt: jax.ShapeDtypeStruct((batch, e_n, cap, 1), dt)
    per_tok = jax.ShapeDtypeStruct((batch, 1, n), I32)
    slot_spec = pl.BlockSpec((None, e_n, cap, 1), lambda b: (b, 0, 0, 0))
    tok_spec = pl.BlockSpec((None, 1, n), lambda b: (b, 0, 0))
    return pl.pallas_call(
        functools.partial(_route_kernel, cap=cap),
        grid=(batch,),
        in_specs=[pl.BlockSpec((e_n, n), lambda b: (0, cb0 + b))],
        out_specs=[slot_spec, slot_spec, slot_spec, tok_spec, tok_spec],
        out_shape=[per_slot(I32), per_slot(I32), per_slot(F32), per_tok, per_tok],
        scratch_shapes=[pltpu.VMEM((e_n, n), F32), pltpu.VMEM((e_n, n), F32)],
        compiler_params=_params("arbitrary"),
        name="expert_choice_route",
    )(logits_t)


ROWS_PER_STEP = 256
ROW_COPY_UNROLL = 8


def _start_rows(n, make_copy):
    def start(r, carry):
        make_copy(r).start()
        return carry

    lax.fori_loop(0, n, start, 0, unroll=ROW_COPY_UNROLL)


def _wait_rows(n, make_copy):
    def wait(r, carry):
        make_copy(r).wait()
        return carry

    lax.fori_loop(0, n, wait, 0, unroll=ROW_COPY_UNROLL)


def _gather_kernel(idx_ref, src_ref, o_ref, stage_ref, sem):
    step, n_steps, rows = pl.program_id(0), pl.num_programs(0), o_ref.shape[0]

    def copies(s):
        slot = s % 2
        return lambda r: pltpu.make_async_copy(src_ref.at[pl.ds(idx_ref[s * rows + r], 1), :],
                                               stage_ref.at[slot, pl.ds(r, 1), :], sem.at[slot])

    @pl.when(step == 0)
    def _():
        _start_rows(rows, copies(step))

    @pl.when(step + 1 < n_steps)
    def _():
        _start_rows(rows, copies(step + 1))

    _wait_rows(rows, copies(step))
    o_ref[...] = stage_ref[step % 2].astype(o_ref.dtype)


def gather_rows(src, idx, out_dtype):
    n, d = idx.shape[0], src.shape[1]
    rows = min(n, ROWS_PER_STEP)
    return pl.pallas_call(
        _gather_kernel,
        grid_spec=pltpu.PrefetchScalarGridSpec(
            num_scalar_prefetch=1,
            grid=(n // rows,),
            in_specs=[pl.BlockSpec(memory_space=pl.ANY)],
            out_specs=pl.BlockSpec((rows, d), lambda i, idx_ref: (i, 0)),
            scratch_shapes=[pltpu.VMEM((2, rows, d), src.dtype), pltpu.SemaphoreType.DMA((2,))],
        ),
        out_shape=jax.ShapeDtypeStruct((n, d), out_dtype),
        compiler_params=_params("arbitrary"),
        name="gather_rows",
    )(idx, src)


def _scatter_kernel(idx_ref, x_ref, o_ref, stage_ref, sem):
    step, n_steps, rows = pl.program_id(0), pl.num_programs(0), x_ref.shape[0]

    def copies(s):
        slot = s % 2
        return lambda r: pltpu.make_async_copy(stage_ref.at[slot, pl.ds(r, 1), :],
                                               o_ref.at[pl.ds(idx_ref[s * rows + r], 1), :], sem.at[slot])

    stage_ref[step % 2] = x_ref[...].astype(stage_ref.dtype)
    _start_rows(rows, copies(step))

    @pl.when(step > 0)
    def _():
        _wait_rows(rows, copies(step - 1))

    @pl.when(step + 1 == n_steps)
    def _():
        _wait_rows(rows, copies(step))


def scatter_rows(x, idx, out_dtype):
    n, d = x.shape
    rows = min(n, ROWS_PER_STEP)
    return pl.pallas_call(
        _scatter_kernel,
        grid_spec=pltpu.PrefetchScalarGridSpec(
            num_scalar_prefetch=1,
            grid=(n // rows,),
            in_specs=[pl.BlockSpec((rows, d), lambda i, idx_ref: (i, 0))],
            out_specs=pl.BlockSpec(memory_space=pl.ANY),
            scratch_shapes=[pltpu.VMEM((2, rows, d), out_dtype), pltpu.SemaphoreType.DMA((2,))],
        ),
        out_shape=jax.ShapeDtypeStruct((n, d), out_dtype),
        compiler_params=_params("arbitrary"),
        name="scatter_rows",
    )(idx, x)


def _ffn_kernel(xg_ref, w1_ref, w3_ref, w2_ref, g_ref, y_ref, hid_s, *, f1, tf):
    s = pl.program_id(1)

    @pl.when(s < f1)
    def _():
        xs = xg_ref[...]
        h1 = jnp.dot(xs, w1_ref[...].astype(BF16), preferred_element_type=F32)
        h3 = jnp.dot(xs, w3_ref[...].astype(BF16), preferred_element_type=F32)
        hid_s[s] = (jax.nn.silu(h1) * h3).astype(BF16)

    @pl.when(s >= f1)
    def _():
        acc = jnp.dot(hid_s[0], w2_ref[0:tf, :].astype(BF16), preferred_element_type=F32)
        for f in range(1, f1):
            acc += jnp.dot(hid_s[f], w2_ref[f * tf:(f + 1) * tf, :].astype(BF16), preferred_element_type=F32)
        y_ref[...] = (acc * g_ref[...]).astype(y_ref.dtype)


def expert_ffn(xg, w1, w3, w2, gate, layer):
    e_n, m, d = xg.shape
    ff = w1.shape[3]
    tf, tn = min(ff, 256), min(d, 512)
    f1, f2 = ff // tf, d // tn
    out_step = lambda s: jnp.maximum(s - f1, 0)
    next_e = lambda e: jnp.minimum(e + 1, e_n - 1)

    def rows_map(e, s):
        return (jnp.where(s < f1, e, next_e(e)), 0, 0)

    def up_map(switch_step):
        def index(e, s):
            switched = s >= f1 + switch_step
            return (layer, jnp.where(switched, next_e(e), e), 0, jnp.where(switched, 0, jnp.minimum(s, f1 - 1)))
        return index

    def down_map(e, s):
        hold = jnp.logical_and(s == 0, e > 0)
        return (layer, jnp.where(hold, e - 1, e), 0, jnp.where(hold, f2 - 1, out_step(s)))

    return pl.pallas_call(
        functools.partial(_ffn_kernel, f1=f1, tf=tf),
        grid=(e_n, f1 + f2),
        in_specs=[
            pl.BlockSpec((None, m, d), rows_map),
            pl.BlockSpec((None, None, d, tf), up_map(f2 // 3)),
            pl.BlockSpec((None, None, d, tf), up_map(2 * f2 // 3)),
            pl.BlockSpec((None, None, ff, tn), down_map),
            pl.BlockSpec((None, m, 1), lambda e, s: (e, 0, 0)),
        ],
        out_specs=pl.BlockSpec((None, m, tn), lambda e, s: (e, 0, out_step(s))),
        out_shape=jax.ShapeDtypeStruct((e_n, m, d), BF16),
        scratch_shapes=[pltpu.VMEM((f1, m, tf), BF16)],
        compiler_params=_params("arbitrary", "arbitrary"),
        name="expert_ffn",
    )(xg, w1, w3, w2, gate)


ITEM_ADD, ITEM_FIRST, ITEM_LAST = 1, 2, 4


def _combine_kernel(tile_ref, chunk_ref, flag_ref, yg_ref, st_ref, ct_ref, x_ref, gt_ref, ng_ref, o_ref, acc_s, *,
                    ch, normalize):
    del tile_ref
    item = pl.program_id(0) * pl.num_programs(1) + pl.program_id(1)
    flags = flag_ref[item]

    @pl.when((flags & ITEM_FIRST) != 0)
    def _():
        acc_s[...] = jnp.zeros_like(acc_s)

    @pl.when((flags & ITEM_ADD) != 0)
    def _():
        pair = chunk_ref[item] * ch + lax.broadcasted_iota(I32, (ch, 1), 0)
        rel = pair - st_ref[...]
        own_t = jnp.where(jnp.logical_and(rel >= 0, rel < ct_ref[...]), 1.0, 0.0).astype(BF16)
        acc_s[...] += lax.dot_general(own_t, yg_ref[...].astype(BF16), (((0,), (0,)), ((), ())),
                                      preferred_element_type=F32)

    @pl.when((flags & ITEM_LAST) != 0)
    def _():
        out = x_ref[...] + gt_ref[0] * acc_s[...]
        o_ref[...] = _rms(out, ng_ref[...]) if normalize else out


def _combine_work_items(start, *, batch, n, tt, pairs, ch):
    tiles, n_chunks = n // tt, pairs // ch
    n_items = tiles + n_chunks
    first = start.reshape(batch, n)[:, ::tt]
    last = jnp.concatenate([first[:, 1:], jnp.full((batch, 1), pairs, I32)], axis=1)
    c_first = jnp.minimum(first // ch, n_chunks - 1)
    c_num = jnp.where(last > first, (last - 1) // ch - first // ch + 1, 0)
    t_items = jnp.maximum(c_num, 1)
    ends = jnp.cumsum(t_items, axis=1)
    k = jnp.arange(n_items, dtype=I32)[None, :]
    tile = jnp.minimum(jnp.sum((ends[:, None, :] <= k[:, :, None]).astype(I32), axis=-1), tiles - 1)
    at = lambda v: jnp.take_along_axis(v, tile, axis=1)
    j = k - at(ends - t_items)
    real = k < ends[:, -1:]
    chunk = at(c_first) + jnp.minimum(j, jnp.maximum(at(c_num) - 1, 0))
    flags = (ITEM_ADD * (real & (j < at(c_num))) + ITEM_FIRST * (real & (j == 0))
             + ITEM_LAST * (real & (j == at(t_items) - 1)))
    flat = lambda v: v.reshape(-1).astype(I32)
    return n_items, flat(tile), flat(chunk), flat(flags)


def combine_residual(yg, start, cnt, x, mod3, rows, k_gate, norm_g, *, batch, n, row0, pairs, pair_row0, normalize):
    d = x.shape[1]
    tt = min(n, 256)
    ch = min(pairs, 512)
    tiles, n_chunks = n // tt, pairs // ch
    n_items, item_tile, item_chunk, item_flags = _combine_work_items(start, batch=batch, n=n, tt=tt, pairs=pairs,
                                                                     ch=ch)
    mod_row = rows.mod_row(tt)
    rt0, chunk0 = row0 // tt, pair_row0 // ch

    def tile_of(b, k, tile_ref):
        return tile_ref[b * n_items + k]

    yg_map = lambda b, k, tile_ref, chunk_ref, flag_ref: (chunk0 + b * n_chunks + chunk_ref[b * n_items + k], 0)
    row_map = lambda b, k, tile_ref, chunk_ref, flag_ref: (rt0 + b * tiles + tile_of(b, k, tile_ref), 0)
    tok_map = lambda b, k, tile_ref, chunk_ref, flag_ref: (b, 0, tile_of(b, k, tile_ref))
    gate_map = lambda b, k, tile_ref, chunk_ref, flag_ref: (
        mod_row(rt0 + b * tiles + tile_of(b, k, tile_ref)), 0, k_gate)
    return pl.pallas_call(
        functools.partial(_combine_kernel, ch=ch, normalize=normalize),
        grid_spec=pltpu.PrefetchScalarGridSpec(
            num_scalar_prefetch=3,
            grid=(batch, n_items),
            in_specs=[
                pl.BlockSpec((ch, d), yg_map),
                pl.BlockSpec((None, 1, tt), tok_map),
                pl.BlockSpec((None, 1, tt), tok_map),
                pl.BlockSpec((tt, d), row_map),
                pl.BlockSpec((1, 1, d), gate_map),
                pl.BlockSpec((1, d), lambda b, k, tile_ref, chunk_ref, flag_ref: (0, 0)),
            ],
            out_specs=pl.BlockSpec((tt, d), row_map),
            scratch_shapes=[pltpu.VMEM((tt, d), F32)],
        ),
        out_shape=jax.ShapeDtypeStruct(x.shape, F32),
        input_output_aliases={6: 0},
        compiler_params=_params("arbitrary", "arbitrary"),
        name="combine_residual",
    )(item_tile, item_chunk, item_flags, yg, start, cnt, x, mod3, norm_g.reshape(1, d))


def expert_choice_ffn_residual(x, h, logits_t, mod3, rows, k_gate, w1, w3, w2, layer, norm_g, *, batch, sets,
                               normalize):
    e_n = logits_t.shape[0]
    d = x.shape[1]
    boff = jnp.arange(batch, dtype=I32)[:, None, None]
    routed, src_rows, pair_rows, gate_cols, pair_row0 = [], [], [], [], 0
    for n, row0 in sets:
        idx, dest, gate, start, cnt = expert_choice_route(logits_t, batch=batch, n=n, col0=row0)
        cap = idx.shape[2]
        pairs = e_n * cap
        src_rows.append(jnp.transpose(idx[..., 0] + row0 + boff * n, (1, 0, 2)).reshape(e_n, batch * cap))
        pair_rows.append(jnp.transpose(dest[..., 0] + pair_row0 + boff * pairs, (1, 0, 2)).reshape(e_n, batch * cap))
        gate_cols.append(jnp.transpose(gate, (1, 0, 2, 3)).reshape(e_n, batch * cap, 1))
        routed.append((start, cnt, n, row0, pairs, pair_row0))
        pair_row0 += batch * pairs
    src_rows = jnp.concatenate(src_rows, axis=1)
    m = src_rows.shape[1]
    xg = gather_rows(h, src_rows.reshape(-1), BF16).reshape(e_n, m, d)
    y = expert_ffn(xg, w1, w3, w2, jnp.concatenate(gate_cols, axis=1), layer).reshape(e_n * m, d)
    yg = scatter_rows(y, jnp.concatenate(pair_rows, axis=1).reshape(-1), F32)
    for start, cnt, n, row0, pairs, pair_row0 in routed:
        x = combine_residual(yg, start, cnt, x, mod3, rows, k_gate, norm_g, batch=batch, n=n, row0=row0,
                             pairs=pairs, pair_row0=pair_row0, normalize=normalize)
    return x


def kernel(x, c, ctx, c_ctx, ada_w, ada_b, norm1_g, norm2_g, w_in, na_rpb, gmlp_norm_g, gmlp_ws, gmlp_bs,
           w_branch_a, w_branch_b, w_out, router_w, exp_w1, exp_w3, exp_w2, final_norm_g):
    batch, n_lat, d = x.shape
    n_ctx = ctx.shape[1]
    depth = ada_w.shape[0]
    rows = Rows(batch, n_lat, n_ctx)
    lat_rows, all_rows = rows.lat_rows, rows.all_rows
    off_k, off_v, off_u = NA_WIDTH, 2 * NA_WIDTH, 3 * NA_WIDTH
    off_gv = off_u + GMLP_WIDTH
    off_ga = off_gv + GMLP_WIDTH
    off_gb = off_ga + d
    in_cols = off_gb + d

    x_head, x_tail = x.reshape(lat_rows, d), ctx.reshape(batch * n_ctx, d)
    cv = jnp.zeros((MOD_ROWS, d), F32).at[:batch].set(c).at[batch].set(c_ctx)

    for layer in range(depth):
        last = layer == depth - 1
        nrows = lat_rows if last else all_rows
        mod3 = ada_mod(cv, ada_w, ada_b, layer).reshape(MOD_ROWS, 1, N_MOD * d)

        h = norm_mod(x_head, x_tail, norm1_g, layer, mod3, rows, 0, 1, 0, nrows, BF16)
        qkv = rest = matmul(h, w_in, layer, row0=0, nrows=nrows, col0=0, ncols=in_cols, out_dtype=BF16,
                            scaled_cols=NA_WIDTH, scale=NA_HEAD_DIM ** -0.5)
        bias_table = build_col_table(na_rpb[layer])
        if last:
            h_ctx = norm_mod(x_head, x_tail, norm1_g, layer, mod3, rows, 0, 1, lat_rows, batch * n_ctx, BF16)
            kvc = matmul(h_ctx, w_in, layer, row0=0, nrows=batch * n_ctx, col0=off_k, ncols=2 * NA_WIDTH,
                         out_dtype=BF16)
            o_lat = o_ctx = neighbourhood_attention(qkv, kvc, bias_table, batch=batch, n_lat=n_lat, n_ctx=n_ctx,
                                                    kc_row0=0, kc_col0=0, vc_col0=NA_WIDTH)
        else:
            o_lat = neighbourhood_attention(qkv, qkv, bias_table, batch=batch, n_lat=n_lat, n_ctx=n_ctx,
                                            kc_row0=lat_rows, kc_col0=off_k, vc_col0=off_v)
            o_ctx = context_attention(qkv, batch=batch, n_ctx=n_ctx, row0=lat_rows)
        o_b = spatial_gating(rest, gmlp_norm_g, gmlp_ws, gmlp_bs, layer, nrows=nrows, u_col0=off_u, gv_col0=off_gv)
        y = gated_merge(o_lat, o_ctx, o_b, w_branch_a, w_branch_b, rest, layer, nrows=nrows, ga_col0=off_ga,
                        gb_col0=off_gb)
        stream = outproj_residual(y, w_out, x_head, x_tail, mod3, layer, rows, 2, nrows=nrows)

        h2, logits_t = norm_mod_router(stream, norm2_g, layer, mod3, rows, 3, 4, router_w, nrows)
        sets = [(n_lat, 0)] if last else [(n_lat, 0), (n_ctx, lat_rows)]
        stream = expert_choice_ffn_residual(stream, h2, logits_t, mod3, rows, 5, exp_w1, exp_w3, exp_w2, layer,
                                            final_norm_g, batch=batch, sets=sets, normalize=last)
        x_head = x_tail = stream

    return stream.reshape(batch, n_lat, d)
```

```python
import functools

import jax
import jax.numpy as jnp
from jax import lax
from jax.experimental import pallas as pl
from jax.experimental.pallas import tpu as pltpu

F32, BF16, I32 = jnp.float32, jnp.bfloat16, jnp.int32

GRID_W = 64
NA_HEADS = 16
NA_HEAD_DIM = 128
NA_WIDTH = NA_HEADS * NA_HEAD_DIM
WIN_H = 8
WIN_W = 16
GMLP_GROUPS = 16
GMLP_WIDTH = 2048
CHUNK = 128
N_EXPERTS = 16
CAPACITY_FACTOR = 2
N_MOD = 6
EPS = 1e-6
MASK_BIAS = -1e30

V7X_VMEM_LIMIT_BYTES = 56 * 1024 * 1024
LANES = 128
MOD_ROWS = 8
NORM_ROW_TILE = 512


def _params(*sem):
    return pltpu.CompilerParams(dimension_semantics=sem, vmem_limit_bytes=V7X_VMEM_LIMIT_BYTES)


def _cast_weight(w_ref, wb_ref):
    rows = w_ref.shape[0]
    step = min(rows, 512)

    def body(i, carry):
        r = pl.multiple_of(i * step, step)
        wb_ref[pl.ds(r, step), :] = w_ref[pl.ds(r, step), :].astype(BF16)
        return carry

    lax.fori_loop(0, rows // step, body, 0)


def _ada_kernel(c_ref, w_ref, b_ref, o_ref):
    c = c_ref[...]
    s = c * jax.nn.sigmoid(c)
    o_ref[...] = jnp.dot(s.astype(BF16), w_ref[...].astype(BF16), preferred_element_type=F32) + b_ref[...]


def ada_mod(cv, ada_w, ada_b, layer):
    _, d, m = ada_w.shape
    tn = min(m, 512)
    return pl.pallas_call(
        _ada_kernel,
        grid=(m // tn,),
        in_specs=[
            pl.BlockSpec((MOD_ROWS, d), lambda j: (0, 0)),
            pl.BlockSpec((None, d, tn), lambda j: (layer, 0, j)),
            pl.BlockSpec((None, 1, tn), lambda j: (layer, 0, j)),
        ],
        out_specs=pl.BlockSpec((MOD_ROWS, tn), lambda j: (0, j)),
        out_shape=jax.ShapeDtypeStruct((MOD_ROWS, m), F32),
        compiler_params=_params("arbitrary"),
        name="ada_mod",
    )(cv, ada_w, ada_b.reshape(ada_b.shape[0], 1, m))


class Rows:
    def __init__(self, batch, n_lat, n_ctx):
        self.batch, self.n_lat, self.n_ctx = batch, n_lat, n_ctx
        self.lat_rows = batch * n_lat
        self.all_rows = self.lat_rows + batch * n_ctx

    def mod_row(self, tile_rows):
        lat_tiles = self.lat_rows // tile_rows
        per_batch = self.n_lat // tile_rows
        batch = self.batch

        def f(i):
            return jnp.where(i < lat_tiles, i // per_batch, batch)

        return f


def _mod_spec(mod_row, k, d):
    return pl.BlockSpec((1, 1, d), lambda i: (mod_row(i), 0, k))


def _rms(x, g):
    return x * lax.rsqrt(jnp.mean(x * x, axis=-1, keepdims=True) + EPS) * g


def _two_source_specs(block, head_tiles, tile_of):
    def head(*ids):
        t, c = tile_of(*ids)
        return (jnp.minimum(t, head_tiles - 1), c)

    def tail(*ids):
        t, c = tile_of(*ids)
        return (jnp.maximum(t - head_tiles, 0), c)

    return pl.BlockSpec(block, head), pl.BlockSpec(block, tail)


def _norm_mod_kernel(xh_ref, xt_ref, g_ref, sh_ref, sc_ref, o_ref, *, head_tiles, t0):
    def emit(x_ref):
        y = _rms(x_ref[...], g_ref[...])
        o_ref[...] = (y * (1.0 + sc_ref[0]) + sh_ref[0]).astype(o_ref.dtype)

    in_head = pl.program_id(0) + t0 < head_tiles
    pl.when(in_head)(lambda: emit(xh_ref))
    pl.when(jnp.logical_not(in_head))(lambda: emit(xt_ref))


def norm_mod(x_head, x_tail, g, layer, mod3, rows, k_shift, k_scale, row0, nrows, out_dtype):
    d = x_head.shape[1]
    tr = NORM_ROW_TILE
    mod_row = rows.mod_row(tr)
    t0 = row0 // tr
    head_tiles = x_head.shape[0] // tr
    head_spec, tail_spec = _two_source_specs((tr, d), head_tiles, lambda i: (i + t0, 0))
    return pl.pallas_call(
        functools.partial(_norm_mod_kernel, head_tiles=head_tiles, t0=t0),
        grid=(nrows // tr,),
        in_specs=[
            head_spec,
            tail_spec,
            pl.BlockSpec((None, 1, d), lambda i: (layer, 0, 0)),
            pl.BlockSpec((1, 1, d), lambda i: (mod_row(i + t0), 0, k_shift)),
            pl.BlockSpec((1, 1, d), lambda i: (mod_row(i + t0), 0, k_scale)),
        ],
        out_specs=pl.BlockSpec((tr, d), lambda i: (i, 0)),
        out_shape=jax.ShapeDtypeStruct((nrows, d), out_dtype),
        compiler_params=_params("arbitrary"),
        name="norm_mod",
    )(x_head, x_tail, g.reshape(g.shape[0], 1, d), mod3, mod3)


WEIGHT_TILE_BUFFERS = pl.Buffered(1)
MATMUL_ROW_TILES = 8

def _mm_kernel(a_ref, w_ref, o_ref, wb_ref, *, scaled_tiles, scale):
    @pl.when(pl.program_id(1) == 0)
    def _():
        _cast_weight(w_ref, wb_ref)

    acc = jnp.dot(a_ref[...], wb_ref[...], preferred_element_type=F32)
    if scaled_tiles:
        acc = acc * jnp.where(pl.program_id(0) < scaled_tiles, scale, 1.0)
    o_ref[...] = acc.astype(o_ref.dtype)


def matmul(a, w, layer, *, row0, nrows, col0, ncols, out_dtype, scaled_cols=0, scale=1.0, tm=512, tn=1024):
    k = a.shape[1]
    if nrows % (MATMUL_ROW_TILES * 16) == 0 and nrows // MATMUL_ROW_TILES > tm:
        tm = nrows // MATMUL_ROW_TILES
    tm, tn = min(tm, nrows), min(tn, ncols)
    r0, c0 = row0 // tm, col0 // tn
    assert scaled_cols % tn == 0 and row0 % tm == 0
    return pl.pallas_call(
        functools.partial(_mm_kernel, scaled_tiles=scaled_cols // tn, scale=scale),
        grid=(ncols // tn, nrows // tm),
        in_specs=[
            pl.BlockSpec((tm, k), lambda j, i: (i + r0, 0)),
            pl.BlockSpec((None, k, tn), lambda j, i: (layer, 0, j + c0), pipeline_mode=WEIGHT_TILE_BUFFERS),
        ],
        out_specs=pl.BlockSpec((tm, tn), lambda j, i: (i, j)),
        out_shape=jax.ShapeDtypeStruct((nrows, ncols), out_dtype),
        scratch_shapes=[pltpu.VMEM((k, tn), BF16)],
        compiler_params=_params("arbitrary", "arbitrary"),
        name="matmul",
    )(a, w)


def _merge_kernel(oa_ref, ob_ref, wa_ref, wb_ref, ga_ref, gb_ref, y_ref, wab_ref, wbb_ref):
    @pl.when(pl.program_id(1) == 0)
    def _():
        _cast_weight(wa_ref, wab_ref)
        _cast_weight(wb_ref, wbb_ref)

    ya = jnp.dot(oa_ref[...], wab_ref[...], preferred_element_type=F32)
    yb = jnp.dot(ob_ref[...], wbb_ref[...], preferred_element_type=F32)
    y = jax.nn.sigmoid(ga_ref[...].astype(F32)) * ya + jax.nn.sigmoid(gb_ref[...].astype(F32)) * yb
    y_ref[...] = y.astype(y_ref.dtype)


def gated_merge(o_a, o_b, w_a, w_b, gates, layer, *, nrows, ga_col0, gb_col0, tm=512, tn=1024):
    ka, kb = o_a.shape[1], o_b.shape[1]
    d = w_a.shape[2]
    tm, tn = min(tm, nrows), min(tn, d)
    ca, cb = ga_col0 // tn, gb_col0 // tn
    return pl.pallas_call(
        _merge_kernel,
        grid=(d // tn, nrows // tm),
        in_specs=[
            pl.BlockSpec((tm, ka), lambda j, i: (i, 0)),
            pl.BlockSpec((tm, kb), lambda j, i: (i, 0)),
            pl.BlockSpec((None, ka, tn), lambda j, i: (layer, 0, j), pipeline_mode=WEIGHT_TILE_BUFFERS),
            pl.BlockSpec((None, kb, tn), lambda j, i: (layer, 0, j), pipeline_mode=WEIGHT_TILE_BUFFERS),
            pl.BlockSpec((tm, tn), lambda j, i: (i, j + ca)),
            pl.BlockSpec((tm, tn), lambda j, i: (i, j + cb)),
        ],
        out_specs=pl.BlockSpec((tm, tn), lambda j, i: (i, j)),
        out_shape=jax.ShapeDtypeStruct((nrows, d), BF16),
        scratch_shapes=[pltpu.VMEM((ka, tn), BF16), pltpu.VMEM((kb, tn), BF16)],
        compiler_params=_params("arbitrary", "arbitrary"),
        name="gated_merge",
    )(o_a, o_b, w_a, w_b, gates, gates)


def _outproj_kernel(y_ref, w_ref, xh_ref, xt_ref, gt_ref, o_ref, wb_ref, *, head_tiles):
    @pl.when(pl.program_id(1) == 0)
    def _():
        _cast_weight(w_ref, wb_ref)

    upd = gt_ref[0] * jnp.dot(y_ref[...], wb_ref[...], preferred_element_type=F32)

    def emit(x_ref):
        o_ref[...] = x_ref[...] + upd

    in_head = pl.program_id(1) < head_tiles
    pl.when(in_head)(lambda: emit(xh_ref))
    pl.when(jnp.logical_not(in_head))(lambda: emit(xt_ref))


def outproj_residual(y, w_out, x_head, x_tail, mod3, layer, rows, k_gate, *, nrows, tm=512, tn=1024):
    k = y.shape[1]
    d = w_out.shape[2]
    tm, tn = min(tm, nrows), min(tn, d)
    mod_row = rows.mod_row(tm)
    gate_col0 = k_gate * (d // tn)
    head_tiles = x_head.shape[0] // tm
    head_spec, tail_spec = _two_source_specs((tm, tn), head_tiles, lambda j, i: (i, j))
    return pl.pallas_call(
        functools.partial(_outproj_kernel, head_tiles=head_tiles),
        grid=(d // tn, nrows // tm),
        in_specs=[
            pl.BlockSpec((tm, k), lambda j, i: (i, 0)),
            pl.BlockSpec((None, k, tn), lambda j, i: (layer, 0, j), pipeline_mode=WEIGHT_TILE_BUFFERS),
            head_spec,
            tail_spec,
            pl.BlockSpec((1, 1, tn), lambda j, i: (mod_row(i), 0, j + gate_col0)),
        ],
        out_specs=pl.BlockSpec((tm, tn), lambda j, i: (i, j)),
        out_shape=jax.ShapeDtypeStruct((nrows, d), F32),
        scratch_shapes=[pltpu.VMEM((k, tn), BF16)],
        compiler_params=_params("arbitrary", "arbitrary"),
        name="outproj_residual",
    )(y, w_out, x_head, x_tail, mod3)


Q_ROWS = 4
BAND_ROWS = WIN_H + Q_ROWS
N_PATTERNS = 3


def build_col_table(rpb):
    cols = jnp.arange(GRID_W)
    col_start = jnp.clip(cols - WIN_W // 2, 0, GRID_W - WIN_W)
    col_ok = (cols[None, :] >= col_start[:, None]) & (cols[None, :] < col_start[:, None] + WIN_W)
    col_off = cols[None, :] - cols[:, None] + (WIN_W - 1)
    pick = (col_off[None] == jnp.arange(2 * WIN_W - 1)[:, None, None]).astype(F32)
    t = jnp.einsum("hrj,jqk->hrqk", rpb.astype(F32), pick, precision=lax.Precision.HIGHEST)
    return jnp.where(col_ok[None, None], t, MASK_BIAS)


def _band_patterns(grid_rows):
    half = WIN_H // 2
    groups = grid_rows // Q_ROWS
    assert grid_rows % Q_ROWS == 0 and grid_rows >= BAND_ROWS
    patterns = []
    for g in (0, min(1, groups - 1), groups - 1):
        band_start = min(max(g * Q_ROWS - half, 0), grid_rows - BAND_ROWS)
        pattern = []
        for a in range(Q_ROWS):
            q_row = g * Q_ROWS + a
            win_start = min(max(q_row - half, 0), grid_rows - WIN_H)
            pattern.append([band_start + i - q_row + WIN_H - 1 if win_start <= band_start + i < win_start + WIN_H
                            else None for i in range(BAND_ROWS)])
        patterns.append(pattern)
    return patterns


def _softmax_pv(parts):
    m = functools.reduce(jnp.maximum, [jnp.max(s, axis=-1, keepdims=True) for s, _ in parts])
    es = [jnp.exp(s - m) for s, _ in parts]
    denom = functools.reduce(jnp.add, [jnp.sum(e, axis=-1, keepdims=True) for e in es])
    acc = functools.reduce(jnp.add, [jnp.dot(e.astype(BF16), v, preferred_element_type=F32)
                                     for e, (_, v) in zip(es, parts)])
    return acc / denom


def _qk(q, k):
    return lax.dot_general(q, k, (((1,), (1,)), ((), ())), preferred_element_type=F32)


def _natten_kernel(q_ref, k_ref, v_ref, kc_ref, vc_ref, ct_ref, o_ref, bt_ref, *, grid_rows):
    groups = grid_rows // Q_ROWS
    nq, band = Q_ROWS * GRID_W, BAND_ROWS * GRID_W
    kc = kc_ref[...]
    vc = vc_ref[...]

    @pl.when(jnp.logical_and(pl.program_id(0) == 0, pl.program_id(1) == 0))
    def _():
        bt_ref[...] = jnp.full(bt_ref.shape, MASK_BIAS, F32)

    @pl.when(pl.program_id(1) == 0)
    def _():
        for p, pattern in enumerate(_band_patterns(grid_rows)):
            for a, row in enumerate(pattern):
                for i, off in enumerate(row):
                    if off is not None:
                        bt_ref[p, a * GRID_W:(a + 1) * GRID_W, i * GRID_W:(i + 1) * GRID_W] = ct_ref[off]

    def body(g, carry):
        band_start = jnp.clip(g * Q_ROWS - WIN_H // 2, 0, grid_rows - BAND_ROWS)
        pattern = jnp.where(g == 0, 0, jnp.where(g == groups - 1, 2, 1))
        q0 = pl.multiple_of(g * nq, nq)
        k0 = pl.multiple_of(band_start * GRID_W, GRID_W)
        q = q_ref[pl.ds(q0, nq), :]
        kb = k_ref[pl.ds(k0, band), :]
        vb = v_ref[pl.ds(k0, band), :]
        s_loc = _qk(q, kb) + bt_ref[pattern]
        s_ctx = _qk(q, kc)
        o_ref[pl.ds(q0, nq), :] = _softmax_pv([(s_loc, vb), (s_ctx, vc)]).astype(o_ref.dtype)
        return carry

    lax.fori_loop(0, groups, body, 0, unroll=4)


def neighbourhood_attention(qkv, kvc, col_table, *, batch, n_lat, n_ctx, kc_row0, kc_col0, vc_col0):
    dh, h = NA_HEAD_DIM, NA_HEADS
    grid_rows = n_lat // GRID_W
    rb0, ck, cv = kc_row0 // n_ctx, kc_col0 // dh, vc_col0 // dh
    return pl.pallas_call(
        functools.partial(_natten_kernel, grid_rows=grid_rows),
        grid=(h, batch),
        in_specs=[
            pl.BlockSpec((n_lat, dh), lambda hh, b: (b, hh)),
            pl.BlockSpec((n_lat, dh), lambda hh, b: (b, h + hh)),
            pl.BlockSpec((n_lat, dh), lambda hh, b: (b, 2 * h + hh)),
            pl.BlockSpec((n_ctx, dh), lambda hh, b: (rb0 + b, ck + hh)),
            pl.BlockSpec((n_ctx, dh), lambda hh, b: (rb0 + b, cv + hh)),
            pl.BlockSpec((None, 2 * WIN_H - 1, GRID_W, GRID_W), lambda hh, b: (hh, 0, 0, 0)),
        ],
        out_specs=pl.BlockSpec((n_lat, dh), lambda hh, b: (b, hh)),
        out_shape=jax.ShapeDtypeStruct((batch * n_lat, NA_WIDTH), BF16),
        scratch_shapes=[pltpu.VMEM((N_PATTERNS, Q_ROWS * GRID_W, BAND_ROWS * GRID_W), F32)],
        compiler_params=_params("arbitrary", "arbitrary"),
        name="neighbourhood_attention",
    )(qkv, qkv, qkv, kvc, kvc, col_table)


def _ctx_attn_kernel(q_ref, k_ref, v_ref, o_ref):
    o_ref[...] = _softmax_pv([(_qk(q_ref[...], k_ref[...]), v_ref[...])]).astype(o_ref.dtype)


def context_attention(qkv, *, batch, n_ctx, row0):
    dh, h = NA_HEAD_DIM, NA_HEADS
    rb0 = row0 // n_ctx
    return pl.pallas_call(
        _ctx_attn_kernel,
        grid=(batch, h),
        in_specs=[
            pl.BlockSpec((n_ctx, dh), lambda b, hh: (rb0 + b, hh)),
            pl.BlockSpec((n_ctx, dh), lambda b, hh: (rb0 + b, h + hh)),
            pl.BlockSpec((n_ctx, dh), lambda b, hh: (rb0 + b, 2 * h + hh)),
        ],
        out_specs=pl.BlockSpec((n_ctx, dh), lambda b, hh: (b, hh)),
        out_shape=jax.ShapeDtypeStruct((batch * n_ctx, NA_WIDTH), BF16),
        compiler_params=_params("arbitrary", "arbitrary"),
        name="context_attention",
    )(qkv, qkv, qkv)


def _gmlp_kernel(u_ref, gv_ref, ng_ref, ws_ref, bst_ref, o_ref):
    gd = GMLP_WIDTH // GMLP_GROUPS
    for c in range(u_ref.shape[0] // CHUNK):
        rows = slice(c * CHUNK, (c + 1) * CHUNK)
        v = jax.nn.gelu(gv_ref[rows, :].astype(F32))
        vc = v - jnp.mean(v, axis=-1, keepdims=True)
        y = vc * lax.rsqrt(jnp.mean(vc * vc, axis=-1, keepdims=True) + EPS) * ng_ref[...]
        yb = y.astype(BF16)
        for g in range(GMLP_GROUPS):
            sl = slice(g * gd, (g + 1) * gd)
            mixed = jnp.dot(ws_ref[g].astype(BF16), yb[:, sl], preferred_element_type=F32) + bst_ref[:, g:g + 1]
            o_ref[rows, sl] = (jax.nn.gelu(u_ref[rows, sl].astype(F32)) * mixed).astype(o_ref.dtype)


def spatial_gating(p, norm_g, ws, bs, layer, *, nrows, u_col0, gv_col0):
    w = GMLP_WIDTH
    bst = jnp.transpose(bs[layer])
    tr = 2 * CHUNK
    return pl.pallas_call(
        _gmlp_kernel,
        grid=(nrows // tr,),
        in_specs=[
            pl.BlockSpec((tr, w), lambda i: (i, u_col0 // w)),
            pl.BlockSpec((tr, w), lambda i: (i, gv_col0 // w)),
            pl.BlockSpec((None, 1, w), lambda i: (layer, 0, 0)),
            pl.BlockSpec((None, GMLP_GROUPS, CHUNK, CHUNK), lambda i: (layer, 0, 0, 0)),
            pl.BlockSpec((CHUNK, GMLP_GROUPS), lambda i: (0, 0)),
        ],
        out_specs=pl.BlockSpec((tr, w), lambda i: (i, 0)),
        out_shape=jax.ShapeDtypeStruct((nrows, w), BF16),
        compiler_params=_params("arbitrary"),
        name="spatial_gating",
    )(p, p, norm_g.reshape(norm_g.shape[0], 1, w), ws, bst)


def _split_bf16(x):
    hi = x.astype(BF16)
    return hi, (x - hi.astype(F32)).astype(BF16)


def _norm_router_kernel(x_ref, g_ref, sh_ref, sc_ref, rw_ref, h_ref, lg_ref):
    h = _rms(x_ref[...], g_ref[...]) * (1.0 + sc_ref[0]) + sh_ref[0]
    h_ref[...] = h
    hh, hl = _split_bf16(h)
    wh, wl = _split_bf16(rw_ref[...])
    lg = (jnp.dot(hh, wh, preferred_element_type=F32) + jnp.dot(hl, wh, preferred_element_type=F32)
          + jnp.dot(hh, wl, preferred_element_type=F32))
    lg_ref[...] = jnp.transpose(lg)[:N_EXPERTS, :]


def norm_mod_router(x, g, layer, mod3, rows, k_shift, k_scale, router_w, nrows):
    d = x.shape[1]
    tr = NORM_ROW_TILE
    mod_row = rows.mod_row(tr)
    rw = jnp.pad(router_w[layer], ((0, 0), (0, LANES - N_EXPERTS)))
    return pl.pallas_call(
        _norm_router_kernel,
        grid=(nrows // tr,),
        in_specs=[
            pl.BlockSpec((tr, d), lambda i: (i, 0)),
            pl.BlockSpec((None, 1, d), lambda i: (layer, 0, 0)),
            pl.BlockSpec((1, 1, d), lambda i: (mod_row(i), 0, k_shift)),
            pl.BlockSpec((1, 1, d), lambda i: (mod_row(i), 0, k_scale)),
            pl.BlockSpec((d, LANES), lambda i: (0, 0)),
        ],
        out_specs=[pl.BlockSpec((tr, d), lambda i: (i, 0)), pl.BlockSpec((N_EXPERTS, tr), lambda i: (0, i))],
        out_shape=[jax.ShapeDtypeStruct((nrows, d), F32), jax.ShapeDtypeStruct((N_EXPERTS, nrows), F32)],
        compiler_params=_params("arbitrary"),
        name="norm_mod_router",
    )(x, g.reshape(g.shape[0], 1, d), mod3, mod3, rw)


def _cumsum_lanes(x):
    r, n = x.shape
    blk = min(n, LANES)
    tri = (lax.broadcasted_iota(I32, (blk, blk), 0) <= lax.broadcasted_iota(I32, (blk, blk), 1)).astype(BF16)
    off = jnp.zeros((r, 1), F32)
    out = []
    for j in range(n // blk):
        cs = jnp.dot(x[:, j * blk:(j + 1) * blk].astype(BF16), tri, preferred_element_type=F32) + off
        out.append(cs)
        off = cs[:, blk - 1:blk]
    return jnp.concatenate(out, axis=1) if len(out) > 1 else out[0]


def _route_kernel(lg_ref, idx_ref, dest_ref, gate_ref, start_ref, cnt_ref, slot_s, aff_s, *, cap):
    e_n, n = lg_ref.shape
    lg = lg_ref[...]
    ex = jnp.exp(lg - jnp.max(lg, axis=0, keepdims=True))
    aff = ex / jnp.sum(ex, axis=0, keepdims=True)
    bits = lax.bitcast_convert_type(aff, I32)

    thr = jnp.zeros((e_n, 1), I32)
    for bit in range(30, -1, -1):
        cand = thr | (1 << bit)
        n_ge = jnp.sum((bits >= cand).astype(F32), axis=1, keepdims=True)
        thr = jnp.where(n_ge >= cap, cand, thr)
    above = bits > thr
    tied = (bits == thr).astype(F32)
    need = cap - jnp.sum(above.astype(F32), axis=1, keepdims=True)
    tied_before = _cumsum_lanes(tied) - tied
    sel = jnp.where(above, 1.0, jnp.where(tied_before < need, tied, 0.0))

    slot = _cumsum_lanes(sel) - sel
    per_tok = jnp.sum(sel, axis=0, keepdims=True)
    per_tok8 = jnp.broadcast_to(per_tok, (8, n))
    start = (_cumsum_lanes(per_tok8) - per_tok8)[0:1]
    start_ref[...] = start.astype(I32)
    cnt_ref[...] = per_tok.astype(I32)
    slot_s[...] = jnp.where(sel > 0.0, slot, -1.0)
    aff_s[...] = aff

    tok = lax.broadcasted_iota(I32, (1, n), 1).astype(F32)
    cc = min(cap, LANES)

    def body(e, chosen_by_earlier):
        srow = slot_s[pl.ds(e, 1), :]
        arow = aff_s[pl.ds(e, 1), :]
        drow = start + chosen_by_earlier
        for c0 in range(0, cap, cc):
            want = (lax.broadcasted_iota(I32, (cc, 1), 0) + c0).astype(F32)
            hit = srow == want
            pick = lambda row: jnp.sum(jnp.where(hit, row, 0.0), axis=1, keepdims=True)
            idx_ref[e, pl.ds(c0, cc), :] = pick(tok).astype(I32)
            dest_ref[e, pl.ds(c0, cc), :] = pick(drow).astype(I32)
            gate_ref[e, pl.ds(c0, cc), :] = pick(arow)
        return chosen_by_earlier + (srow >= 0.0).astype(F32)

    lax.fori_loop(0, e_n, body, jnp.zeros((1, n), F32))


def expert_choice_route(logits_t, *, batch, n, col0):
    e_n = logits_t.shape[0]
    cap = CAPACITY_FACTOR * n // e_n
    cb0 = col0 // n
    per_slot = lambda dt: jax.ShapeDtypeStruct((batch, e_n, cap, 1), dt)
    per_tok = jax.ShapeDtypeStruct((batch, 1, n), I32)
    slot_spec = pl.BlockSpec((None, e_n, cap, 1), lambda b: (b, 0, 0, 0))
    tok_spec = pl.BlockSpec((None, 1, n), lambda b: (b, 0, 0))
    return pl.pallas_call(
        functools.partial(_route_kernel, cap=cap),
        grid=(batch,),
        in_specs=[pl.BlockSpec((e_n, n), lambda b: (0, cb0 + b))],
        out_specs=[slot_spec, slot_spec, slot_spec, tok_spec, tok_spec],
        out_shape=[per_slot(I32), per_slot(I32), per_slot(F32), per_tok, per_tok],
        scratch_shapes=[pltpu.VMEM((e_n, n), F32), pltpu.VMEM((e_n, n), F32)],
        compiler_params=_params("arbitrary"),
        name="expert_choice_route",
    )(logits_t)


ROWS_PER_STEP = 512
ROW_COPY_UNROLL = 8


def _start_rows(n, make_copy):
    def start(r, carry):
        make_copy(r).start()
        return carry

    lax.fori_loop(0, n, start, 0, unroll=ROW_COPY_UNROLL)


def _wait_rows(n, make_copy):
    def wait(r, carry):
        make_copy(r).wait()
        return carry

    lax.fori_loop(0, n, wait, 0, unroll=ROW_COPY_UNROLL)


def _gather_kernel(idx_ref, src_ref, o_ref, stage_ref, sem):
    step, n_steps, rows = pl.program_id(0), pl.num_programs(0), o_ref.shape[0]

    def copies(s):
        slot = s % 2
        return lambda r: pltpu.make_async_copy(src_ref.at[pl.ds(idx_ref[s * rows + r], 1), :],
                                               stage_ref.at[slot, pl.ds(r, 1), :], sem.at[slot])

    @pl.when(step == 0)
    def _():
        _start_rows(rows, copies(step))

    @pl.when(step + 1 < n_steps)
    def _():
        _start_rows(rows, copies(step + 1))

    _wait_rows(rows, copies(step))
    o_ref[...] = stage_ref[step % 2].astype(o_ref.dtype)


def gather_rows(src, idx, out_dtype):
    n, d = idx.shape[0], src.shape[1]
    rows = min(n, ROWS_PER_STEP)
    return pl.pallas_call(
        _gather_kernel,
        grid_spec=pltpu.PrefetchScalarGridSpec(
            num_scalar_prefetch=1,
            grid=(n // rows,),
            in_specs=[pl.BlockSpec(memory_space=pl.ANY)],
            out_specs=pl.BlockSpec((rows, d), lambda i, idx_ref: (i, 0)),
            scratch_shapes=[pltpu.VMEM((2, rows, d), src.dtype), pltpu.SemaphoreType.DMA((2,))],
        ),
        out_shape=jax.ShapeDtypeStruct((n, d), out_dtype),
        compiler_params=_params("arbitrary"),
        name="gather_rows",
    )(idx, src)


def _scatter_kernel(idx_ref, x_ref, o_ref, stage_ref, sem):
    step, n_steps, rows = pl.program_id(0), pl.num_programs(0), x_ref.shape[0]

    def copies(s):
        slot = s % 2
        return lambda r: pltpu.make_async_copy(stage_ref.at[slot, pl.ds(r, 1), :],
                                               o_ref.at[pl.ds(idx_ref[s * rows + r], 1), :], sem.at[slot])

    stage_ref[step % 2] = x_ref[...].astype(stage_ref.dtype)
    _start_rows(rows, copies(step))

    @pl.when(step > 0)
    def _():
        _wait_rows(rows, copies(step - 1))

    @pl.when(step + 1 == n_steps)
    def _():
        _wait_rows(rows, copies(step))


def scatter_rows(x, idx, out_dtype):
    n, d = x.shape
    rows = min(n, ROWS_PER_STEP)
    return pl.pallas_call(
        _scatter_kernel,
        grid_spec=pltpu.PrefetchScalarGridSpec(
            num_scalar_prefetch=1,
            grid=(n // rows,),
            in_specs=[pl.BlockSpec((rows, d), lambda i, idx_ref: (i, 0))],
            out_specs=pl.BlockSpec(memory_space=pl.ANY),
            scratch_shapes=[pltpu.VMEM((2, rows, d), out_dtype), pltpu.SemaphoreType.DMA((2,))],
        ),
        out_shape=jax.ShapeDtypeStruct((n, d), out_dtype),
        compiler_params=_params("arbitrary"),
        name="scatter_rows",
    )(idx, x)


def _ffn_kernel(xg_ref, w1_ref, w3_ref, w2_ref, g_ref, y_ref, hid_s, *, f1, tf):
    s = pl.program_id(1)

    @pl.when(s < f1)
    def _():
        xs = xg_ref[...]
        h1 = jnp.dot(xs, w1_ref[...].astype(BF16), preferred_element_type=F32)
        h3 = jnp.dot(xs, w3_ref[...].astype(BF16), preferred_element_type=F32)
        hid_s[s] = (jax.nn.silu(h1) * h3).astype(BF16)

    @pl.when(s >= f1)
    def _():
        acc = jnp.dot(hid_s[0], w2_ref[0:tf, :].astype(BF16), preferred_element_type=F32)
        for f in range(1, f1):
            acc += jnp.dot(hid_s[f], w2_ref[f * tf:(f + 1) * tf, :].astype(BF16), preferred_element_type=F32)
        y_ref[...] = (acc * g_ref[...]).astype(y_ref.dtype)


def expert_ffn(xg, w1, w3, w2, gate, layer):
    e_n, m, d = xg.shape
    ff = w1.shape[3]
    tf, tn = min(ff, 256), min(d, 512)
    f1, f2 = ff // tf, d // tn
    out_step = lambda s: jnp.maximum(s - f1, 0)
    next_e = lambda e: jnp.minimum(e + 1, e_n - 1)

    def rows_map(e, s):
        return (jnp.where(s < f1, e, next_e(e)), 0, 0)

    def up_map(switch_step):
        def index(e, s):
            switched = s >= f1 + switch_step
            return (layer, jnp.where(switched, next_e(e), e), 0, jnp.where(switched, 0, jnp.minimum(s, f1 - 1)))
        return index

    def down_map(e, s):
        hold = jnp.logical_and(s == 0, e > 0)
        return (layer, jnp.where(hold, e - 1, e), 0, jnp.where(hold, f2 - 1, out_step(s)))

    return pl.pallas_call(
        functools.partial(_ffn_kernel, f1=f1, tf=tf),
        grid=(e_n, f1 + f2),
        in_specs=[
            pl.BlockSpec((None, m, d), rows_map),
            pl.BlockSpec((None, None, d, tf), up_map(f2 // 3)),
            pl.BlockSpec((None, None, d, tf), up_map(2 * f2 // 3)),
            pl.BlockSpec((None, None, ff, tn), down_map),
            pl.BlockSpec((None, m, 1), lambda e, s: (e, 0, 0)),
        ],
        out_specs=pl.BlockSpec((None, m, tn), lambda e, s: (e, 0, out_step(s))),
        out_shape=jax.ShapeDtypeStruct((e_n, m, d), BF16),
        scratch_shapes=[pltpu.VMEM((f1, m, tf), BF16)],
        compiler_params=_params("arbitrary", "arbitrary"),
        name="expert_ffn",
    )(xg, w1, w3, w2, gate)


ITEM_ADD, ITEM_FIRST, ITEM_LAST = 1, 2, 4


def _combine_kernel(tile_ref, chunk_ref, flag_ref, yg_ref, st_ref, ct_ref, x_ref, gt_ref, ng_ref, o_ref, acc_s, *,
                    ch, normalize):
    del tile_ref
    item = pl.program_id(0) * pl.num_programs(1) + pl.program_id(1)
    flags = flag_ref[item]

    @pl.when((flags & ITEM_FIRST) != 0)
    def _():
        acc_s[...] = jnp.zeros_like(acc_s)

    @pl.when((flags & ITEM_ADD) != 0)
    def _():
        pair = chunk_ref[item] * ch + lax.broadcasted_iota(I32, (ch, 1), 0)
        rel = pair - st_ref[...]
        own_t = jnp.where(jnp.logical_and(rel >= 0, rel < ct_ref[...]), 1.0, 0.0).astype(BF16)
        acc_s[...] += lax.dot_general(own_t, yg_ref[...].astype(BF16), (((0,), (0,)), ((), ())),
                                      preferred_element_type=F32)

    @pl.when((flags & ITEM_LAST) != 0)
    def _():
        out = x_ref[...] + gt_ref[0] * acc_s[...]
        o_ref[...] = _rms(out, ng_ref[...]) if normalize else out


def _combine_work_items(start, *, batch, n, tt, pairs, ch):
    tiles, n_chunks = n // tt, pairs // ch
    n_items = tiles + n_chunks
    first = start.reshape(batch, n)[:, ::tt]
    last = jnp.concatenate([first[:, 1:], jnp.full((batch, 1), pairs, I32)], axis=1)
    c_first = jnp.minimum(first // ch, n_chunks - 1)
    c_num = jnp.where(last > first, (last - 1) // ch - first // ch + 1, 0)
    t_items = jnp.maximum(c_num, 1)
    ends = jnp.cumsum(t_items, axis=1)
    k = jnp.arange(n_items, dtype=I32)[None, :]
    tile = jnp.minimum(jnp.sum((ends[:, None, :] <= k[:, :, None]).astype(I32), axis=-1), tiles - 1)
    at = lambda v: jnp.take_along_axis(v, tile, axis=1)
    j = k - at(ends - t_items)
    real = k < ends[:, -1:]
    chunk = at(c_first) + jnp.minimum(j, jnp.maximum(at(c_num) - 1, 0))
    flags = (ITEM_ADD * (real & (j < at(c_num))) + ITEM_FIRST * (real & (j == 0))
             + ITEM_LAST * (real & (j == at(t_items) - 1)))
    flat = lambda v: v.reshape(-1).astype(I32)
    return n_items, flat(tile), flat(chunk), flat(flags)


def combine_residual(yg, start, cnt, x, mod3, rows, k_gate, norm_g, *, batch, n, row0, pairs, pair_row0, normalize):
    d = x.shape[1]
    tt = min(n, 256)
    ch = min(pairs, 512)
    tiles, n_chunks = n // tt, pairs // ch
    n_items, item_tile, item_chunk, item_flags = _combine_work_items(start, batch=batch, n=n, tt=tt, pairs=pairs,
                                                                     ch=ch)
    mod_row = rows.mod_row(tt)
    rt0, chunk0 = row0 // tt, pair_row0 // ch

    def tile_of(b, k, tile_ref):
        return tile_ref[b * n_items + k]

    yg_map = lambda b, k, tile_ref, chunk_ref, flag_ref: (chunk0 + b * n_chunks + chunk_ref[b * n_items + k], 0)
    row_map = lambda b, k, tile_ref, chunk_ref, flag_ref: (rt0 + b * tiles + tile_of(b, k, tile_ref), 0)
    tok_map = lambda b, k, tile_ref, chunk_ref, flag_ref: (b, 0, tile_of(b, k, tile_ref))
    gate_map = lambda b, k, tile_ref, chunk_ref, flag_ref: (
        mod_row(rt0 + b * tiles + tile_of(b, k, tile_ref)), 0, k_gate)
    return pl.pallas_call(
        functools.partial(_combine_kernel, ch=ch, normalize=normalize),
        grid_spec=pltpu.PrefetchScalarGridSpec(
            num_scalar_prefetch=3,
            grid=(batch, n_items),
            in_specs=[
                pl.BlockSpec((ch, d), yg_map),
                pl.BlockSpec((None, 1, tt), tok_map),
                pl.BlockSpec((None, 1, tt), tok_map),
                pl.BlockSpec((tt, d), row_map),
                pl.BlockSpec((1, 1, d), gate_map),
                pl.BlockSpec((1, d), lambda b, k, tile_ref, chunk_ref, flag_ref: (0, 0)),
            ],
            out_specs=pl.BlockSpec((tt, d), row_map),
            scratch_shapes=[pltpu.VMEM((tt, d), F32)],
        ),
        out_shape=jax.ShapeDtypeStruct(x.shape, F32),
        input_output_aliases={6: 0},
        compiler_params=_params("arbitrary", "arbitrary"),
        name="combine_residual",
    )(item_tile, item_chunk, item_flags, yg, start, cnt, x, mod3, norm_g.reshape(1, d))


def expert_choice_ffn_residual(x, h, logits_t, mod3, rows, k_gate, w1, w3, w2, layer, norm_g, *, batch, sets,
                               normalize):
    e_n = logits_t.shape[0]
    d = x.shape[1]
    boff = jnp.arange(batch, dtype=I32)[:, None, None]
    routed, src_rows, pair_rows, gate_cols, pair_row0 = [], [], [], [], 0
    for n, row0 in sets:
        idx, dest, gate, start, cnt = expert_choice_route(logits_t, batch=batch, n=n, col0=row0)
        cap = idx.shape[2]
        pairs = e_n * cap
        src_rows.append(jnp.transpose(idx[..., 0] + row0 + boff * n, (1, 0, 2)).reshape(e_n, batch * cap))
        pair_rows.append(jnp.transpose(dest[..., 0] + pair_row0 + boff * pairs, (1, 0, 2)).reshape(e_n, batch * cap))
        gate_cols.append(jnp.transpose(gate, (1, 0, 2, 3)).reshape(e_n, batch * cap, 1))
        routed.append((start, cnt, n, row0, pairs, pair_row0))
        pair_row0 += batch * pairs
    src_rows = jnp.concatenate(src_rows, axis=1)
    m = src_rows.shape[1]
    xg = gather_rows(h, src_rows.reshape(-1), BF16).reshape(e_n, m, d)
    y = expert_ffn(xg, w1, w3, w2, jnp.concatenate(gate_cols, axis=1), layer).reshape(e_n * m, d)
    yg = scatter_rows(y, jnp.concatenate(pair_rows, axis=1).reshape(-1), F32)
    for start, cnt, n, row0, pairs, pair_row0 in routed:
        x = combine_residual(yg, start, cnt, x, mod3, rows, k_gate, norm_g, batch=batch, n=n, row0=row0,
                             pairs=pairs, pair_row0=pair_row0, normalize=normalize)
    return x


def kernel(x, c, ctx, c_ctx, ada_w, ada_b, norm1_g, norm2_g, w_in, na_rpb, gmlp_norm_g, gmlp_ws, gmlp_bs,
           w_branch_a, w_branch_b, w_out, router_w, exp_w1, exp_w3, exp_w2, final_norm_g):
    batch, n_lat, d = x.shape
    n_ctx = ctx.shape[1]
    depth = ada_w.shape[0]
    rows = Rows(batch, n_lat, n_ctx)
    lat_rows, all_rows = rows.lat_rows, rows.all_rows
    off_k, off_v, off_u = NA_WIDTH, 2 * NA_WIDTH, 3 * NA_WIDTH
    off_gv = off_u + GMLP_WIDTH
    off_ga = off_gv + GMLP_WIDTH
    off_gb = off_ga + d
    in_cols = off_gb + d

    x_head, x_tail = x.reshape(lat_rows, d), ctx.reshape(batch * n_ctx, d)
    cv = jnp.zeros((MOD_ROWS, d), F32).at[:batch].set(c).at[batch].set(c_ctx)

    for layer in range(depth):
        last = layer == depth - 1
        nrows = lat_rows if last else all_rows
        mod3 = ada_mod(cv, ada_w, ada_b, layer).reshape(MOD_ROWS, 1, N_MOD * d)

        h = norm_mod(x_head, x_tail, norm1_g, layer, mod3, rows, 0, 1, 0, nrows, BF16)
        qkv = rest = matmul(h, w_in, layer, row0=0, nrows=nrows, col0=0, ncols=in_cols, out_dtype=BF16,
                            scaled_cols=NA_WIDTH, scale=NA_HEAD_DIM ** -0.5)
        bias_table = build_col_table(na_rpb[layer])
        if last:
            h_ctx = norm_mod(x_head, x_tail, norm1_g, layer, mod3, rows, 0, 1, lat_rows, batch * n_ctx, BF16)
            kvc = matmul(h_ctx, w_in, layer, row0=0, nrows=batch * n_ctx, col0=off_k, ncols=2 * NA_WIDTH,
                         out_dtype=BF16)
            o_a = neighbourhood_attention(qkv, kvc, bias_table, batch=batch, n_lat=n_lat, n_ctx=n_ctx,
                                          kc_row0=0, kc_col0=0, vc_col0=NA_WIDTH)
        else:
            o_lat = neighbourhood_attention(qkv, qkv, bias_table, batch=batch, n_lat=n_lat, n_ctx=n_ctx,
                                            kc_row0=lat_rows, kc_col0=off_k, vc_col0=off_v)
            o_ctx = context_attention(qkv, batch=batch, n_ctx=n_ctx, row0=lat_rows)
            o_a = jnp.concatenate([o_lat, o_ctx], axis=0)
        o_b = spatial_gating(rest, gmlp_norm_g, gmlp_ws, gmlp_bs, layer, nrows=nrows, u_col0=off_u, gv_col0=off_gv)
        y = gated_merge(o_a, o_b, w_branch_a, w_branch_b, rest, layer, nrows=nrows, ga_col0=off_ga, gb_col0=off_gb)
        stream = outproj_residual(y, w_out, x_head, x_tail, mod3, layer, rows, 2, nrows=nrows)

        h2, logits_t = norm_mod_router(stream, norm2_g, layer, mod3, rows, 3, 4, router_w, nrows)
        sets = [(n_lat, 0)] if last else [(n_lat, 0), (n_ctx, lat_rows)]
        stream = expert_choice_ffn_residual(stream, h2, logits_t, mod3, rows, 5, exp_w1, exp_w3, exp_w2, layer,
                                            final_norm_g, batch=batch, sets=sets, normalize=last)
        x_head = x_tail = stream

    return stream.reshape(batch, n_lat, d)
```
